```python
import math
import jax, jax.numpy as jnp
from jax import lax
import numpy as np

D_MODEL = 1024
BATCH = 2
SEQ = 8192
DEPTH = 2

CHUNK = 64
N_MEM = 256
SB_HEADS = 8
SB_HEAD_DIM = 64
SB_WIDTH = SB_HEADS * SB_HEAD_DIM
SB_BLOCK = 128
CONV_CH = 256
CONV_WIDTH = 31
SSM_CH = 256
SSM_GROUP = 16
SSM_GROUPS = SSM_CH // SSM_GROUP
SSM_STATE = 64
MIX_WIDTH = SB_WIDTH + CONV_CH + SSM_CH
IN_PROJ = 3 * SB_WIDTH + 2 * CONV_CH + SSM_CH
XA_HEADS = 4
XA_HEAD_DIM = D_MODEL // XA_HEADS
XA_WIDTH = XA_HEADS * XA_HEAD_DIM
FFN_HIDDEN = ((int(math.ceil(8 * D_MODEL / 3)) + 255) // 256) * 256
EPS = 1e-6

kernel_name = "hybrid_sb_conformer_s5_encoder"


def rms_norm(x, g):
    xf = x.astype(jnp.float32)
    y = xf * lax.rsqrt(jnp.mean(xf * xf, axis=-1, keepdims=True) + EPS)
    return (y * g.astype(jnp.float32)).astype(x.dtype)


def layer_norm(x, g, b):
    xf = x.astype(jnp.float32)
    mu = jnp.mean(xf, axis=-1, keepdims=True)
    var = jnp.mean(jnp.square(xf - mu), axis=-1, keepdims=True)
    y = (xf - mu) * lax.rsqrt(var + EPS)
    return (y * g.astype(jnp.float32) + b.astype(jnp.float32)).astype(x.dtype)


def stick_breaking_attention(q, k, v):
    bsz, L, H, hd = q.shape
    nb = L // SB_BLOCK
    kh = k.transpose(0, 2, 1, 3)
    vh = v.transpose(0, 2, 1, 3)
    q_blocks = q.transpose(0, 2, 1, 3).reshape(bsz, H, nb, SB_BLOCK, hd).transpose(2, 0, 1, 3, 4)
    key_pos = jnp.arange(L)
    scale = hd ** -0.5

    def one_block(args):
        qb, blk = args
        z = jnp.einsum('bhqd,bhkd->bhqk', qb, kh).astype(jnp.float32) * scale
        q_pos = blk * SB_BLOCK + jnp.arange(SB_BLOCK)
        mask = key_pos[None, :] < q_pos[:, None]
        log_1mb = jnp.where(mask, jax.nn.log_sigmoid(-z), 0.0)
        later = lax.cumsum(log_1mb, axis=3, reverse=True) - log_1mb
        w = jnp.where(mask, jnp.exp(jax.nn.log_sigmoid(z) + later), 0.0)
        return jnp.einsum('bhqk,bhkd->bhqd', w.astype(vh.dtype), vh)

    out = lax.map(one_block, (q_blocks, jnp.arange(nb)))
    return out.transpose(1, 0, 3, 2, 4).reshape(bsz, L, H * hd)


def conformer_conv(u2, dw_w, dw_b, ln_g, ln_b, pw2_w):
    a, b = jnp.split(u2, 2, axis=-1)
    h = a * jax.nn.sigmoid(b)
    h = lax.conv_general_dilated(
        h, dw_w[:, None, :].astype(h.dtype), window_strides=(1,),
        padding=((CONV_WIDTH - 1, 0),), dimension_numbers=('NWC', 'WIO', 'NWC'),
        feature_group_count=CONV_CH) + dw_b
    h = jax.nn.silu(layer_norm(h, ln_g, ln_b))
    return h @ pw2_w


def s5_ssm(u, lam_re, lam_im, log_dt, b_re, b_im, c_re, c_im, d, glu_w):
    bsz, L, _ = u.shape
    f32 = jnp.float32
    uf = u.astype(f32).reshape(bsz, L, SSM_GROUPS, SSM_GROUP)
    lr, li = lam_re.astype(f32), lam_im.astype(f32)
    dt = jnp.exp(log_dt.astype(f32))[:, None]
    mag = jnp.exp(lr * dt)
    ar, ai = mag * jnp.cos(li * dt), mag * jnp.sin(li * dt)
    den = lr * lr + li * li
    fr = ((ar - 1.0) * lr + ai * li) / den
    fi = (ai * lr - (ar - 1.0) * li) / den
    br, bi = b_re.astype(f32), b_im.astype(f32)
    bbr = fr[..., None] * br - fi[..., None] * bi
    bbi = fr[..., None] * bi + fi[..., None] * br
    bu_r = jnp.einsum('blgh,gph->blgp', uf, bbr)
    bu_i = jnp.einsum('blgh,gph->blgp', uf, bbi)
    shape = bu_r.shape
    a_r = jnp.broadcast_to(ar, shape)
    a_i = jnp.broadcast_to(ai, shape)

    def combine(e1, e2):
        a1r, a1i, b1r, b1i = e1
        a2r, a2i, b2r, b2i = e2
        return (a2r * a1r - a2i * a1i,
                a2r * a1i + a2i * a1r,
                a2r * b1r - a2i * b1i + b2r,
                a2r * b1i + a2i * b1r + b2i)

    _, _, xr, xi = lax.associative_scan(combine, (a_r, a_i, bu_r, bu_i), axis=1)
    y = (jnp.einsum('blgp,ghp->blgh', xr, c_re.astype(f32))
         - jnp.einsum('blgp,ghp->blgh', xi, c_im.astype(f32)))
    y = y.reshape(bsz, L, SSM_CH) + d.astype(f32) * uf.reshape(bsz, L, SSM_CH)
    y = y.astype(u.dtype)
    ya, yb = jnp.split(y @ glu_w, 2, axis=-1)
    return ya * jax.nn.sigmoid(yb)


def memory_cross_attention(h, m, wq, wk, wv, q_g, k_g, wo):
    bsz, L, _ = h.shape
    q = rms_norm((h @ wq).reshape(bsz, L, XA_HEADS, XA_HEAD_DIM), q_g)
    k = rms_norm((m @ wk).reshape(bsz, N_MEM, XA_HEADS, XA_HEAD_DIM), k_g)
    v = (m @ wv).reshape(bsz, N_MEM, XA_HEADS, XA_HEAD_DIM)
    s = jnp.einsum('blhd,bmhd->bhlm', q, k).astype(jnp.float32) * (XA_HEAD_DIM ** -0.5)
    p = jax.nn.softmax(s, axis=-1).astype(v.dtype)
    o = jnp.einsum('bhlm,bmhd->blhd', p, v).reshape(bsz, L, XA_WIDTH)
    return o @ wo


def setup_inputs(seed: int = 0) -> dict:
    key = jax.random.key(seed)
    ks = iter(jax.random.split(key, 40))
    f32 = jnp.float32

    def nrm(shape, scale):
        return jax.random.normal(next(ks), shape, f32) * scale

    def gain(shape):
        return 1.0 + 0.02 * jax.random.normal(next(ks), shape, f32)

    n_idx = jnp.arange(SSM_STATE, dtype=f32)
    return {
        "x": nrm((BATCH, SEQ, D_MODEL), 1.0),
        "mem": nrm((BATCH, N_MEM, D_MODEL), 1.0),
        "norm_mix_g": gain((DEPTH, D_MODEL)),
        "w_in": nrm((DEPTH, D_MODEL, IN_PROJ), D_MODEL ** -0.5),
        "sb_q_norm_g": gain((DEPTH, SB_HEAD_DIM)),
        "sb_k_norm_g": gain((DEPTH, SB_HEAD_DIM)),
        "conv_dw_w": nrm((DEPTH, CONV_WIDTH, CONV_CH), CONV_WIDTH ** -0.5),
        "conv_dw_b": nrm((DEPTH, CONV_CH), 0.02),
        "conv_ln_g": gain((DEPTH, CONV_CH)),
        "conv_ln_b": nrm((DEPTH, CONV_CH), 0.02),
        "conv_pw2_w": nrm((DEPTH, CONV_CH, CONV_CH), CONV_CH ** -0.5),
        "ssm_lam_re": -0.5 * jnp.exp(nrm((DEPTH, SSM_GROUPS, SSM_STATE), 0.05)),
        "ssm_lam_im": jnp.pi * n_idx * jnp.exp(nrm((DEPTH, SSM_GROUPS, SSM_STATE), 0.01)),
        "ssm_log_dt": jax.random.uniform(next(ks), (DEPTH, SSM_GROUPS), f32,
                                         math.log(1e-3), math.log(1e-1)),
        "ssm_b_re": nrm((DEPTH, SSM_GROUPS, SSM_STATE, SSM_GROUP), (2 * SSM_GROUP) ** -0.5),
        "ssm_b_im": nrm((DEPTH, SSM_GROUPS, SSM_STATE, SSM_GROUP), (2 * SSM_GROUP) ** -0.5),
        "ssm_c_re": nrm((DEPTH, SSM_GROUPS, SSM_GROUP, SSM_STATE), (2 * SSM_STATE) ** -0.5),
        "ssm_c_im": nrm((DEPTH, SSM_GROUPS, SSM_GROUP, SSM_STATE), (2 * SSM_STATE) ** -0.5),
        "ssm_d": nrm((DEPTH, SSM_CH), 1.0),
        "ssm_glu_w": nrm((DEPTH, SSM_CH, 2 * SSM_CH), SSM_CH ** -0.5),
        "branch_norm_g": gain((DEPTH, MIX_WIDTH)),
        "w_out": nrm((DEPTH, MIX_WIDTH, D_MODEL), MIX_WIDTH ** -0.5),
        "norm_xa_g": gain((DEPTH, D_MODEL)),
        "norm_mem_g": gain((DEPTH, D_MODEL)),
        "xa_wq": nrm((DEPTH, D_MODEL, XA_WIDTH), D_MODEL ** -0.5),
        "xa_wk": nrm((DEPTH, D_MODEL, XA_WIDTH), D_MODEL ** -0.5),
        "xa_wv": nrm((DEPTH, D_MODEL, XA_WIDTH), D_MODEL ** -0.5),
        "xa_q_norm_g": gain((DEPTH, XA_HEAD_DIM)),
        "xa_k_norm_g": gain((DEPTH, XA_HEAD_DIM)),
        "xa_wo": nrm((DEPTH, XA_WIDTH, D_MODEL), XA_WIDTH ** -0.5),
        "norm_ffn_g": gain((DEPTH, D_MODEL)),
        "ffn_w_in": nrm((DEPTH, D_MODEL, 2 * FFN_HIDDEN), D_MODEL ** -0.5),
        "ffn_w_out": nrm((DEPTH, FFN_HIDDEN, D_MODEL), FFN_HIDDEN ** -0.5),
    }


def reference(x, mem, norm_mix_g, w_in, sb_q_norm_g, sb_k_norm_g, conv_dw_w, conv_dw_b,
              conv_ln_g, conv_ln_b, conv_pw2_w, ssm_lam_re, ssm_lam_im, ssm_log_dt,
              ssm_b_re, ssm_b_im, ssm_c_re, ssm_c_im, ssm_d, ssm_glu_w, branch_norm_g,
              w_out, norm_xa_g, norm_mem_g, xa_wq, xa_wk, xa_wv, xa_q_norm_g, xa_k_norm_g,
              xa_wo, norm_ffn_g, ffn_w_in, ffn_w_out):
    bsz, L, _ = x.shape
    s1 = SB_WIDTH
    s2 = 2 * SB_WIDTH
    s3 = 3 * SB_WIDTH
    s4 = s3 + 2 * CONV_CH
    for l in range(DEPTH):
        h = rms_norm(x, norm_mix_g[l])
        p = h @ w_in[l]
        q = rms_norm(p[..., :s1].reshape(bsz, L, SB_HEADS, SB_HEAD_DIM), sb_q_norm_g[l])
        k = rms_norm(p[..., s1:s2].reshape(bsz, L, SB_HEADS, SB_HEAD_DIM), sb_k_norm_g[l])
        v = p[..., s2:s3].reshape(bsz, L, SB_HEADS, SB_HEAD_DIM)
        o_sb = stick_breaking_attention(q, k, v)
        o_conv = conformer_conv(p[..., s3:s4], conv_dw_w[l], conv_dw_b[l],
                                conv_ln_g[l], conv_ln_b[l], conv_pw2_w[l])
        o_ssm = s5_ssm(p[..., s4:], ssm_lam_re[l], ssm_lam_im[l], ssm_log_dt[l],
                       ssm_b_re[l], ssm_b_im[l], ssm_c_re[l], ssm_c_im[l],
                       ssm_d[l], ssm_glu_w[l])
        g = branch_norm_g[l]
        mixed = jnp.concatenate([
            rms_norm(o_sb, g[:SB_WIDTH]),
            rms_norm(o_conv, g[SB_WIDTH:SB_WIDTH + CONV_CH]),
            rms_norm(o_ssm, g[SB_WIDTH + CONV_CH:]),
        ], axis=-1)
        x = x + mixed @ w_out[l]
        hx = rms_norm(x, norm_xa_g[l])
        hm = rms_norm(mem, norm_mem_g[l])
        x = x + memory_cross_attention(hx, hm, xa_wq[l], xa_wk[l], xa_wv[l],
                                       xa_q_norm_g[l], xa_k_norm_g[l], xa_wo[l])
        hf = rms_norm(x, norm_ffn_g[l])
        gate, up = jnp.split(hf @ ffn_w_in[l], 2, axis=-1)
        x = x + (jax.nn.silu(gate) * up) @ ffn_w_out[l]
    return x
```

```python
import functools
import math

import jax
import jax.numpy as jnp
from jax import lax
from jax.experimental import pallas as pl
from jax.experimental.pallas import tpu as pltpu

F32 = jnp.float32
BF16 = jnp.bfloat16
EPS = 1e-6

V7X_LANES = 128
V7X_SUBLANES = 8
V7X_VMEM_BYTES = 64 * 1024 * 1024

SB_HEADS = 8
SB_HEAD_DIM = 64
SB_WIDTH = SB_HEADS * SB_HEAD_DIM
CONV_CH = 256
CONV_WIDTH = 31
CONV_HALO = 32
SSM_CH = 256
SSM_GROUP = 16
SSM_GROUPS = SSM_CH // SSM_GROUP
SSM_STATE = 64
SSM_CHUNK = 16
XA_HEADS = 4
XA_HEAD_DIM = 256


def _vmem_limit(nbytes):
    return int(min(max(nbytes * 3 // 2, 16 * 1024 * 1024), V7X_VMEM_BYTES - 8 * 1024 * 1024))


def _params(semantics, nbytes):
    return pltpu.CompilerParams(dimension_semantics=semantics, vmem_limit_bytes=_vmem_limit(nbytes))


def _dot(a, b):
    return jnp.dot(a, b, preferred_element_type=F32)


def _dot_nt(a, b):
    return lax.dot_general(a, b, (((1,), (1,)), ((), ())), preferred_element_type=F32)


def _rms_rows(xf, g):
    return xf * lax.rsqrt(jnp.mean(xf * xf, axis=-1, keepdims=True) + EPS) * g


def _sigmoid(x):
    return 1.0 / (1.0 + jnp.exp(-x))


def _segment_mean_sq(p, seg):
    sq = p * p
    hi = sq.astype(BF16)
    lo = (sq - hi.astype(F32)).astype(BF16)
    return (_dot(hi, seg) + _dot(lo, seg)) * (1.0 / SB_HEAD_DIM)


def _mix_in_kernel(x_ref, g_ref, w_ref, qg_ref, kg_ref, seg_ref,
                   q_ref, k_ref, v_ref, c_ref, u_ref):
    h = _rms_rows(x_ref[...], g_ref[...]).astype(BF16)
    s1, s2, s3 = SB_WIDTH, 2 * SB_WIDTH, 3 * SB_WIDTH
    s4 = s3 + 2 * CONV_CH
    seg = seg_ref[...]
    pq = _dot(h, w_ref[:, 0:s1])
    q_ref[...] = (pq * lax.rsqrt(_segment_mean_sq(pq, seg) + EPS) * qg_ref[...]).astype(BF16)
    pk = _dot(h, w_ref[:, s1:s2])
    k_ref[...] = (pk * lax.rsqrt(_segment_mean_sq(pk, seg) + EPS) * kg_ref[...]).astype(BF16)
    v_ref[...] = _dot(h, w_ref[:, s2:s3]).astype(BF16)
    c_ref[...] = _dot(h, w_ref[:, s3:s4])
    u_ref[...] = _dot(h, w_ref[:, s4:])


def _mix_in(x2, g, w_bf, qg, kg, seg, tm):
    t, d = x2.shape
    n_in = w_bf.shape[1]
    row = lambda i: (i, 0)
    const = lambda i: (0, 0)
    nbytes = 2 * (tm * d * 4 + d * n_in * 2 + tm * (3 * SB_WIDTH * 2 + 2 * CONV_CH * 4 + SSM_CH * 4)) + tm * n_in * 4
    return pl.pallas_call(
        _mix_in_kernel,
        grid=(t // tm,),
        in_specs=[pl.BlockSpec((tm, d), row), pl.BlockSpec((1, d), const),
                  pl.BlockSpec((d, n_in), const), pl.BlockSpec((1, SB_WIDTH), const),
                  pl.BlockSpec((1, SB_WIDTH), const), pl.BlockSpec((SB_WIDTH, SB_WIDTH), const)],
        out_specs=[pl.BlockSpec((tm, SB_WIDTH), row), pl.BlockSpec((tm, SB_WIDTH), row),
                   pl.BlockSpec((tm, SB_WIDTH), row), pl.BlockSpec((tm, 2 * CONV_CH), row),
                   pl.BlockSpec((tm, SSM_CH), row)],
        out_shape=[jax.ShapeDtypeStruct((t, SB_WIDTH), BF16), jax.ShapeDtypeStruct((t, SB_WIDTH), BF16),
                   jax.ShapeDtypeStruct((t, SB_WIDTH), BF16), jax.ShapeDtypeStruct((t, 2 * CONV_CH), F32),
                   jax.ShapeDtypeStruct((t, SSM_CH), F32)],
        compiler_params=_params(("parallel",), nbytes),
        name="mix_in",
    )(x2, g, w_bf, qg, kg, seg)


def _sb_block(q, k, v, tri, carry, diag_mask):
    acc, run = carry
    z = _dot_nt(q, k)
    sp = jnp.maximum(z, 0.0) + jnp.log(1.0 + jnp.exp(-jnp.abs(z)))
    lm = -sp
    if diag_mask is not None:
        lm = jnp.where(diag_mask, lm, 0.0)
    hi = lm.astype(BF16)
    lo = (lm - hi.astype(F32)).astype(BF16)
    later = _dot(hi, tri) + _dot(lo, tri)
    w = jnp.exp(z - sp + later + run)
    if diag_mask is not None:
        w = jnp.where(diag_mask, w, 0.0)
    acc = acc + _dot(w.astype(BF16), v)
    run = run + jnp.sum(lm, axis=1, keepdims=True)
    return acc, run


def _sb_attn_kernel(q_ref, k_ref, v_ref, tri_ref, o_ref, *, tq):
    i = pl.program_id(2)
    tri = tri_ref[...]
    row = lax.broadcasted_iota(jnp.int32, (tq, tq), 0)
    col = lax.broadcasted_iota(jnp.int32, (tq, tq), 1)
    diag_mask = col < row
    heads = [slice(hh * SB_HEAD_DIM, (hh + 1) * SB_HEAD_DIM) for hh in range(V7X_LANES // SB_HEAD_DIM)]
    qs = [q_ref[0, :, hs] for hs in heads]
    start = pl.multiple_of(i * tq, tq)
    carries = []
    for hs, q in zip(heads, qs):
        init = (jnp.zeros((tq, SB_HEAD_DIM), F32), jnp.zeros((tq, 1), F32))
        carries.append(_sb_block(q, k_ref[0, pl.ds(start, tq), hs], v_ref[0, pl.ds(start, tq), hs],
                                 tri, init, diag_mask))

    def body(n, cs):
        st = pl.multiple_of((i - 1 - n) * tq, tq)
        return tuple(_sb_block(q, k_ref[0, pl.ds(st, tq), hs], v_ref[0, pl.ds(st, tq), hs], tri, c, None)
                     for hs, q, c in zip(heads, qs, cs))

    carries = lax.fori_loop(0, i, body, tuple(carries))
    for hs, (acc, _) in zip(heads, carries):
        o_ref[0, :, hs] = acc


def _sb_attn(q, k, v, tri, tq):
    b, l, w = q.shape
    nbytes = 2 * (2 * l * V7X_LANES * 2 + tq * V7X_LANES * 2 + tq * V7X_LANES * 4) + 12 * tq * tq * 4
    return pl.pallas_call(
        functools.partial(_sb_attn_kernel, tq=tq),
        grid=(b, w // V7X_LANES, l // tq),
        in_specs=[pl.BlockSpec((1, tq, V7X_LANES), lambda bb, hp, i: (bb, i, hp)),
                  pl.BlockSpec((1, l, V7X_LANES), lambda bb, hp, i: (bb, 0, hp)),
                  pl.BlockSpec((1, l, V7X_LANES), lambda bb, hp, i: (bb, 0, hp)),
                  pl.BlockSpec((tq, tq), lambda bb, hp, i: (0, 0))],
        out_specs=pl.BlockSpec((1, tq, V7X_LANES), lambda bb, hp, i: (bb, i, hp)),
        out_shape=jax.ShapeDtypeStruct((b, l, w), F32),
        compiler_params=_params(("parallel", "parallel", "arbitrary"), nbytes),
        name="sb_attn",
    )(q, k, v, tri)


def _conv_kernel(cur_ref, prev_ref, dww_ref, dwb_ref, lng_ref, lnb_ref, pw_ref, bg_ref, o_ref, hbuf, *, tc):
    i = pl.program_id(1)
    cur = cur_ref[0]
    prev = prev_ref[0]
    hbuf[CONV_HALO:, :] = cur[:, :CONV_CH] * _sigmoid(cur[:, CONV_CH:])
    hprev = prev[:, :CONV_CH] * _sigmoid(prev[:, CONV_CH:])
    hbuf[:CONV_HALO, :] = jnp.where(i > 0, hprev, 0.0)
    acc = jnp.zeros((tc, CONV_CH), F32) + dwb_ref[...]
    off = CONV_HALO - (CONV_WIDTH - 1)
    for j in range(CONV_WIDTH):
        acc = acc + dww_ref[j:j + 1, :] * hbuf[off + j:off + j + tc, :]
    mu = jnp.mean(acc, axis=-1, keepdims=True)
    cen = acc - mu
    var = jnp.mean(cen * cen, axis=-1, keepdims=True)
    y = cen * lax.rsqrt(var + EPS) * lng_ref[...] + lnb_ref[...]
    y = y * _sigmoid(y)
    o = _dot(y.astype(BF16), pw_ref[...])
    o_ref[0] = _rms_rows(o, bg_ref[...])


def _conv_branch(c, dww, dwb, lng, lnb, pw_bf, bg, tc):
    b, l, w = c.shape
    per_tile = tc // CONV_HALO
    const = lambda bb, i: (0, 0)
    nbytes = 2 * (tc * w * 4 + CONV_HALO * w * 4 + tc * CONV_CH * 4) + (tc + CONV_HALO) * CONV_CH * 4 * 6
    return pl.pallas_call(
        functools.partial(_conv_kernel, tc=tc),
        grid=(b, l // tc),
        in_specs=[pl.BlockSpec((1, tc, w), lambda bb, i: (bb, i, 0)),
                  pl.BlockSpec((1, CONV_HALO, w), lambda bb, i: (bb, jnp.maximum(i * per_tile - 1, 0), 0)),
                  pl.BlockSpec((CONV_WIDTH, CONV_CH), const), pl.BlockSpec((1, CONV_CH), const),
                  pl.BlockSpec((1, CONV_CH), const), pl.BlockSpec((1, CONV_CH), const),
                  pl.BlockSpec((CONV_CH, CONV_CH), const), pl.BlockSpec((1, CONV_CH), const)],
        out_specs=pl.BlockSpec((1, tc, CONV_CH), lambda bb, i: (bb, i, 0)),
        out_shape=jax.ShapeDtypeStruct((b, l, CONV_CH), F32),
        scratch_shapes=[pltpu.VMEM((tc + CONV_HALO, CONV_CH), F32)],
        compiler_params=_params(("parallel", "parallel"), nbytes),
        name="conv_branch",
    )(c, c, dww, dwb, lng, lnb, pw_bf, bg)


def _ssm_operators(lam_re, lam_im, log_dt, b_re, b_im, c_re, c_im, d):
    hp = lax.Precision.HIGHEST
    lr, li = lam_re.astype(F32), lam_im.astype(F32)
    dt = jnp.exp(log_dt.astype(F32))[:, None]
    mag = jnp.exp(lr * dt)
    ar, ai = mag * jnp.cos(li * dt), mag * jnp.sin(li * dt)
    den = lr * lr + li * li
    fr = ((ar - 1.0) * lr + ai * li) / den
    fi = (ai * lr - (ar - 1.0) * li) / den
    br, bi = b_re.astype(F32), b_im.astype(F32)
    bbr = fr[..., None] * br - fi[..., None] * bi
    bbi = fr[..., None] * bi + fi[..., None] * br
    cr, ci = c_re.astype(F32), c_im.astype(F32)
    n = jnp.arange(SSM_CHUNK + 1, dtype=F32)[:, None, None]
    pmag = jnp.exp(n * (lr * dt)[None])
    pr, pi = pmag * jnp.cos(n * (li * dt)[None]), pmag * jnp.sin(n * (li * dt)[None])
    abr = pr[:, :, :, None] * bbr[None] - pi[:, :, :, None] * bbi[None]
    abi = pr[:, :, :, None] * bbi[None] + pi[:, :, :, None] * bbr[None]
    kern = (jnp.einsum('ngpi,gop->ngio', abr, cr, precision=hp)
            - jnp.einsum('ngpi,gop->ngio', abi, ci, precision=hp))
    s_idx = jnp.arange(SSM_CHUNK)[:, None]
    t_idx = jnp.arange(SSM_CHUNK)[None, :]
    lag = t_idx - s_idx
    k_st = kern[jnp.clip(lag, 0, SSM_CHUNK)]
    k_st = jnp.where((lag >= 0)[:, :, None, None, None], k_st, 0.0)
    w_intra = k_st.transpose(2, 0, 3, 1, 4).reshape(SSM_GROUPS, SSM_CHUNK * SSM_GROUP, SSM_CHUNK * SSM_GROUP)
    rev = SSM_CHUNK - 1 - jnp.arange(SSM_CHUNK)
    w_st_r = abr[rev].transpose(1, 0, 3, 2).reshape(SSM_GROUPS, SSM_CHUNK * SSM_GROUP, SSM_STATE)
    w_st_i = abi[rev].transpose(1, 0, 3, 2).reshape(SSM_GROUPS, SSM_CHUNK * SSM_GROUP, SSM_STATE)
    p1r, p1i = pr[1:], pi[1:]
    o_r = cr[None] * p1r[:, :, None, :] - ci[None] * p1i[:, :, None, :]
    o_i = -(cr[None] * p1i[:, :, None, :] + ci[None] * p1r[:, :, None, :])
    w_out_r = o_r.transpose(1, 3, 0, 2).reshape(SSM_GROUPS, SSM_STATE, SSM_CHUNK * SSM_GROUP)
    w_out_i = o_i.transpose(1, 3, 0, 2).reshape(SSM_GROUPS, SSM_STATE, SSM_CHUNK * SSM_GROUP)
    a_n = jnp.stack([pr[SSM_CHUNK].reshape(1, -1), pi[SSM_CHUNK].reshape(1, -1)])
    d_c = jnp.tile(d.astype(F32).reshape(SSM_GROUPS, 1, SSM_GROUP), (1, 1, SSM_CHUNK))
    return (w_intra.astype(BF16), w_st_r.astype(BF16), w_st_i.astype(BF16),
            w_out_r.astype(BF16), w_out_i.astype(BF16), a_n, d_c)


def _ssm_kernel(u_ref, wi_ref, wsr_ref, wsi_ref, wor_ref, woi_ref, an_ref, dc_ref, y_ref,
                sr_scr, si_scr, xr_scr, xi_scr, *, nc):
    p = SSM_STATE
    for g in range(SSM_GROUPS):
        ub = u_ref[0, g].astype(BF16)
        sr_scr[:, g * p:(g + 1) * p] = _dot(ub, wsr_ref[g])
        si_scr[:, g * p:(g + 1) * p] = _dot(ub, wsi_ref[g])
    ar = an_ref[0]
    ai = an_ref[1]

    def body(c8, carry):
        xr, xi = carry
        r0 = pl.multiple_of(c8 * V7X_SUBLANES, V7X_SUBLANES)
        sr = sr_scr[pl.ds(r0, V7X_SUBLANES), :]
        si = si_scr[pl.ds(r0, V7X_SUBLANES), :]
        rows_r, rows_i = [], []
        for r in range(V7X_SUBLANES):
            rows_r.append(xr)
            rows_i.append(xi)
            xr, xi = (ar * xr - ai * xi + sr[r:r + 1, :], ar * xi + ai * xr + si[r:r + 1, :])
        xr_scr[pl.ds(r0, V7X_SUBLANES), :] = jnp.concatenate(rows_r, axis=0)
        xi_scr[pl.ds(r0, V7X_SUBLANES), :] = jnp.concatenate(rows_i, axis=0)
        return xr, xi

    zero = jnp.zeros((1, SSM_GROUPS * p), F32)
    lax.fori_loop(0, nc // V7X_SUBLANES, body, (zero, zero))
    for g in range(SSM_GROUPS):
        uf = u_ref[0, g]
        y = _dot(uf.astype(BF16), wi_ref[g])
        y = y + _dot(xr_scr[:, g * p:(g + 1) * p].astype(BF16), wor_ref[g])
        y = y + _dot(xi_scr[:, g * p:(g + 1) * p].astype(BF16), woi_ref[g])
        y_ref[0, g] = y + dc_ref[g] * uf


def _ssm_scan(uc, ops):
    w_intra, w_st_r, w_st_i, w_out_r, w_out_i, a_n, d_c = ops
    b, g, nc, cw = uc.shape
    gp = SSM_GROUPS * SSM_STATE
    full = lambda shape: pl.BlockSpec(shape, lambda bb: (0,) * len(shape))
    nbytes = 4 * g * nc * cw * 4 + 4 * nc * gp * 4 + 4 * g * cw * cw * 2
    return pl.pallas_call(
        functools.partial(_ssm_kernel, nc=nc),
        grid=(b,),
        in_specs=[pl.BlockSpec((1, g, nc, cw), lambda bb: (bb, 0, 0, 0)),
                  full(w_intra.shape), full(w_st_r.shape), full(w_st_i.shape),
                  full(w_out_r.shape), full(w_out_i.shape), full(a_n.shape), full(d_c.shape)],
        out_specs=pl.BlockSpec((1, g, nc, cw), lambda bb: (bb, 0, 0, 0)),
        out_shape=jax.ShapeDtypeStruct((b, g, nc, cw), F32),
        scratch_shapes=[pltpu.VMEM((nc, gp), F32) for _ in range(4)],
        compiler_params=_params(("parallel",), nbytes),
        name="ssm_scan",
    )(uc, w_intra, w_st_r, w_st_i, w_out_r, w_out_i, a_n, d_c)


def _ssm_glu_kernel(y_ref, w_ref, bg_ref, o_ref):
    z = _dot(y_ref[...].astype(BF16), w_ref[...])
    o = z[:, :SSM_CH] * _sigmoid(z[:, SSM_CH:])
    o_ref[...] = _rms_rows(o, bg_ref[...])


def _ssm_glu(y2, w_bf, bg, tm):
    t = y2.shape[0]
    row = lambda i: (i, 0)
    const = lambda i: (0, 0)
    nbytes = 2 * (2 * tm * SSM_CH * 4 + SSM_CH * 2 * SSM_CH * 2) + 4 * tm * SSM_CH * 4
    return pl.pallas_call(
        _ssm_glu_kernel,
        grid=(t // tm,),
        in_specs=[pl.BlockSpec((tm, SSM_CH), row), pl.BlockSpec((SSM_CH, 2 * SSM_CH), const),
                  pl.BlockSpec((1, SSM_CH), const)],
        out_specs=pl.BlockSpec((tm, SSM_CH), row),
        out_shape=jax.ShapeDtypeStruct((t, SSM_CH), F32),
        compiler_params=_params(("parallel",), nbytes),
        name="ssm_glu",
    )(y2, w_bf, bg)


def _mix_out_kernel(x_ref, sb_ref, cv_ref, sm_ref, bg_ref, w_ref, o_ref):
    sb = _rms_rows(sb_ref[...], bg_ref[...]).astype(BF16)
    s1 = SB_WIDTH
    s2 = SB_WIDTH + CONV_CH
    y = _dot(sb, w_ref[0:s1, :])
    y = y + _dot(cv_ref[...].astype(BF16), w_ref[s1:s2, :])
    y = y + _dot(sm_ref[...].astype(BF16), w_ref[s2:, :])
    o_ref[...] = x_ref[...] + y


def _mix_out(x2, sb2, cv2, sm2, bg_sb, w_bf, tm):
    t, d = x2.shape
    row = lambda i: (i, 0)
    const = lambda i: (0, 0)
    nbytes = 2 * (2 * tm * d * 4 + tm * (SB_WIDTH + CONV_CH + SSM_CH) * 4 + d * d * 2) + 2 * tm * d * 4
    return pl.pallas_call(
        _mix_out_kernel,
        grid=(t // tm,),
        in_specs=[pl.BlockSpec((tm, d), row), pl.BlockSpec((tm, SB_WIDTH), row),
                  pl.BlockSpec((tm, CONV_CH), row), pl.BlockSpec((tm, SSM_CH), row),
                  pl.BlockSpec((1, SB_WIDTH), const), pl.BlockSpec(w_bf.shape, const)],
        out_specs=pl.BlockSpec((tm, d), row),
        out_shape=jax.ShapeDtypeStruct((t, d), F32),
        compiler_params=_params(("parallel",), nbytes),
        name="mix_out",
    )(x2, sb2, cv2, sm2, bg_sb, w_bf)


def _mem_kv_kernel(m_ref, g_ref, wk_ref, wv_ref, kg_ref, k_ref, v_ref):
    hm = _rms_rows(m_ref[0], g_ref[...]).astype(BF16)
    kk = _dot(hm, wk_ref[...])
    for hh in range(XA_HEADS):
        hs = slice(hh * XA_HEAD_DIM, (hh + 1) * XA_HEAD_DIM)
        k_ref[0, :, hs] = _rms_rows(kk[:, hs], kg_ref[...]).astype(BF16)
    v_ref[0] = _dot(hm, wv_ref[...]).astype(BF16)


def _mem_kv(mem, g, wk_bf, wv_bf, kg):
    b, n, d = mem.shape
    const = lambda bb: (0, 0)
    blk = pl.BlockSpec((1, n, d), lambda bb: (bb, 0, 0))
    nbytes = 2 * (n * d * 4 + 2 * d * d * 2 + 2 * n * d * 2) + 4 * n * d * 4
    return pl.pallas_call(
        _mem_kv_kernel,
        grid=(b,),
        in_specs=[blk, pl.BlockSpec((1, d), const), pl.BlockSpec((d, d), const),
                  pl.BlockSpec((d, d), const), pl.BlockSpec((1, XA_HEAD_DIM), const)],
        out_specs=[blk, blk],
        out_shape=[jax.ShapeDtypeStruct((b, n, d), BF16), jax.ShapeDtypeStruct((b, n, d), BF16)],
        compiler_params=_params(("parallel",), nbytes),
        name="mem_kv",
    )(mem, g, wk_bf, wv_bf, kg)


def _xattn_kernel(x_ref, g_ref, wq_ref, qg_ref, k_ref, v_ref, wo_ref, o_ref, ob_scr):
    x = x_ref[0]
    hx = _rms_rows(x, g_ref[...]).astype(BF16)
    q = _dot(hx, wq_ref[...])
    scale = XA_HEAD_DIM ** -0.5
    for hh in range(XA_HEADS):
        hs = slice(hh * XA_HEAD_DIM, (hh + 1) * XA_HEAD_DIM)
        qh = (_rms_rows(q[:, hs], qg_ref[...]) * scale).astype(BF16)
        s = _dot_nt(qh, k_ref[0, :, hs])
        s = s - jnp.max(s, axis=-1, keepdims=True)
        e = jnp.exp(s)
        p = e / jnp.sum(e, axis=-1, keepdims=True)
        ob_scr[:, hs] = _dot(p.astype(BF16), v_ref[0, :, hs]).astype(BF16)
    o_ref[0] = x + _dot(ob_scr[...], wo_ref[...])


def _xattn(x, g, wq_bf, qg, k_bf, v_bf, wo_bf, tm):
    b, l, d = x.shape
    n = k_bf.shape[1]
    const = lambda bb, i: (0, 0)
    row = pl.BlockSpec((1, tm, d), lambda bb, i: (bb, i, 0))
    kv = pl.BlockSpec((1, n, d), lambda bb, i: (bb, 0, 0))
    nbytes = 2 * (2 * tm * d * 4 + 2 * d * d * 2 + 2 * n * d * 2) + 4 * tm * d * 4
    return pl.pallas_call(
        _xattn_kernel,
        grid=(b, l // tm),
        in_specs=[row, pl.BlockSpec((1, d), const), pl.BlockSpec((d, d), const),
                  pl.BlockSpec((1, XA_HEAD_DIM), const), kv, kv, pl.BlockSpec((d, d), const)],
        out_specs=row,
        out_shape=jax.ShapeDtypeStruct((b, l, d), F32),
        scratch_shapes=[pltpu.VMEM((tm, d), BF16)],
        compiler_params=_params(("parallel", "parallel"), nbytes),
        name="xattn",
    )(x, g, wq_bf, qg, k_bf, v_bf, wo_bf)


def _ffn_kernel(x_ref, g_ref, wg_ref, wu_ref, wo_ref, o_ref, h_scr, acc_scr):
    j = pl.program_id(1)

    @pl.when(j == 0)
    def _():
        h_scr[...] = _rms_rows(x_ref[...], g_ref[...]).astype(BF16)
        acc_scr[...] = x_ref[...]

    h = h_scr[...]
    gate = _dot(h, wg_ref[...])
    up = _dot(h, wu_ref[...])
    act = (gate * _sigmoid(gate) * up).astype(BF16)
    acc_scr[...] += _dot(act, wo_ref[...])

    @pl.when(j == pl.num_programs(1) - 1)
    def _():
        o_ref[...] = acc_scr[...]


def _ffn(x2, g, w_in_bf, w_out_bf, tm, th):
    t, d = x2.shape
    hidden = w_out_bf.shape[0]
    nh = hidden // th
    nbytes = 2 * (2 * tm * d * 4 + 2 * d * th * 2 + th * d * 2) + tm * d * 6 + 4 * tm * th * 4
    return pl.pallas_call(
        _ffn_kernel,
        grid=(t // tm, nh),
        in_specs=[pl.BlockSpec((tm, d), lambda i, j: (i, 0)), pl.BlockSpec((1, d), lambda i, j: (0, 0)),
                  pl.BlockSpec((d, th), lambda i, j: (0, j)),
                  pl.BlockSpec((d, th), lambda i, j: (0, j + nh)),
                  pl.BlockSpec((th, d), lambda i, j: (j, 0))],
        out_specs=pl.BlockSpec((tm, d), lambda i, j: (i, 0)),
        out_shape=jax.ShapeDtypeStruct((t, d), F32),
        scratch_shapes=[pltpu.VMEM((tm, d), BF16), pltpu.VMEM((tm, d), F32)],
        compiler_params=_params(("parallel", "arbitrary"), nbytes),
        name="ffn",
    )(x2, g, w_in_bf, w_in_bf, w_out_bf)


def _tile(n, want):
    want = min(want, n)
    for cand in range(want, 0, -1):
        if n % cand == 0 and (cand % V7X_SUBLANES == 0 or cand == n):
            return cand
    return n


def kernel(x, mem, norm_mix_g, w_in, sb_q_norm_g, sb_k_norm_g, conv_dw_w, conv_dw_b, conv_ln_g, conv_ln_b, conv_pw2_w, ssm_lam_re, ssm_lam_im, ssm_log_dt, ssm_b_re, ssm_b_im, ssm_c_re, ssm_c_im, ssm_d, ssm_glu_w, branch_norm_g, w_out, norm_xa_g, norm_mem_g, xa_wq, xa_wk, xa_wv, xa_q_norm_g, xa_k_norm_g, xa_wo, norm_ffn_g, ffn_w_in, ffn_w_out):
    bsz, seq, d = x.shape
    depth = w_in.shape[0]
    t = bsz * seq
    nc = seq // SSM_CHUNK
    tm = _tile(t, 512)
    tq = _tile(seq, 256)
    tc = _tile(seq, 512)
    tm_ffn = _tile(t, 1024)
    th = 256
    assert seq % SSM_CHUNK == 0 and nc % V7X_SUBLANES == 0 and tc % CONV_HALO == 0

    seg = (jnp.arange(SB_WIDTH)[:, None] // SB_HEAD_DIM == jnp.arange(SB_WIDTH)[None, :] // SB_HEAD_DIM).astype(BF16)
    tri = (jnp.arange(tq)[:, None] > jnp.arange(tq)[None, :]).astype(BF16)
    row = lambda a: a.reshape(1, -1).astype(F32)

    x2 = x.reshape(t, d)
    for l in range(depth):
        qg = row(jnp.tile(sb_q_norm_g[l], SB_HEADS)) * (SB_HEAD_DIM ** -0.5)
        kg = row(jnp.tile(sb_k_norm_g[l], SB_HEADS))
        q, k, v, c, u = _mix_in(x2, row(norm_mix_g[l]), w_in[l].astype(BF16), qg, kg, seg, tm)
        o_sb = _sb_attn(q.reshape(bsz, seq, SB_WIDTH), k.reshape(bsz, seq, SB_WIDTH),
                        v.reshape(bsz, seq, SB_WIDTH), tri, tq)
        bg = branch_norm_g[l].astype(F32)
        o_conv = _conv_branch(c.reshape(bsz, seq, 2 * CONV_CH), conv_dw_w[l].astype(F32), row(conv_dw_b[l]),
                              row(conv_ln_g[l]), row(conv_ln_b[l]), conv_pw2_w[l].astype(BF16),
                              row(bg[SB_WIDTH:SB_WIDTH + CONV_CH]), tc)
        ops = _ssm_operators(ssm_lam_re[l], ssm_lam_im[l], ssm_log_dt[l], ssm_b_re[l], ssm_b_im[l],
                             ssm_c_re[l], ssm_c_im[l], ssm_d[l])
        uc = u.reshape(bsz, nc, SSM_CHUNK, SSM_GROUPS, SSM_GROUP).transpose(0, 3, 1, 2, 4)
        uc = uc.reshape(bsz, SSM_GROUPS, nc, SSM_CHUNK * SSM_GROUP)
        yc = _ssm_scan(uc, ops)
        y = yc.reshape(bsz, SSM_GROUPS, nc, SSM_CHUNK, SSM_GROUP).transpose(0, 2, 3, 1, 4).reshape(t, SSM_CH)
        o_ssm = _ssm_glu(y, ssm_glu_w[l].astype(BF16), row(bg[SB_WIDTH + CONV_CH:]), tm)
        x2 = _mix_out(x2, o_sb.reshape(t, SB_WIDTH), o_conv.reshape(t, CONV_CH), o_ssm,
                      row(bg[:SB_WIDTH]), w_out[l].astype(BF16), tm)
        k_m, v_m = _mem_kv(mem, row(norm_mem_g[l]), xa_wk[l].astype(BF16), xa_wv[l].astype(BF16),
                           row(xa_k_norm_g[l]))
        x2 = _xattn(x2.reshape(bsz, seq, d), row(norm_xa_g[l]), xa_wq[l].astype(BF16), row(xa_q_norm_g[l]),
                    k_m, v_m, xa_wo[l].astype(BF16), tm if seq % tm == 0 else _tile(seq, 512)).reshape(t, d)
        x2 = _ffn(x2, row(norm_ffn_g[l]), ffn_w_in[l].astype(BF16), ffn_w_out[l].astype(BF16), tm_ffn, th)
    return x2.reshape(bsz, seq, d)
```

```python
import functools
import math

import jax
import jax.numpy as jnp
from jax import lax
from jax.experimental import pallas as pl
from jax.experimental.pallas import tpu as pltpu

F32 = jnp.float32
BF16 = jnp.bfloat16
EPS = 1e-6

V7X_LANES = 128
V7X_SUBLANES = 8
V7X_VMEM_BYTES = 64 * 1024 * 1024

SB_HEADS = 8
SB_HEAD_DIM = 64
SB_WIDTH = SB_HEADS * SB_HEAD_DIM
CONV_CH = 256
CONV_WIDTH = 31
CONV_HALO = 32
SSM_CH = 256
SSM_GROUP = 16
SSM_GROUPS = SSM_CH // SSM_GROUP
SSM_STATE = 64
SSM_CHUNK = 16
XA_HEADS = 4
XA_HEAD_DIM = 256


def _vmem_limit(nbytes):
    return int(min(max(nbytes * 3 // 2, 16 * 1024 * 1024), V7X_VMEM_BYTES - 8 * 1024 * 1024))


def _params(semantics, nbytes):
    return pltpu.CompilerParams(dimension_semantics=semantics, vmem_limit_bytes=_vmem_limit(nbytes))


def _dot(a, b):
    return jnp.dot(a, b, preferred_element_type=F32)


def _dot_nt(a, b):
    return lax.dot_general(a, b, (((1,), (1,)), ((), ())), preferred_element_type=F32)


def _rms_rows(xf, g):
    return xf * lax.rsqrt(jnp.mean(xf * xf, axis=-1, keepdims=True) + EPS) * g


def _sigmoid(x):
    return 1.0 / (1.0 + jnp.exp(-x))


def _segment_mean_sq(p, seg):
    sq = p * p
    hi = sq.astype(BF16)
    lo = (sq - hi.astype(F32)).astype(BF16)
    return (_dot(hi, seg) + _dot(lo, seg)) * (1.0 / SB_HEAD_DIM)


def _mix_in_kernel(x_ref, g_ref, w_ref, qg_ref, kg_ref, seg_ref,
                   q_ref, k_ref, v_ref, c_ref, u_ref):
    h = _rms_rows(x_ref[...], g_ref[...]).astype(BF16)
    s1, s2, s3 = SB_WIDTH, 2 * SB_WIDTH, 3 * SB_WIDTH
    s4 = s3 + 2 * CONV_CH
    seg = seg_ref[...]
    pq = _dot(h, w_ref[:, 0:s1])
    q_ref[...] = (pq * lax.rsqrt(_segment_mean_sq(pq, seg) + EPS) * qg_ref[...]).astype(BF16)
    pk = _dot(h, w_ref[:, s1:s2])
    k_ref[...] = (pk * lax.rsqrt(_segment_mean_sq(pk, seg) + EPS) * kg_ref[...]).astype(BF16)
    v_ref[...] = _dot(h, w_ref[:, s2:s3]).astype(BF16)
    c_ref[...] = _dot(h, w_ref[:, s3:s4])
    u_ref[...] = _dot(h, w_ref[:, s4:])


def _mix_in(x2, g, w_bf, qg, kg, seg, tm):
    t, d = x2.shape
    n_in = w_bf.shape[1]
    row = lambda i: (i, 0)
    const = lambda i: (0, 0)
    nbytes = 2 * (tm * d * 4 + d * n_in * 2 + tm * (3 * SB_WIDTH * 2 + 2 * CONV_CH * 4 + SSM_CH * 4)) + tm * n_in * 4
    return pl.pallas_call(
        _mix_in_kernel,
        grid=(t // tm,),
        in_specs=[pl.BlockSpec((tm, d), row), pl.BlockSpec((1, d), const),
                  pl.BlockSpec((d, n_in), const), pl.BlockSpec((1, SB_WIDTH), const),
                  pl.BlockSpec((1, SB_WIDTH), const), pl.BlockSpec((SB_WIDTH, SB_WIDTH), const)],
        out_specs=[pl.BlockSpec((tm, SB_WIDTH), row), pl.BlockSpec((tm, SB_WIDTH), row),
                   pl.BlockSpec((tm, SB_WIDTH), row), pl.BlockSpec((tm, 2 * CONV_CH), row),
                   pl.BlockSpec((tm, SSM_CH), row)],
        out_shape=[jax.ShapeDtypeStruct((t, SB_WIDTH), BF16), jax.ShapeDtypeStruct((t, SB_WIDTH), BF16),
                   jax.ShapeDtypeStruct((t, SB_WIDTH), BF16), jax.ShapeDtypeStruct((t, 2 * CONV_CH), F32),
                   jax.ShapeDtypeStruct((t, SSM_CH), F32)],
        compiler_params=_params(("parallel",), nbytes),
        name="mix_in",
    )(x2, g, w_bf, qg, kg, seg)


SB_SLOTS = 2
SB_GROUPS_PER_STEP = 2


def _sb_decay(z, run, ntri, mask):
    sp = jnp.maximum(z, 0.0) + jnp.log(1.0 + jnp.exp(-jnp.abs(z)))
    if mask is not None:
        sp = jnp.where(mask, sp, 0.0)
    later = _dot(sp.astype(BF16), ntri)
    return z - sp + run, later, run + later[:, 0:1] - sp[:, 0:1]


def _sb_weights(t0, later, mask):
    w = jnp.exp(t0 + later)
    if mask is not None:
        w = jnp.where(mask, w, 0.0)
    return w.astype(BF16)


def _sb_attn_kernel(q_ref, k_ref, v_ref, ntri_ref, o_ref, q2_scr, z_scr, t0_scr, lat_scr, run_scr, acc_scr, *, tq):
    i = pl.program_id(2)
    nh = V7X_LANES // SB_HEAD_DIM
    m = nh * tq
    groups = [slice(g * V7X_LANES, (g + 1) * V7X_LANES) for g in range(SB_GROUPS_PER_STEP)]

    def rows(j):
        return pl.ds(pl.multiple_of((i - jnp.minimum(j, i)) * tq, tq), tq)

    def scores(j, slot):
        for g, gs in enumerate(groups):
            z_scr[g, slot] = _dot_nt(q2_scr[g], k_ref[0, rows(j), gs])

    def decay(slot, mask):
        for g in range(len(groups)):
            t0, later, run = _sb_decay(z_scr[g, slot], run_scr[g], ntri_ref[...], mask)
            t0_scr[g, slot] = t0
            lat_scr[g, slot] = later
            run_scr[g] = run

    def output(j, slot, mask):
        for g, gs in enumerate(groups):
            acc_scr[g] += _dot(_sb_weights(t0_scr[g, slot], lat_scr[g, slot], mask), v_ref[0, rows(j), gs])

    lane_head = lax.broadcasted_iota(jnp.int32, (tq, V7X_LANES), 1) // SB_HEAD_DIM
    for g, gs in enumerate(groups):
        q = q_ref[0, :, gs]
        for h in range(nh):
            q2_scr[g, h * tq:(h + 1) * tq, :] = jnp.where(lane_head == h, q, jnp.zeros_like(q))
    row = lax.broadcasted_iota(jnp.int32, (m, tq), 0) % tq
    col = lax.broadcasted_iota(jnp.int32, (m, tq), 1)
    diag_mask = col < row
    run_scr[...] = jnp.zeros_like(run_scr)
    acc_scr[...] = jnp.zeros_like(acc_scr)
    scores(0, 0)
    decay(0, diag_mask)
    scores(1, 1)
    output(0, 0, diag_mask)
    scores(2, 0)
    decay(1, None)

    def body(p, carry):
        t = 2 * p + 1
        scores(t + 2, 1)
        output(t, 1, None)
        decay(0, None)
        scores(t + 3, 0)
        output(t + 1, 0, None)
        decay(1, None)
        return carry

    lax.fori_loop(0, i // 2, body, 0)

    @pl.when(i % 2 == 1)
    def _():
        output(i, 1, None)

    for g, gs in enumerate(groups):
        out = acc_scr[g, 0:tq, :]
        for h in range(1, nh):
            out = jnp.where(lane_head == h, acc_scr[g, h * tq:(h + 1) * tq, :], out)
        o_ref[0, :, gs] = out


def _sb_attn(q, k, v, ntri, tq):
    b, l, w = q.shape
    ng = SB_GROUPS_PER_STEP
    gw = ng * V7X_LANES
    m = (V7X_LANES // SB_HEAD_DIM) * tq
    blk = pltpu.VMEM((ng, SB_SLOTS, m, tq), F32)
    nbytes = (2 * (2 * l * gw * 2 + tq * gw * 2 + tq * gw * 4 + tq * tq * 2)
              + ng * (3 * SB_SLOTS * m * tq * 4 + 3 * m * V7X_LANES * 4) + 6 * m * tq * 4)
    return pl.pallas_call(
        functools.partial(_sb_attn_kernel, tq=tq),
        grid=(b, w // gw, l // tq),
        in_specs=[pl.BlockSpec((1, tq, gw), lambda bb, hp, i: (bb, i, hp)),
                  pl.BlockSpec((1, l, gw), lambda bb, hp, i: (bb, 0, hp)),
                  pl.BlockSpec((1, l, gw), lambda bb, hp, i: (bb, 0, hp)),
                  pl.BlockSpec((tq, tq), lambda bb, hp, i: (0, 0))],
        out_specs=pl.BlockSpec((1, tq, gw), lambda bb, hp, i: (bb, i, hp)),
        out_shape=jax.ShapeDtypeStruct((b, l, w), F32),
        scratch_shapes=[pltpu.VMEM((ng, m, V7X_LANES), BF16), blk, blk, blk,
                        pltpu.VMEM((ng, m, 1), F32), pltpu.VMEM((ng, m, V7X_LANES), F32)],
        compiler_params=_params(("parallel", "parallel", "arbitrary"), nbytes),
        name="sb_attn",
    )(q, k, v, ntri)


def _conv_kernel(cur_ref, prev_ref, dww_ref, dwb_ref, lng_ref, lnb_ref, pw_ref, bg_ref, o_ref, hbuf, *, tc):
    i = pl.program_id(1)
    cur = cur_ref[0]
    prev = prev_ref[0]
    hbuf[CONV_HALO:, :] = cur[:, :CONV_CH] * _sigmoid(cur[:, CONV_CH:])
    hprev = prev[:, :CONV_CH] * _sigmoid(prev[:, CONV_CH:])
    hbuf[:CONV_HALO, :] = jnp.where(i > 0, hprev, 0.0)
    acc = jnp.zeros((tc, CONV_CH), F32) + dwb_ref[...]
    off = CONV_HALO - (CONV_WIDTH - 1)
    for j in range(CONV_WIDTH):
        acc = acc + dww_ref[j:j + 1, :] * hbuf[off + j:off + j + tc, :]
    mu = jnp.mean(acc, axis=-1, keepdims=True)
    cen = acc - mu
    var = jnp.mean(cen * cen, axis=-1, keepdims=True)
    y = cen * lax.rsqrt(var + EPS) * lng_ref[...] + lnb_ref[...]
    y = y * _sigmoid(y)
    o = _dot(y.astype(BF16), pw_ref[...])
    o_ref[0] = _rms_rows(o, bg_ref[...])


def _conv_branch(c, dww, dwb, lng, lnb, pw_bf, bg, tc):
    b, l, w = c.shape
    per_tile = tc // CONV_HALO
    const = lambda bb, i: (0, 0)
    nbytes = 2 * (tc * w * 4 + CONV_HALO * w * 4 + tc * CONV_CH * 4) + (tc + CONV_HALO) * CONV_CH * 4 * 6
    return pl.pallas_call(
        functools.partial(_conv_kernel, tc=tc),
        grid=(b, l // tc),
        in_specs=[pl.BlockSpec((1, tc, w), lambda bb, i: (bb, i, 0)),
                  pl.BlockSpec((1, CONV_HALO, w), lambda bb, i: (bb, jnp.maximum(i * per_tile - 1, 0), 0)),
                  pl.BlockSpec((CONV_WIDTH, CONV_CH), const), pl.BlockSpec((1, CONV_CH), const),
                  pl.BlockSpec((1, CONV_CH), const), pl.BlockSpec((1, CONV_CH), const),
                  pl.BlockSpec((CONV_CH, CONV_CH), const), pl.BlockSpec((1, CONV_CH), const)],
        out_specs=pl.BlockSpec((1, tc, CONV_CH), lambda bb, i: (bb, i, 0)),
        out_shape=jax.ShapeDtypeStruct((b, l, CONV_CH), F32),
        scratch_shapes=[pltpu.VMEM((tc + CONV_HALO, CONV_CH), F32)],
        compiler_params=_params(("parallel", "parallel"), nbytes),
        name="conv_branch",
    )(c, c, dww, dwb, lng, lnb, pw_bf, bg)


def _ssm_operators(lam_re, lam_im, log_dt, b_re, b_im, c_re, c_im, d):
    hp = lax.Precision.HIGHEST
    lr, li = lam_re.astype(F32), lam_im.astype(F32)
    dt = jnp.exp(log_dt.astype(F32))[:, None]
    mag = jnp.exp(lr * dt)
    ar, ai = mag * jnp.cos(li * dt), mag * jnp.sin(li * dt)
    den = lr * lr + li * li
    fr = ((ar - 1.0) * lr + ai * li) / den
    fi = (ai * lr - (ar - 1.0) * li) / den
    br, bi = b_re.astype(F32), b_im.astype(F32)
    bbr = fr[..., None] * br - fi[..., None] * bi
    bbi = fr[..., None] * bi + fi[..., None] * br
    cr, ci = c_re.astype(F32), c_im.astype(F32)
    n = jnp.arange(SSM_CHUNK + 1, dtype=F32)[:, None, None]
    pmag = jnp.exp(n * (lr * dt)[None])
    pr, pi = pmag * jnp.cos(n * (li * dt)[None]), pmag * jnp.sin(n * (li * dt)[None])
    abr = pr[:, :, :, None] * bbr[None] - pi[:, :, :, None] * bbi[None]
    abi = pr[:, :, :, None] * bbi[None] + pi[:, :, :, None] * bbr[None]
    kern = (jnp.einsum('ngpi,gop->ngio', abr, cr, precision=hp)
            - jnp.einsum('ngpi,gop->ngio', abi, ci, precision=hp))
    s_idx = jnp.arange(SSM_CHUNK)[:, None]
    t_idx = jnp.arange(SSM_CHUNK)[None, :]
    lag = t_idx - s_idx
    k_st = kern[jnp.clip(lag, 0, SSM_CHUNK)]
    k_st = jnp.where((lag >= 0)[:, :, None, None, None], k_st, 0.0)
    w_intra = k_st.transpose(2, 0, 3, 1, 4).reshape(SSM_GROUPS, SSM_CHUNK * SSM_GROUP, SSM_CHUNK * SSM_GROUP)
    rev = SSM_CHUNK - 1 - jnp.arange(SSM_CHUNK)
    w_st_r = abr[rev].transpose(1, 0, 3, 2).reshape(SSM_GROUPS, SSM_CHUNK * SSM_GROUP, SSM_STATE)
    w_st_i = abi[rev].transpose(1, 0, 3, 2).reshape(SSM_GROUPS, SSM_CHUNK * SSM_GROUP, SSM_STATE)
    p1r, p1i = pr[1:], pi[1:]
    o_r = cr[None] * p1r[:, :, None, :] - ci[None] * p1i[:, :, None, :]
    o_i = -(cr[None] * p1i[:, :, None, :] + ci[None] * p1r[:, :, None, :])
    w_out_r = o_r.transpose(1, 3, 0, 2).reshape(SSM_GROUPS, SSM_STATE, SSM_CHUNK * SSM_GROUP)
    w_out_i = o_i.transpose(1, 3, 0, 2).reshape(SSM_GROUPS, SSM_STATE, SSM_CHUNK * SSM_GROUP)
    a_n = jnp.stack([pr[SSM_CHUNK].reshape(1, -1), pi[SSM_CHUNK].reshape(1, -1)])
    d_c = jnp.tile(d.astype(F32).reshape(SSM_GROUPS, 1, SSM_GROUP), (1, 1, SSM_CHUNK))
    return (w_intra.astype(BF16), w_st_r.astype(BF16), w_st_i.astype(BF16),
            w_out_r.astype(BF16), w_out_i.astype(BF16), a_n, d_c)


def _ssm_kernel(u_ref, wi_ref, wsr_ref, wsi_ref, wor_ref, woi_ref, an_ref, dc_ref, y_ref,
                sr_scr, si_scr, xr_scr, xi_scr, *, nc):
    p = SSM_STATE
    for g in range(SSM_GROUPS):
        ub = u_ref[0, g].astype(BF16)
        sr_scr[:, g * p:(g + 1) * p] = _dot(ub, wsr_ref[g])
        si_scr[:, g * p:(g + 1) * p] = _dot(ub, wsi_ref[g])
    ar = an_ref[0]
    ai = an_ref[1]

    def body(c8, carry):
        xr, xi = carry
        r0 = pl.multiple_of(c8 * V7X_SUBLANES, V7X_SUBLANES)
        sr = sr_scr[pl.ds(r0, V7X_SUBLANES), :]
        si = si_scr[pl.ds(r0, V7X_SUBLANES), :]
        rows_r, rows_i = [], []
        for r in range(V7X_SUBLANES):
            rows_r.append(xr)
            rows_i.append(xi)
            xr, xi = (ar * xr - ai * xi + sr[r:r + 1, :], ar * xi + ai * xr + si[r:r + 1, :])
        xr_scr[pl.ds(r0, V7X_SUBLANES), :] = jnp.concatenate(rows_r, axis=0)
        xi_scr[pl.ds(r0, V7X_SUBLANES), :] = jnp.concatenate(rows_i, axis=0)
        return xr, xi

    zero = jnp.zeros((1, SSM_GROUPS * p), F32)
    lax.fori_loop(0, nc // V7X_SUBLANES, body, (zero, zero))
    for g in range(SSM_GROUPS):
        uf = u_ref[0, g]
        y = _dot(uf.astype(BF16), wi_ref[g])
        y = y + _dot(xr_scr[:, g * p:(g + 1) * p].astype(BF16), wor_ref[g])
        y = y + _dot(xi_scr[:, g * p:(g + 1) * p].astype(BF16), woi_ref[g])
        y_ref[0, g] = y + dc_ref[g] * uf


def _ssm_scan(uc, ops):
    w_intra, w_st_r, w_st_i, w_out_r, w_out_i, a_n, d_c = ops
    b, g, nc, cw = uc.shape
    gp = SSM_GROUPS * SSM_STATE
    full = lambda shape: pl.BlockSpec(shape, lambda bb: (0,) * len(shape))
    nbytes = 4 * g * nc * cw * 4 + 4 * nc * gp * 4 + 4 * g * cw * cw * 2
    return pl.pallas_call(
        functools.partial(_ssm_kernel, nc=nc),
        grid=(b,),
        in_specs=[pl.BlockSpec((1, g, nc, cw), lambda bb: (bb, 0, 0, 0)),
                  full(w_intra.shape), full(w_st_r.shape), full(w_st_i.shape),
                  full(w_out_r.shape), full(w_out_i.shape), full(a_n.shape), full(d_c.shape)],
        out_specs=pl.BlockSpec((1, g, nc, cw), lambda bb: (bb, 0, 0, 0)),
        out_shape=jax.ShapeDtypeStruct((b, g, nc, cw), F32),
        scratch_shapes=[pltpu.VMEM((nc, gp), F32) for _ in range(4)],
        compiler_params=_params(("parallel",), nbytes),
        name="ssm_scan",
    )(uc, w_intra, w_st_r, w_st_i, w_out_r, w_out_i, a_n, d_c)


def _ssm_glu_kernel(y_ref, w_ref, bg_ref, o_ref):
    z = _dot(y_ref[...].astype(BF16), w_ref[...])
    o = z[:, :SSM_CH] * _sigmoid(z[:, SSM_CH:])
    o_ref[...] = _rms_rows(o, bg_ref[...])


def _ssm_glu(y2, w_bf, bg, tm):
    t = y2.shape[0]
    row = lambda i: (i, 0)
    const = lambda i: (0, 0)
    nbytes = 2 * (2 * tm * SSM_CH * 4 + SSM_CH * 2 * SSM_CH * 2) + 4 * tm * SSM_CH * 4
    return pl.pallas_call(
        _ssm_glu_kernel,
        grid=(t // tm,),
        in_specs=[pl.BlockSpec((tm, SSM_CH), row), pl.BlockSpec((SSM_CH, 2 * SSM_CH), const),
                  pl.BlockSpec((1, SSM_CH), const)],
        out_specs=pl.BlockSpec((tm, SSM_CH), row),
        out_shape=jax.ShapeDtypeStruct((t, SSM_CH), F32),
        compiler_params=_params(("parallel",), nbytes),
        name="ssm_glu",
    )(y2, w_bf, bg)


def _mix_out_kernel(x_ref, sb_ref, cv_ref, sm_ref, bg_ref, w_ref, o_ref):
    sb = _rms_rows(sb_ref[...], bg_ref[...]).astype(BF16)
    s1 = SB_WIDTH
    s2 = SB_WIDTH + CONV_CH
    y = _dot(sb, w_ref[0:s1, :])
    y = y + _dot(cv_ref[...].astype(BF16), w_ref[s1:s2, :])
    y = y + _dot(sm_ref[...].astype(BF16), w_ref[s2:, :])
    o_ref[...] = x_ref[...] + y


def _mix_out(x2, sb2, cv2, sm2, bg_sb, w_bf, tm):
    t, d = x2.shape
    row = lambda i: (i, 0)
    const = lambda i: (0, 0)
    nbytes = 2 * (2 * tm * d * 4 + tm * (SB_WIDTH + CONV_CH + SSM_CH) * 4 + d * d * 2) + 2 * tm * d * 4
    return pl.pallas_call(
        _mix_out_kernel,
        grid=(t // tm,),
        in_specs=[pl.BlockSpec((tm, d), row), pl.BlockSpec((tm, SB_WIDTH), row),
                  pl.BlockSpec((tm, CONV_CH), row), pl.BlockSpec((tm, SSM_CH), row),
                  pl.BlockSpec((1, SB_WIDTH), const), pl.BlockSpec(w_bf.shape, const)],
        out_specs=pl.BlockSpec((tm, d), row),
        out_shape=jax.ShapeDtypeStruct((t, d), F32),
        compiler_params=_params(("parallel",), nbytes),
        name="mix_out",
    )(x2, sb2, cv2, sm2, bg_sb, w_bf)


def _mem_kv_kernel(m_ref, g_ref, wk_ref, wv_ref, kg_ref, k_ref, v_ref):
    hm = _rms_rows(m_ref[0], g_ref[...]).astype(BF16)
    kk = _dot(hm, wk_ref[...])
    for hh in range(XA_HEADS):
        hs = slice(hh * XA_HEAD_DIM, (hh + 1) * XA_HEAD_DIM)
        k_ref[0, :, hs] = _rms_rows(kk[:, hs], kg_ref[...]).astype(BF16)
    v_ref[0] = _dot(hm, wv_ref[...]).astype(BF16)


def _mem_kv(mem, g, wk_bf, wv_bf, kg):
    b, n, d = mem.shape
    const = lambda bb: (0, 0)
    blk = pl.BlockSpec((1, n, d), lambda bb: (bb, 0, 0))
    nbytes = 2 * (n * d * 4 + 2 * d * d * 2 + 2 * n * d * 2) + 4 * n * d * 4
    return pl.pallas_call(
        _mem_kv_kernel,
        grid=(b,),
        in_specs=[blk, pl.BlockSpec((1, d), const), pl.BlockSpec((d, d), const),
                  pl.BlockSpec((d, d), const), pl.BlockSpec((1, XA_HEAD_DIM), const)],
        out_specs=[blk, blk],
        out_shape=[jax.ShapeDtypeStruct((b, n, d), BF16), jax.ShapeDtypeStruct((b, n, d), BF16)],
        compiler_params=_params(("parallel",), nbytes),
        name="mem_kv",
    )(mem, g, wk_bf, wv_bf, kg)


def _xattn_kernel(x_ref, g_ref, wq_ref, qg_ref, k_ref, v_ref, wo_ref, o_ref, ob_scr):
    x = x_ref[0]
    hx = _rms_rows(x, g_ref[...]).astype(BF16)
    q = _dot(hx, wq_ref[...])
    scale = XA_HEAD_DIM ** -0.5
    for hh in range(XA_HEADS):
        hs = slice(hh * XA_HEAD_DIM, (hh + 1) * XA_HEAD_DIM)
        qh = (_rms_rows(q[:, hs], qg_ref[...]) * scale).astype(BF16)
        s = _dot_nt(qh, k_ref[0, :, hs])
        s = s - jnp.max(s, axis=-1, keepdims=True)
        e = jnp.exp(s)
        p = e / jnp.sum(e, axis=-1, keepdims=True)
        ob_scr[:, hs] = _dot(p.astype(BF16), v_ref[0, :, hs]).astype(BF16)
    o_ref[0] = x + _dot(ob_scr[...], wo_ref[...])


def _xattn(x, g, wq_bf, qg, k_bf, v_bf, wo_bf, tm):
    b, l, d = x.shape
    n = k_bf.shape[1]
    const = lambda bb, i: (0, 0)
    row = pl.BlockSpec((1, tm, d), lambda bb, i: (bb, i, 0))
    kv = pl.BlockSpec((1, n, d), lambda bb, i: (bb, 0, 0))
    nbytes = 2 * (2 * tm * d * 4 + 2 * d * d * 2 + 2 * n * d * 2) + 4 * tm * d * 4
    return pl.pallas_call(
        _xattn_kernel,
        grid=(b, l // tm),
        in_specs=[row, pl.BlockSpec((1, d), const), pl.BlockSpec((d, d), const),
                  pl.BlockSpec((1, XA_HEAD_DIM), const), kv, kv, pl.BlockSpec((d, d), const)],
        out_specs=row,
        out_shape=jax.ShapeDtypeStruct((b, l, d), F32),
        scratch_shapes=[pltpu.VMEM((tm, d), BF16)],
        compiler_params=_params(("parallel", "parallel"), nbytes),
        name="xattn",
    )(x, g, wq_bf, qg, k_bf, v_bf, wo_bf)


def _ffn_kernel(x_ref, g_ref, wg_ref, wu_ref, wo_ref, o_ref, h_scr, acc_scr):
    j = pl.program_id(1)

    @pl.when(j == 0)
    def _():
        h_scr[...] = _rms_rows(x_ref[...], g_ref[...]).astype(BF16)
        acc_scr[...] = x_ref[...]

    h = h_scr[...]
    gate = _dot(h, wg_ref[...])
    up = _dot(h, wu_ref[...])
    act = (gate * _sigmoid(gate) * up).astype(BF16)
    acc_scr[...] += _dot(act, wo_ref[...])

    @pl.when(j == pl.num_programs(1) - 1)
    def _():
        o_ref[...] = acc_scr[...]


def _ffn(x2, g, w_in_bf, w_out_bf, tm, th):
    t, d = x2.shape
    hidden = w_out_bf.shape[0]
    nh = hidden // th
    nbytes = 2 * (2 * tm * d * 4 + 2 * d * th * 2 + th * d * 2) + tm * d * 6 + 4 * tm * th * 4
    return pl.pallas_call(
        _ffn_kernel,
        grid=(t // tm, nh),
        in_specs=[pl.BlockSpec((tm, d), lambda i, j: (i, 0)), pl.BlockSpec((1, d), lambda i, j: (0, 0)),
                  pl.BlockSpec((d, th), lambda i, j: (0, j)),
                  pl.BlockSpec((d, th), lambda i, j: (0, j + nh)),
                  pl.BlockSpec((th, d), lambda i, j: (j, 0))],
        out_specs=pl.BlockSpec((tm, d), lambda i, j: (i, 0)),
        out_shape=jax.ShapeDtypeStruct((t, d), F32),
        scratch_shapes=[pltpu.VMEM((tm, d), BF16), pltpu.VMEM((tm, d), F32)],
        compiler_params=_params(("parallel", "arbitrary"), nbytes),
        name="ffn",
    )(x2, g, w_in_bf, w_in_bf, w_out_bf)


def _tile(n, want):
    want = min(want, n)
    for cand in range(want, 0, -1):
        if n % cand == 0 and (cand % V7X_SUBLANES == 0 or cand == n):
            return cand
    return n


def kernel(x, mem, norm_mix_g, w_in, sb_q_norm_g, sb_k_norm_g, conv_dw_w, conv_dw_b, conv_ln_g, conv_ln_b, conv_pw2_w, ssm_lam_re, ssm_lam_im, ssm_log_dt, ssm_b_re, ssm_b_im, ssm_c_re, ssm_c_im, ssm_d, ssm_glu_w, branch_norm_g, w_out, norm_xa_g, norm_mem_g, xa_wq, xa_wk, xa_wv, xa_q_norm_g, xa_k_norm_g, xa_wo, norm_ffn_g, ffn_w_in, ffn_w_out):
    bsz, seq, d = x.shape
    depth = w_in.shape[0]
    t = bsz * seq
    nc = seq // SSM_CHUNK
    tm = _tile(t, 512)
    tq = _tile(seq, 256)
    tc = _tile(seq, 512)
    tm_ffn = _tile(t, 1024)
    th = 256
    assert seq % SSM_CHUNK == 0 and nc % V7X_SUBLANES == 0 and tc % CONV_HALO == 0

    seg = (jnp.arange(SB_WIDTH)[:, None] // SB_HEAD_DIM == jnp.arange(SB_WIDTH)[None, :] // SB_HEAD_DIM).astype(BF16)
    ntri = -(jnp.arange(tq)[:, None] > jnp.arange(tq)[None, :]).astype(BF16)
    row = lambda a: a.reshape(1, -1).astype(F32)

    x2 = x.reshape(t, d)
    for l in range(depth):
        qg = row(jnp.tile(sb_q_norm_g[l], SB_HEADS)) * (SB_HEAD_DIM ** -0.5)
        kg = row(jnp.tile(sb_k_norm_g[l], SB_HEADS))
        q, k, v, c, u = _mix_in(x2, row(norm_mix_g[l]), w_in[l].astype(BF16), qg, kg, seg, tm)
        o_sb = _sb_attn(q.reshape(bsz, seq, SB_WIDTH), k.reshape(bsz, seq, SB_WIDTH),
                        v.reshape(bsz, seq, SB_WIDTH), ntri, tq)
        bg = branch_norm_g[l].astype(F32)
        o_conv = _conv_branch(c.reshape(bsz, seq, 2 * CONV_CH), conv_dw_w[l].astype(F32), row(conv_dw_b[l]),
                              row(conv_ln_g[l]), row(conv_ln_b[l]), conv_pw2_w[l].astype(BF16),
                              row(bg[SB_WIDTH:SB_WIDTH + CONV_CH]), tc)
        ops = _ssm_operators(ssm_lam_re[l], ssm_lam_im[l], ssm_log_dt[l], ssm_b_re[l], ssm_b_im[l],
                             ssm_c_re[l], ssm_c_im[l], ssm_d[l])
        uc = u.reshape(bsz, nc, SSM_CHUNK, SSM_GROUPS, SSM_GROUP).transpose(0, 3, 1, 2, 4)
        uc = uc.reshape(bsz, SSM_GROUPS, nc, SSM_CHUNK * SSM_GROUP)
        yc = _ssm_scan(uc, ops)
        y = yc.reshape(bsz, SSM_GROUPS, nc, SSM_CHUNK, SSM_GROUP).transpose(0, 2, 3, 1, 4).reshape(t, SSM_CH)
        o_ssm = _ssm_glu(y, ssm_glu_w[l].astype(BF16), row(bg[SB_WIDTH + CONV_CH:]), tm)
        x2 = _mix_out(x2, o_sb.reshape(t, SB_WIDTH), o_conv.reshape(t, CONV_CH), o_ssm,
                      row(bg[:SB_WIDTH]), w_out[l].astype(BF16), tm)
        k_m, v_m = _mem_kv(mem, row(norm_mem_g[l]), xa_wk[l].astype(BF16), xa_wv[l].astype(BF16),
                           row(xa_k_norm_g[l]))
        x2 = _xattn(x2.reshape(bsz, seq, d), row(norm_xa_g[l]), xa_wq[l].astype(BF16), row(xa_q_norm_g[l]),
                    k_m, v_m, xa_wo[l].astype(BF16), tm if seq % tm == 0 else _tile(seq, 512)).reshape(t, d)
        x2 = _ffn(x2, row(norm_ffn_g[l]), ffn_w_in[l].astype(BF16), ffn_w_out[l].astype(BF16), tm_ffn, th)
    return x2.reshape(bsz, seq, d)
```

```python
import functools
import math

import jax
import jax.numpy as jnp
from jax import lax
from jax.experimental import pallas as pl
from jax.experimental.pallas import tpu as pltpu

F32 = jnp.float32
BF16 = jnp.bfloat16
EPS = 1e-6

V7X_LANES = 128
V7X_SUBLANES = 8
V7X_VMEM_BYTES = 64 * 1024 * 1024

SB_HEADS = 8
SB_HEAD_DIM = 64
SB_WIDTH = SB_HEADS * SB_HEAD_DIM
CONV_CH = 256
CONV_WIDTH = 31
CONV_HALO = 32
SSM_CH = 256
SSM_GROUP = 16
SSM_GROUPS = SSM_CH // SSM_GROUP
SSM_STATE = 64
SSM_CHUNK = 16
XA_HEADS = 4
XA_HEAD_DIM = 256


def _vmem_limit(nbytes):
    return int(min(max(nbytes * 3 // 2, 16 * 1024 * 1024), V7X_VMEM_BYTES - 8 * 1024 * 1024))


def _params(semantics, nbytes):
    return pltpu.CompilerParams(dimension_semantics=semantics, vmem_limit_bytes=_vmem_limit(nbytes))


def _dot(a, b):
    return jnp.dot(a, b, preferred_element_type=F32)


def _dot_nt(a, b):
    return lax.dot_general(a, b, (((1,), (1,)), ((), ())), preferred_element_type=F32)


def _rms_rows(xf, g):
    return xf * lax.rsqrt(jnp.mean(xf * xf, axis=-1, keepdims=True) + EPS) * g


def _sigmoid(x):
    return 1.0 / (1.0 + jnp.exp(-x))


def _segment_mean_sq(p, seg):
    sq = p * p
    hi = sq.astype(BF16)
    lo = (sq - hi.astype(F32)).astype(BF16)
    return (_dot(hi, seg) + _dot(lo, seg)) * (1.0 / SB_HEAD_DIM)


def _mix_in_kernel(x_ref, g_ref, w_ref, qg_ref, kg_ref, seg_ref,
                   q_ref, k_ref, v_ref, c_ref, u_ref):
    h = _rms_rows(x_ref[...], g_ref[...]).astype(BF16)
    s1, s2, s3 = SB_WIDTH, 2 * SB_WIDTH, 3 * SB_WIDTH
    s4 = s3 + 2 * CONV_CH
    seg = seg_ref[...]
    pq = _dot(h, w_ref[:, 0:s1])
    q_ref[...] = (pq * lax.rsqrt(_segment_mean_sq(pq, seg) + EPS) * qg_ref[...]).astype(BF16)
    pk = _dot(h, w_ref[:, s1:s2])
    k_ref[...] = (pk * lax.rsqrt(_segment_mean_sq(pk, seg) + EPS) * kg_ref[...]).astype(BF16)
    v_ref[...] = _dot(h, w_ref[:, s2:s3]).astype(BF16)
    c_ref[...] = _dot(h, w_ref[:, s3:s4])
    u_ref[...] = _dot(h, w_ref[:, s4:])


def _mix_in(x2, g, w_bf, qg, kg, seg, tm):
    t, d = x2.shape
    n_in = w_bf.shape[1]
    row = lambda i: (i, 0)
    const = lambda i: (0, 0)
    nbytes = 2 * (tm * d * 4 + d * n_in * 2 + tm * (3 * SB_WIDTH * 2 + 2 * CONV_CH * 4 + SSM_CH * 4)) + tm * n_in * 4
    return pl.pallas_call(
        _mix_in_kernel,
        grid=(t // tm,),
        in_specs=[pl.BlockSpec((tm, d), row), pl.BlockSpec((1, d), const),
                  pl.BlockSpec((d, n_in), const), pl.BlockSpec((1, SB_WIDTH), const),
                  pl.BlockSpec((1, SB_WIDTH), const), pl.BlockSpec((SB_WIDTH, SB_WIDTH), const)],
        out_specs=[pl.BlockSpec((tm, SB_WIDTH), row), pl.BlockSpec((tm, SB_WIDTH), row),
                   pl.BlockSpec((tm, SB_WIDTH), row), pl.BlockSpec((tm, 2 * CONV_CH), row),
                   pl.BlockSpec((tm, SSM_CH), row)],
        out_shape=[jax.ShapeDtypeStruct((t, SB_WIDTH), BF16), jax.ShapeDtypeStruct((t, SB_WIDTH), BF16),
                   jax.ShapeDtypeStruct((t, SB_WIDTH), BF16), jax.ShapeDtypeStruct((t, 2 * CONV_CH), F32),
                   jax.ShapeDtypeStruct((t, SSM_CH), F32)],
        compiler_params=_params(("parallel",), nbytes),
        name="mix_in",
    )(x2, g, w_bf, qg, kg, seg)


SB_SLOTS = 2
SB_GROUPS_PER_STEP = 2
SB_SOFTPLUS_LINEAR = 40.0
SB_EXP_UNDERFLOW = -104.0
SB_NEVER = -1e30


def _sb_decay(z, run, ntri, mask):
    sp = jnp.maximum(jnp.log(1.0 + jnp.exp(jnp.minimum(z, SB_SOFTPLUS_LINEAR))), z)
    if mask is not None:
        sp = jnp.where(mask, sp, 0.0)
    later = _dot(sp.astype(BF16), ntri)
    return z - sp + run, later, run + later[:, 0:1] - sp[:, 0:1]


def _sb_weights(t0, later, mask):
    w = jnp.exp(t0 + later)
    if mask is not None:
        w = jnp.where(mask, w, 0.0)
    return w.astype(BF16)


def _sb_attn_kernel(q_ref, k_ref, v_ref, ntri_ref, o_ref, q2_scr, z_scr, t0_scr, lat_scr, run_scr, acc_scr, *, tq):
    i = pl.program_id(2)
    nh = V7X_LANES // SB_HEAD_DIM
    m = nh * tq
    groups = [slice(g * V7X_LANES, (g + 1) * V7X_LANES) for g in range(SB_GROUPS_PER_STEP)]

    def rows(j):
        return pl.ds(pl.multiple_of((i - jnp.minimum(j, i)) * tq, tq), tq)

    def scores(j, slot):
        for g, gs in enumerate(groups):
            z_scr[g, slot] = _dot_nt(q2_scr[g], k_ref[0, rows(j), gs])

    def decay(slot, mask=None, bias=None):
        for g in range(len(groups)):
            run = run_scr[g] if bias is None else run_scr[g] + bias
            t0, later, run = _sb_decay(z_scr[g, slot], run, ntri_ref[...], mask)
            t0_scr[g, slot] = t0
            lat_scr[g, slot] = later
            run_scr[g] = run

    def output(j, slot, mask=None):
        for g, gs in enumerate(groups):
            acc_scr[g] += _dot(_sb_weights(t0_scr[g, slot], lat_scr[g, slot], mask), v_ref[0, rows(j), gs])

    def live():
        run = run_scr[0]
        for g in range(1, len(groups)):
            run = jnp.maximum(run, run_scr[g])
        return (jnp.max(run) >= SB_EXP_UNDERFLOW).astype(jnp.int32)

    lane_head = lax.broadcasted_iota(jnp.int32, (tq, V7X_LANES), 1) // SB_HEAD_DIM
    for g, gs in enumerate(groups):
        q = q_ref[0, :, gs]
        for h in range(nh):
            q2_scr[g, h * tq:(h + 1) * tq, :] = jnp.where(lane_head == h, q, jnp.zeros_like(q))
    row = lax.broadcasted_iota(jnp.int32, (m, tq), 0) % tq
    col = lax.broadcasted_iota(jnp.int32, (m, tq), 1)
    diag_mask = col < row
    run_scr[...] = jnp.zeros_like(run_scr)
    acc_scr[...] = jnp.zeros_like(acc_scr)

    scores(0, 0)
    scores(1, 1)
    decay(0, mask=diag_mask)
    decay(1, bias=jnp.where(i >= 1, 0.0, SB_NEVER))
    output(0, 0, diag_mask)
    output(1, 1)

    @pl.when(jnp.logical_and(i >= 2, live() > 0))
    def _():
        scores(2, 0)
        decay(0)
        scores(3, 1)
        more = live()
        output(2, 0)
        scores(4, 0)
        decay(1)
        n_pairs = (i - 2) // 2

        def cond(carry):
            p, more = carry
            return jnp.logical_and(p < n_pairs, more > 0)

        def body(carry):
            p, _ = carry
            t = 2 * p + 3
            scores(t + 2, 1)
            output(t, 1)
            decay(0)
            more = live()
            scores(t + 3, 0)
            output(t + 1, 0)
            decay(1)
            return p + 1, more

        _, more = lax.while_loop(cond, body, (0, more))

        @pl.when(jnp.logical_and(more > 0, (i - 2) % 2 == 1))
        def _():
            output(i, 1)

    for g, gs in enumerate(groups):
        out = acc_scr[g, 0:tq, :]
        for h in range(1, nh):
            out = jnp.where(lane_head == h, acc_scr[g, h * tq:(h + 1) * tq, :], out)
        o_ref[0, :, gs] = out


def _sb_attn(q, k, v, ntri, tq):
    b, l, w = q.shape
    ng = SB_GROUPS_PER_STEP
    gw = ng * V7X_LANES
    m = (V7X_LANES // SB_HEAD_DIM) * tq
    blk = pltpu.VMEM((ng, SB_SLOTS, m, tq), F32)
    nbytes = (2 * (2 * l * gw * 2 + tq * gw * 2 + tq * gw * 4 + tq * tq * 2)
              + ng * (3 * SB_SLOTS * m * tq * 4 + 3 * m * V7X_LANES * 4) + 6 * m * tq * 4)
    return pl.pallas_call(
        functools.partial(_sb_attn_kernel, tq=tq),
        grid=(b, w // gw, l // tq),
        in_specs=[pl.BlockSpec((1, tq, gw), lambda bb, hp, i: (bb, i, hp)),
                  pl.BlockSpec((1, l, gw), lambda bb, hp, i: (bb, 0, hp)),
                  pl.BlockSpec((1, l, gw), lambda bb, hp, i: (bb, 0, hp)),
                  pl.BlockSpec((tq, tq), lambda bb, hp, i: (0, 0))],
        out_specs=pl.BlockSpec((1, tq, gw), lambda bb, hp, i: (bb, i, hp)),
        out_shape=jax.ShapeDtypeStruct((b, l, w), F32),
        scratch_shapes=[pltpu.VMEM((ng, m, V7X_LANES), BF16), blk, blk, blk,
                        pltpu.VMEM((ng, m, 1), F32), pltpu.VMEM((ng, m, V7X_LANES), F32)],
        compiler_params=_params(("parallel", "parallel", "arbitrary"), nbytes),
        name="sb_attn",
    )(q, k, v, ntri)


def _conv_kernel(cur_ref, prev_ref, dww_ref, dwb_ref, lng_ref, lnb_ref, pw_ref, bg_ref, o_ref, hbuf, *, tc):
    i = pl.program_id(1)
    cur = cur_ref[0]
    prev = prev_ref[0]
    hbuf[CONV_HALO:, :] = cur[:, :CONV_CH] * _sigmoid(cur[:, CONV_CH:])
    hprev = prev[:, :CONV_CH] * _sigmoid(prev[:, CONV_CH:])
    hbuf[:CONV_HALO, :] = jnp.where(i > 0, hprev, 0.0)
    acc = jnp.zeros((tc, CONV_CH), F32) + dwb_ref[...]
    off = CONV_HALO - (CONV_WIDTH - 1)
    for j in range(CONV_WIDTH):
        acc = acc + dww_ref[j:j + 1, :] * hbuf[off + j:off + j + tc, :]
    mu = jnp.mean(acc, axis=-1, keepdims=True)
    cen = acc - mu
    var = jnp.mean(cen * cen, axis=-1, keepdims=True)
    y = cen * lax.rsqrt(var + EPS) * lng_ref[...] + lnb_ref[...]
    y = y * _sigmoid(y)
    o = _dot(y.astype(BF16), pw_ref[...])
    o_ref[0] = _rms_rows(o, bg_ref[...])


def _conv_branch(c, dww, dwb, lng, lnb, pw_bf, bg, tc):
    b, l, w = c.shape
    per_tile = tc // CONV_HALO
    const = lambda bb, i: (0, 0)
    nbytes = 2 * (tc * w * 4 + CONV_HALO * w * 4 + tc * CONV_CH * 4) + (tc + CONV_HALO) * CONV_CH * 4 * 6
    return pl.pallas_call(
        functools.partial(_conv_kernel, tc=tc),
        grid=(b, l // tc),
        in_specs=[pl.BlockSpec((1, tc, w), lambda bb, i: (bb, i, 0)),
                  pl.BlockSpec((1, CONV_HALO, w), lambda bb, i: (bb, jnp.maximum(i * per_tile - 1, 0), 0)),
                  pl.BlockSpec((CONV_WIDTH, CONV_CH), const), pl.BlockSpec((1, CONV_CH), const),
                  pl.BlockSpec((1, CONV_CH), const), pl.BlockSpec((1, CONV_CH), const),
                  pl.BlockSpec((CONV_CH, CONV_CH), const), pl.BlockSpec((1, CONV_CH), const)],
        out_specs=pl.BlockSpec((1, tc, CONV_CH), lambda bb, i: (bb, i, 0)),
        out_shape=jax.ShapeDtypeStruct((b, l, CONV_CH), F32),
        scratch_shapes=[pltpu.VMEM((tc + CONV_HALO, CONV_CH), F32)],
        compiler_params=_params(("parallel", "parallel"), nbytes),
        name="conv_branch",
    )(c, c, dww, dwb, lng, lnb, pw_bf, bg)


def _ssm_operators(lam_re, lam_im, log_dt, b_re, b_im, c_re, c_im, d):
    hp = lax.Precision.HIGHEST
    lr, li = lam_re.astype(F32), lam_im.astype(F32)
    dt = jnp.exp(log_dt.astype(F32))[:, None]
    mag = jnp.exp(lr * dt)
    ar, ai = mag * jnp.cos(li * dt), mag * jnp.sin(li * dt)
    den = lr * lr + li * li
    fr = ((ar - 1.0) * lr + ai * li) / den
    fi = (ai * lr - (ar - 1.0) * li) / den
    br, bi = b_re.astype(F32), b_im.astype(F32)
    bbr = fr[..., None] * br - fi[..., None] * bi
    bbi = fr[..., None] * bi + fi[..., None] * br
    cr, ci = c_re.astype(F32), c_im.astype(F32)
    n = jnp.arange(SSM_CHUNK + 1, dtype=F32)[:, None, None]
    pmag = jnp.exp(n * (lr * dt)[None])
    pr, pi = pmag * jnp.cos(n * (li * dt)[None]), pmag * jnp.sin(n * (li * dt)[None])
    abr = pr[:, :, :, None] * bbr[None] - pi[:, :, :, None] * bbi[None]
    abi = pr[:, :, :, None] * bbi[None] + pi[:, :, :, None] * bbr[None]
    kern = (jnp.einsum('ngpi,gop->ngio', abr, cr, precision=hp)
            - jnp.einsum('ngpi,gop->ngio', abi, ci, precision=hp))
    s_idx = jnp.arange(SSM_CHUNK)[:, None]
    t_idx = jnp.arange(SSM_CHUNK)[None, :]
    lag = t_idx - s_idx
    k_st = kern[jnp.clip(lag, 0, SSM_CHUNK)]
    k_st = jnp.where((lag >= 0)[:, :, None, None, None], k_st, 0.0)
    w_intra = k_st.transpose(2, 0, 3, 1, 4).reshape(SSM_GROUPS, SSM_CHUNK * SSM_GROUP, SSM_CHUNK * SSM_GROUP)
    rev = SSM_CHUNK - 1 - jnp.arange(SSM_CHUNK)
    w_st_r = abr[rev].transpose(1, 0, 3, 2).reshape(SSM_GROUPS, SSM_CHUNK * SSM_GROUP, SSM_STATE)
    w_st_i = abi[rev].transpose(1, 0, 3, 2).reshape(SSM_GROUPS, SSM_CHUNK * SSM_GROUP, SSM_STATE)
    p1r, p1i = pr[1:], pi[1:]
    o_r = cr[None] * p1r[:, :, None, :] - ci[None] * p1i[:, :, None, :]
    o_i = -(cr[None] * p1i[:, :, None, :] + ci[None] * p1r[:, :, None, :])
    w_out_r = o_r.transpose(1, 3, 0, 2).reshape(SSM_GROUPS, SSM_STATE, SSM_CHUNK * SSM_GROUP)
    w_out_i = o_i.transpose(1, 3, 0, 2).reshape(SSM_GROUPS, SSM_STATE, SSM_CHUNK * SSM_GROUP)
    a_n = jnp.stack([pr[SSM_CHUNK].reshape(1, -1), pi[SSM_CHUNK].reshape(1, -1)])
    d_c = jnp.tile(d.astype(F32).reshape(SSM_GROUPS, 1, SSM_GROUP), (1, 1, SSM_CHUNK))
    return (w_intra.astype(BF16), w_st_r.astype(BF16), w_st_i.astype(BF16),
            w_out_r.astype(BF16), w_out_i.astype(BF16), a_n, d_c)


def _ssm_kernel(u_ref, wi_ref, wsr_ref, wsi_ref, wor_ref, woi_ref, an_ref, dc_ref, y_ref,
                sr_scr, si_scr, xr_scr, xi_scr, *, nc):
    p = SSM_STATE
    for g in range(SSM_GROUPS):
        ub = u_ref[0, g].astype(BF16)
        sr_scr[:, g * p:(g + 1) * p] = _dot(ub, wsr_ref[g])
        si_scr[:, g * p:(g + 1) * p] = _dot(ub, wsi_ref[g])
    ar = an_ref[0]
    ai = an_ref[1]

    def body(c8, carry):
        xr, xi = carry
        r0 = pl.multiple_of(c8 * V7X_SUBLANES, V7X_SUBLANES)
        sr = sr_scr[pl.ds(r0, V7X_SUBLANES), :]
        si = si_scr[pl.ds(r0, V7X_SUBLANES), :]
        rows_r, rows_i = [], []
        for r in range(V7X_SUBLANES):
            rows_r.append(xr)
            rows_i.append(xi)
            xr, xi = (ar * xr - ai * xi + sr[r:r + 1, :], ar * xi + ai * xr + si[r:r + 1, :])
        xr_scr[pl.ds(r0, V7X_SUBLANES), :] = jnp.concatenate(rows_r, axis=0)
        xi_scr[pl.ds(r0, V7X_SUBLANES), :] = jnp.concatenate(rows_i, axis=0)
        return xr, xi

    zero = jnp.zeros((1, SSM_GROUPS * p), F32)
    lax.fori_loop(0, nc // V7X_SUBLANES, body, (zero, zero))
    for g in range(SSM_GROUPS):
        uf = u_ref[0, g]
        y = _dot(uf.astype(BF16), wi_ref[g])
        y = y + _dot(xr_scr[:, g * p:(g + 1) * p].astype(BF16), wor_ref[g])
        y = y + _dot(xi_scr[:, g * p:(g + 1) * p].astype(BF16), woi_ref[g])
        y_ref[0, g] = y + dc_ref[g] * uf


def _ssm_scan(uc, ops):
    w_intra, w_st_r, w_st_i, w_out_r, w_out_i, a_n, d_c = ops
    b, g, nc, cw = uc.shape
    gp = SSM_GROUPS * SSM_STATE
    full = lambda shape: pl.BlockSpec(shape, lambda bb: (0,) * len(shape))
    nbytes = 4 * g * nc * cw * 4 + 4 * nc * gp * 4 + 4 * g * cw * cw * 2
    return pl.pallas_call(
        functools.partial(_ssm_kernel, nc=nc),
        grid=(b,),
        in_specs=[pl.BlockSpec((1, g, nc, cw), lambda bb: (bb, 0, 0, 0)),
                  full(w_intra.shape), full(w_st_r.shape), full(w_st_i.shape),
                  full(w_out_r.shape), full(w_out_i.shape), full(a_n.shape), full(d_c.shape)],
        out_specs=pl.BlockSpec((1, g, nc, cw), lambda bb: (bb, 0, 0, 0)),
        out_shape=jax.ShapeDtypeStruct((b, g, nc, cw), F32),
        scratch_shapes=[pltpu.VMEM((nc, gp), F32) for _ in range(4)],
        compiler_params=_params(("parallel",), nbytes),
        name="ssm_scan",
    )(uc, w_intra, w_st_r, w_st_i, w_out_r, w_out_i, a_n, d_c)


def _ssm_glu_kernel(y_ref, w_ref, bg_ref, o_ref):
    z = _dot(y_ref[...].astype(BF16), w_ref[...])
    o = z[:, :SSM_CH] * _sigmoid(z[:, SSM_CH:])
    o_ref[...] = _rms_rows(o, bg_ref[...])


def _ssm_glu(y2, w_bf, bg, tm):
    t = y2.shape[0]
    row = lambda i: (i, 0)
    const = lambda i: (0, 0)
    nbytes = 2 * (2 * tm * SSM_CH * 4 + SSM_CH * 2 * SSM_CH * 2) + 4 * tm * SSM_CH * 4
    return pl.pallas_call(
        _ssm_glu_kernel,
        grid=(t // tm,),
        in_specs=[pl.BlockSpec((tm, SSM_CH), row), pl.BlockSpec((SSM_CH, 2 * SSM_CH), const),
                  pl.BlockSpec((1, SSM_CH), const)],
        out_specs=pl.BlockSpec((tm, SSM_CH), row),
        out_shape=jax.ShapeDtypeStruct((t, SSM_CH), F32),
        compiler_params=_params(("parallel",), nbytes),
        name="ssm_glu",
    )(y2, w_bf, bg)


def _mix_out_kernel(x_ref, sb_ref, cv_ref, sm_ref, bg_ref, w_ref, o_ref):
    sb = _rms_rows(sb_ref[...], bg_ref[...]).astype(BF16)
    s1 = SB_WIDTH
    s2 = SB_WIDTH + CONV_CH
    y = _dot(sb, w_ref[0:s1, :])
    y = y + _dot(cv_ref[...].astype(BF16), w_ref[s1:s2, :])
    y = y + _dot(sm_ref[...].astype(BF16), w_ref[s2:, :])
    o_ref[...] = x_ref[...] + y


def _mix_out(x2, sb2, cv2, sm2, bg_sb, w_bf, tm):
    t, d = x2.shape
    row = lambda i: (i, 0)
    const = lambda i: (0, 0)
    nbytes = 2 * (2 * tm * d * 4 + tm * (SB_WIDTH + CONV_CH + SSM_CH) * 4 + d * d * 2) + 2 * tm * d * 4
    return pl.pallas_call(
        _mix_out_kernel,
        grid=(t // tm,),
        in_specs=[pl.BlockSpec((tm, d), row), pl.BlockSpec((tm, SB_WIDTH), row),
                  pl.BlockSpec((tm, CONV_CH), row), pl.BlockSpec((tm, SSM_CH), row),
                  pl.BlockSpec((1, SB_WIDTH), const), pl.BlockSpec(w_bf.shape, const)],
        out_specs=pl.BlockSpec((tm, d), row),
        out_shape=jax.ShapeDtypeStruct((t, d), F32),
        compiler_params=_params(("parallel",), nbytes),
        name="mix_out",
    )(x2, sb2, cv2, sm2, bg_sb, w_bf)


def _mem_kv_kernel(m_ref, g_ref, wk_ref, wv_ref, kg_ref, k_ref, v_ref):
    hm = _rms_rows(m_ref[0], g_ref[...]).astype(BF16)
    kk = _dot(hm, wk_ref[...])
    for hh in range(XA_HEADS):
        hs = slice(hh * XA_HEAD_DIM, (hh + 1) * XA_HEAD_DIM)
        k_ref[0, :, hs] = _rms_rows(kk[:, hs], kg_ref[...]).astype(BF16)
    v_ref[0] = _dot(hm, wv_ref[...]).astype(BF16)


def _mem_kv(mem, g, wk_bf, wv_bf, kg):
    b, n, d = mem.shape
    const = lambda bb: (0, 0)
    blk = pl.BlockSpec((1, n, d), lambda bb: (bb, 0, 0))
    nbytes = 2 * (n * d * 4 + 2 * d * d * 2 + 2 * n * d * 2) + 4 * n * d * 4
    return pl.pallas_call(
        _mem_kv_kernel,
        grid=(b,),
        in_specs=[blk, pl.BlockSpec((1, d), const), pl.BlockSpec((d, d), const),
                  pl.BlockSpec((d, d), const), pl.BlockSpec((1, XA_HEAD_DIM), const)],
        out_specs=[blk, blk],
        out_shape=[jax.ShapeDtypeStruct((b, n, d), BF16), jax.ShapeDtypeStruct((b, n, d), BF16)],
        compiler_params=_params(("parallel",), nbytes),
        name="mem_kv",
    )(mem, g, wk_bf, wv_bf, kg)


def _xattn_kernel(x_ref, g_ref, wq_ref, qg_ref, k_ref, v_ref, wo_ref, o_ref, ob_scr):
    x = x_ref[0]
    hx = _rms_rows(x, g_ref[...]).astype(BF16)
    q = _dot(hx, wq_ref[...])
    scale = XA_HEAD_DIM ** -0.5
    for hh in range(XA_HEADS):
        hs = slice(hh * XA_HEAD_DIM, (hh + 1) * XA_HEAD_DIM)
        qh = (_rms_rows(q[:, hs], qg_ref[...]) * scale).astype(BF16)
        s = _dot_nt(qh, k_ref[0, :, hs])
        s = s - jnp.max(s, axis=-1, keepdims=True)
        e = jnp.exp(s)
        p = e / jnp.sum(e, axis=-1, keepdims=True)
        ob_scr[:, hs] = _dot(p.astype(BF16), v_ref[0, :, hs]).astype(BF16)
    o_ref[0] = x + _dot(ob_scr[...], wo_ref[...])


def _xattn(x, g, wq_bf, qg, k_bf, v_bf, wo_bf, tm):
    b, l, d = x.shape
    n = k_bf.shape[1]
    const = lambda bb, i: (0, 0)
    row = pl.BlockSpec((1, tm, d), lambda bb, i: (bb, i, 0))
    kv = pl.BlockSpec((1, n, d), lambda bb, i: (bb, 0, 0))
    nbytes = 2 * (2 * tm * d * 4 + 2 * d * d * 2 + 2 * n * d * 2) + 4 * tm * d * 4
    return pl.pallas_call(
        _xattn_kernel,
        grid=(b, l // tm),
        in_specs=[row, pl.BlockSpec((1, d), const), pl.BlockSpec((d, d), const),
                  pl.BlockSpec((1, XA_HEAD_DIM), const), kv, kv, pl.BlockSpec((d, d), const)],
        out_specs=row,
        out_shape=jax.ShapeDtypeStruct((b, l, d), F32),
        scratch_shapes=[pltpu.VMEM((tm, d), BF16)],
        compiler_params=_params(("parallel", "parallel"), nbytes),
        name="xattn",
    )(x, g, wq_bf, qg, k_bf, v_bf, wo_bf)


def _ffn_kernel(x_ref, g_ref, wg_ref, wu_ref, wo_ref, o_ref, h_scr, acc_scr):
    j = pl.program_id(1)

    @pl.when(j == 0)
    def _():
        h_scr[...] = _rms_rows(x_ref[...], g_ref[...]).astype(BF16)
        acc_scr[...] = x_ref[...]

    h = h_scr[...]
    gate = _dot(h, wg_ref[...])
    up = _dot(h, wu_ref[...])
    act = (gate * _sigmoid(gate) * up).astype(BF16)
    acc_scr[...] += _dot(act, wo_ref[...])

    @pl.when(j == pl.num_programs(1) - 1)
    def _():
        o_ref[...] = acc_scr[...]


def _ffn(x2, g, w_in_bf, w_out_bf, tm, th):
    t, d = x2.shape
    hidden = w_out_bf.shape[0]
    nh = hidden // th
    nbytes = 2 * (2 * tm * d * 4 + 2 * d * th * 2 + th * d * 2) + tm * d * 6 + 4 * tm * th * 4
    return pl.pallas_call(
        _ffn_kernel,
        grid=(t // tm, nh),
        in_specs=[pl.BlockSpec((tm, d), lambda i, j: (i, 0)), pl.BlockSpec((1, d), lambda i, j: (0, 0)),
                  pl.BlockSpec((d, th), lambda i, j: (0, j)),
                  pl.BlockSpec((d, th), lambda i, j: (0, j + nh)),
                  pl.BlockSpec((th, d), lambda i, j: (j, 0))],
        out_specs=pl.BlockSpec((tm, d), lambda i, j: (i, 0)),
        out_shape=jax.ShapeDtypeStruct((t, d), F32),
        scratch_shapes=[pltpu.VMEM((tm, d), BF16), pltpu.VMEM((tm, d), F32)],
        compiler_params=_params(("parallel", "arbitrary"), nbytes),
        name="ffn",
    )(x2, g, w_in_bf, w_in_bf, w_out_bf)


def _tile(n, want):
    want = min(want, n)
    for cand in range(want, 0, -1):
        if n % cand == 0 and (cand % V7X_SUBLANES == 0 or cand == n):
            return cand
    return n


def kernel(x, mem, norm_mix_g, w_in, sb_q_norm_g, sb_k_norm_g, conv_dw_w, conv_dw_b, conv_ln_g, conv_ln_b, conv_pw2_w, ssm_lam_re, ssm_lam_im, ssm_log_dt, ssm_b_re, ssm_b_im, ssm_c_re, ssm_c_im, ssm_d, ssm_glu_w, branch_norm_g, w_out, norm_xa_g, norm_mem_g, xa_wq, xa_wk, xa_wv, xa_q_norm_g, xa_k_norm_g, xa_wo, norm_ffn_g, ffn_w_in, ffn_w_out):
    bsz, seq, d = x.shape
    depth = w_in.shape[0]
    t = bsz * seq
    nc = seq // SSM_CHUNK
    tm = _tile(t, 512)
    tq = _tile(seq, 256)
    tc = _tile(seq, 512)
    tm_ffn = _tile(t, 1024)
    th = 256
    assert seq % SSM_CHUNK == 0 and nc % V7X_SUBLANES == 0 and tc % CONV_HALO == 0

    seg = (jnp.arange(SB_WIDTH)[:, None] // SB_HEAD_DIM == jnp.arange(SB_WIDTH)[None, :] // SB_HEAD_DIM).astype(BF16)
    ntri = -(jnp.arange(tq)[:, None] > jnp.arange(tq)[None, :]).astype(BF16)
    row = lambda a: a.reshape(1, -1).astype(F32)

    x2 = x.reshape(t, d)
    for l in range(depth):
        qg = row(jnp.tile(sb_q_norm_g[l], SB_HEADS)) * (SB_HEAD_DIM ** -0.5)
        kg = row(jnp.tile(sb_k_norm_g[l], SB_HEADS))
        q, k, v, c, u = _mix_in(x2, row(norm_mix_g[l]), w_in[l].astype(BF16), qg, kg, seg, tm)
        o_sb = _sb_attn(q.reshape(bsz, seq, SB_WIDTH), k.reshape(bsz, seq, SB_WIDTH),
                        v.reshape(bsz, seq, SB_WIDTH), ntri, tq)
        bg = branch_norm_g[l].astype(F32)
        o_conv = _conv_branch(c.reshape(bsz, seq, 2 * CONV_CH), conv_dw_w[l].astype(F32), row(conv_dw_b[l]),
                              row(conv_ln_g[l]), row(conv_ln_b[l]), conv_pw2_w[l].astype(BF16),
                              row(bg[SB_WIDTH:SB_WIDTH + CONV_CH]), tc)
        ops = _ssm_operators(ssm_lam_re[l], ssm_lam_im[l], ssm_log_dt[l], ssm_b_re[l], ssm_b_im[l],
                             ssm_c_re[l], ssm_c_im[l], ssm_d[l])
        uc = u.reshape(bsz, nc, SSM_CHUNK, SSM_GROUPS, SSM_GROUP).transpose(0, 3, 1, 2, 4)
        uc = uc.reshape(bsz, SSM_GROUPS, nc, SSM_CHUNK * SSM_GROUP)
        yc = _ssm_scan(uc, ops)
        y = yc.reshape(bsz, SSM_GROUPS, nc, SSM_CHUNK, SSM_GROUP).transpose(0, 2, 3, 1, 4).reshape(t, SSM_CH)
        o_ssm = _ssm_glu(y, ssm_glu_w[l].astype(BF16), row(bg[SB_WIDTH + CONV_CH:]), tm)
        x2 = _mix_out(x2, o_sb.reshape(t, SB_WIDTH), o_conv.reshape(t, CONV_CH), o_ssm,
                      row(bg[:SB_WIDTH]), w_out[l].astype(BF16), tm)
        k_m, v_m = _mem_kv(mem, row(norm_mem_g[l]), xa_wk[l].astype(BF16), xa_wv[l].astype(BF16),
                           row(xa_k_norm_g[l]))
        x2 = _xattn(x2.reshape(bsz, seq, d), row(norm_xa_g[l]), xa_wq[l].astype(BF16), row(xa_q_norm_g[l]),
                    k_m, v_m, xa_wo[l].astype(BF16), tm if seq % tm == 0 else _tile(seq, 512)).reshape(t, d)
        x2 = _ffn(x2, row(norm_ffn_g[l]), ffn_w_in[l].astype(BF16), ffn_w_out[l].astype(BF16), tm_ffn, th)
    return x2.reshape(bsz, seq, d)
```

```python
import functools
import math

import jax
import jax.numpy as jnp
from jax import lax
from jax.experimental import pallas as pl
from jax.experimental.pallas import tpu as pltpu

F32 = jnp.float32
BF16 = jnp.bfloat16
EPS = 1e-6

V7X_LANES = 128
V7X_SUBLANES = 8
V7X_VMEM_BYTES = 64 * 1024 * 1024

SB_HEADS = 8
SB_HEAD_DIM = 64
SB_WIDTH = SB_HEADS * SB_HEAD_DIM
CONV_CH = 256
CONV_WIDTH = 31
CONV_HALO = 32
SSM_CH = 256
SSM_GROUP = 16
SSM_GROUPS = SSM_CH // SSM_GROUP
SSM_STATE = 64
SSM_CHUNK = 16
XA_HEADS = 4
XA_HEAD_DIM = 256


def _vmem_limit(nbytes):
    return int(min(max(nbytes * 3 // 2, 16 * 1024 * 1024), V7X_VMEM_BYTES - 8 * 1024 * 1024))


def _params(semantics, nbytes):
    return pltpu.CompilerParams(dimension_semantics=semantics, vmem_limit_bytes=_vmem_limit(nbytes))


def _dot(a, b):
    return jnp.dot(a, b, preferred_element_type=F32)


def _dot_nt(a, b):
    return lax.dot_general(a, b, (((1,), (1,)), ((), ())), preferred_element_type=F32)


def _rms_rows(xf, g):
    return xf * lax.rsqrt(jnp.mean(xf * xf, axis=-1, keepdims=True) + EPS) * g


def _sigmoid(x):
    return 1.0 / (1.0 + jnp.exp(-x))


def _segment_mean_sq(p, seg):
    sq = p * p
    hi = sq.astype(BF16)
    lo = (sq - hi.astype(F32)).astype(BF16)
    return (_dot(hi, seg) + _dot(lo, seg)) * (1.0 / SB_HEAD_DIM)


def _mix_in_kernel(x_ref, g_ref, w_ref, qg_ref, kg_ref, seg_ref,
                   q_ref, k_ref, v_ref, c_ref, u_ref):
    h = _rms_rows(x_ref[...], g_ref[...]).astype(BF16)
    s1, s2, s3 = SB_WIDTH, 2 * SB_WIDTH, 3 * SB_WIDTH
    s4 = s3 + 2 * CONV_CH
    seg = seg_ref[...]
    pq = _dot(h, w_ref[:, 0:s1])
    q_ref[...] = (pq * lax.rsqrt(_segment_mean_sq(pq, seg) + EPS) * qg_ref[...]).astype(BF16)
    pk = _dot(h, w_ref[:, s1:s2])
    k_ref[...] = (pk * lax.rsqrt(_segment_mean_sq(pk, seg) + EPS) * kg_ref[...]).astype(BF16)
    v_ref[...] = _dot(h, w_ref[:, s2:s3]).astype(BF16)
    c_ref[...] = _dot(h, w_ref[:, s3:s4])
    u_ref[...] = _dot(h, w_ref[:, s4:])


def _mix_in(x2, g, w_bf, qg, kg, seg, tm):
    t, d = x2.shape
    n_in = w_bf.shape[1]
    row = lambda i: (i, 0)
    const = lambda i: (0, 0)
    nbytes = 2 * (tm * d * 4 + d * n_in * 2 + tm * (3 * SB_WIDTH * 2 + 2 * CONV_CH * 4 + SSM_CH * 4)) + tm * n_in * 4
    return pl.pallas_call(
        _mix_in_kernel,
        grid=(t // tm,),
        in_specs=[pl.BlockSpec((tm, d), row), pl.BlockSpec((1, d), const),
                  pl.BlockSpec((d, n_in), const), pl.BlockSpec((1, SB_WIDTH), const),
                  pl.BlockSpec((1, SB_WIDTH), const), pl.BlockSpec((SB_WIDTH, SB_WIDTH), const)],
        out_specs=[pl.BlockSpec((tm, SB_WIDTH), row), pl.BlockSpec((tm, SB_WIDTH), row),
                   pl.BlockSpec((tm, SB_WIDTH), row), pl.BlockSpec((tm, 2 * CONV_CH), row),
                   pl.BlockSpec((tm, SSM_CH), row)],
        out_shape=[jax.ShapeDtypeStruct((t, SB_WIDTH), BF16), jax.ShapeDtypeStruct((t, SB_WIDTH), BF16),
                   jax.ShapeDtypeStruct((t, SB_WIDTH), BF16), jax.ShapeDtypeStruct((t, 2 * CONV_CH), F32),
                   jax.ShapeDtypeStruct((t, SSM_CH), F32)],
        compiler_params=_params(("parallel",), nbytes),
        name="mix_in",
    )(x2, g, w_bf, qg, kg, seg)


SB_SLOTS = 2
SB_GROUPS_PER_STEP = 2
SB_SOFTPLUS_LINEAR = 40.0
SB_EXP_UNDERFLOW = -104.0
SB_NEVER = -1e30


def _sb_decay(z, run, ntri, mask):
    sp = jnp.maximum(jnp.log(1.0 + jnp.exp(jnp.minimum(z, SB_SOFTPLUS_LINEAR))), z)
    if mask is not None:
        sp = jnp.where(mask, sp, 0.0)
    later = _dot(sp.astype(BF16), ntri)
    return z - sp + run, later, run + later[:, 0:1] - sp[:, 0:1]


def _sb_weights(t0, later, mask):
    w = jnp.exp(t0 + later)
    if mask is not None:
        w = jnp.where(mask, w, 0.0)
    return w.astype(BF16)


def _sb_attn_kernel(q_ref, k_ref, v_ref, ntri_ref, o_ref, q2_scr, z_scr, t0_scr, lat_scr, run_scr, acc_scr, *, tq):
    i = pl.program_id(2)
    nh = V7X_LANES // SB_HEAD_DIM
    m = nh * tq
    groups = [slice(g * V7X_LANES, (g + 1) * V7X_LANES) for g in range(SB_GROUPS_PER_STEP)]

    def rows(j):
        return pl.ds(pl.multiple_of((i - jnp.minimum(j, i)) * tq, tq), tq)

    def scores(j, slot):
        for g, gs in enumerate(groups):
            z_scr[g, slot] = _dot_nt(q2_scr[g], k_ref[0, rows(j), gs])

    def decay(slot, mask=None, bias=None):
        for g in range(len(groups)):
            run = run_scr[g] if bias is None else run_scr[g] + bias
            t0, later, run = _sb_decay(z_scr[g, slot], run, ntri_ref[...], mask)
            t0_scr[g, slot] = t0
            lat_scr[g, slot] = later
            run_scr[g] = run

    def output(j, slot, mask=None):
        for g, gs in enumerate(groups):
            acc_scr[g] += _dot(_sb_weights(t0_scr[g, slot], lat_scr[g, slot], mask), v_ref[0, rows(j), gs])

    def live():
        run = run_scr[0]
        for g in range(1, len(groups)):
            run = jnp.maximum(run, run_scr[g])
        return (jnp.max(run) >= SB_EXP_UNDERFLOW).astype(jnp.int32)

    lane_head = lax.broadcasted_iota(jnp.int32, (tq, V7X_LANES), 1) // SB_HEAD_DIM
    for g, gs in enumerate(groups):
        q = q_ref[0, :, gs]
        for h in range(nh):
            q2_scr[g, h * tq:(h + 1) * tq, :] = jnp.where(lane_head == h, q, jnp.zeros_like(q))
    row = lax.broadcasted_iota(jnp.int32, (m, tq), 0) % tq
    col = lax.broadcasted_iota(jnp.int32, (m, tq), 1)
    diag_mask = col < row
    run_scr[...] = jnp.zeros_like(run_scr)
    acc_scr[...] = jnp.zeros_like(acc_scr)

    scores(0, 0)
    scores(1, 1)
    decay(0, mask=diag_mask)
    decay(1, bias=jnp.where(i >= 1, 0.0, SB_NEVER))
    output(0, 0, diag_mask)
    output(1, 1)

    @pl.when(jnp.logical_and(i >= 2, live() > 0))
    def _():
        scores(2, 0)
        decay(0)
        scores(3, 1)
        more = live()
        output(2, 0)
        scores(4, 0)
        decay(1)
        n_pairs = (i - 2) // 2

        def cond(carry):
            p, more = carry
            return jnp.logical_and(p < n_pairs, more > 0)

        def body(carry):
            p, _ = carry
            t = 2 * p + 3
            scores(t + 2, 1)
            output(t, 1)
            decay(0)
            more = live()
            scores(t + 3, 0)
            output(t + 1, 0)
            decay(1)
            return p + 1, more

        _, more = lax.while_loop(cond, body, (0, more))

        @pl.when(jnp.logical_and(more > 0, (i - 2) % 2 == 1))
        def _():
            output(i, 1)

    for g, gs in enumerate(groups):
        out = acc_scr[g, 0:tq, :]
        for h in range(1, nh):
            out = jnp.where(lane_head == h, acc_scr[g, h * tq:(h + 1) * tq, :], out)
        o_ref[0, :, gs] = out


def _sb_attn(q, k, v, ntri, tq):
    b, l, w = q.shape
    ng = SB_GROUPS_PER_STEP
    gw = ng * V7X_LANES
    m = (V7X_LANES // SB_HEAD_DIM) * tq
    blk = pltpu.VMEM((ng, SB_SLOTS, m, tq), F32)
    nbytes = (2 * (2 * l * gw * 2 + tq * gw * 2 + tq * gw * 4 + tq * tq * 2)
              + ng * (3 * SB_SLOTS * m * tq * 4 + 3 * m * V7X_LANES * 4) + 6 * m * tq * 4)
    return pl.pallas_call(
        functools.partial(_sb_attn_kernel, tq=tq),
        grid=(b, w // gw, l // tq),
        in_specs=[pl.BlockSpec((1, tq, gw), lambda bb, hp, i: (bb, i, hp)),
                  pl.BlockSpec((1, l, gw), lambda bb, hp, i: (bb, 0, hp)),
                  pl.BlockSpec((1, l, gw), lambda bb, hp, i: (bb, 0, hp)),
                  pl.BlockSpec((tq, tq), lambda bb, hp, i: (0, 0))],
        out_specs=pl.BlockSpec((1, tq, gw), lambda bb, hp, i: (bb, i, hp)),
        out_shape=jax.ShapeDtypeStruct((b, l, w), F32),
        scratch_shapes=[pltpu.VMEM((ng, m, V7X_LANES), BF16), blk, blk, blk,
                        pltpu.VMEM((ng, m, 1), F32), pltpu.VMEM((ng, m, V7X_LANES), F32)],
        compiler_params=_params(("parallel", "parallel", "arbitrary"), nbytes),
        name="sb_attn",
    )(q, k, v, ntri)


def _conv_kernel(cur_ref, prev_ref, dww_ref, dwb_ref, lng_ref, lnb_ref, pw_ref, bg_ref, o_ref, hbuf, *, tc):
    i = pl.program_id(1)
    cur = cur_ref[0]
    prev = prev_ref[0]
    hbuf[CONV_HALO:, :] = cur[:, :CONV_CH] * _sigmoid(cur[:, CONV_CH:])
    hprev = prev[:, :CONV_CH] * _sigmoid(prev[:, CONV_CH:])
    hbuf[:CONV_HALO, :] = jnp.where(i > 0, hprev, 0.0)
    acc = jnp.zeros((tc, CONV_CH), F32) + dwb_ref[...]
    off = CONV_HALO - (CONV_WIDTH - 1)
    for j in range(CONV_WIDTH):
        acc = acc + dww_ref[j:j + 1, :] * hbuf[off + j:off + j + tc, :]
    mu = jnp.mean(acc, axis=-1, keepdims=True)
    cen = acc - mu
    var = jnp.mean(cen * cen, axis=-1, keepdims=True)
    y = cen * lax.rsqrt(var + EPS) * lng_ref[...] + lnb_ref[...]
    y = y * _sigmoid(y)
    o = _dot(y.astype(BF16), pw_ref[...])
    o_ref[0] = _rms_rows(o, bg_ref[...])


def _conv_branch(c, dww, dwb, lng, lnb, pw_bf, bg, tc):
    b, l, w = c.shape
    per_tile = tc // CONV_HALO
    const = lambda bb, i: (0, 0)
    nbytes = 2 * (tc * w * 4 + CONV_HALO * w * 4 + tc * CONV_CH * 4) + (tc + CONV_HALO) * CONV_CH * 4 * 6
    return pl.pallas_call(
        functools.partial(_conv_kernel, tc=tc),
        grid=(b, l // tc),
        in_specs=[pl.BlockSpec((1, tc, w), lambda bb, i: (bb, i, 0)),
                  pl.BlockSpec((1, CONV_HALO, w), lambda bb, i: (bb, jnp.maximum(i * per_tile - 1, 0), 0)),
                  pl.BlockSpec((CONV_WIDTH, CONV_CH), const), pl.BlockSpec((1, CONV_CH), const),
                  pl.BlockSpec((1, CONV_CH), const), pl.BlockSpec((1, CONV_CH), const),
                  pl.BlockSpec((CONV_CH, CONV_CH), const), pl.BlockSpec((1, CONV_CH), const)],
        out_specs=pl.BlockSpec((1, tc, CONV_CH), lambda bb, i: (bb, i, 0)),
        out_shape=jax.ShapeDtypeStruct((b, l, CONV_CH), F32),
        scratch_shapes=[pltpu.VMEM((tc + CONV_HALO, CONV_CH), F32)],
        compiler_params=_params(("parallel", "parallel"), nbytes),
        name="conv_branch",
    )(c, c, dww, dwb, lng, lnb, pw_bf, bg)


def _ssm_operators(lam_re, lam_im, log_dt, b_re, b_im, c_re, c_im):
    lr, li = lam_re.astype(F32), lam_im.astype(F32)
    dt = jnp.exp(log_dt.astype(F32))[:, None]
    mag = jnp.exp(lr * dt)
    ar, ai = mag * jnp.cos(li * dt), mag * jnp.sin(li * dt)
    den = lr * lr + li * li
    fr = ((ar - 1.0) * lr + ai * li) / den
    fi = (ai * lr - (ar - 1.0) * li) / den
    br, bi = b_re.astype(F32), b_im.astype(F32)
    bbr = fr[..., None] * br - fi[..., None] * bi
    bbi = fr[..., None] * bi + fi[..., None] * br
    cr, ci = c_re.astype(F32), c_im.astype(F32)
    eye = jnp.eye(SSM_GROUPS, dtype=F32)
    gp = SSM_GROUPS * SSM_STATE

    def rows_gh(w):
        return jnp.einsum('gph,gk->ghkp', w, eye).reshape(SSM_CH, gp)

    def rows_gp(w):
        return jnp.einsum('ghp,gk->gpkh', w, eye).reshape(gp, SSM_CH)

    b_op = jnp.concatenate([rows_gh(bbr), rows_gh(bbi)], axis=1)
    c_op = jnp.concatenate([rows_gp(cr), rows_gp(-ci)], axis=0)
    n = jnp.arange(1, SSM_CHUNK + 1, dtype=F32)[:, None]
    pmag = jnp.exp(n * (lr * dt).reshape(1, gp))
    ang = n * (li * dt).reshape(1, gp)
    return b_op.astype(BF16), c_op.astype(BF16), pmag * jnp.cos(ang), pmag * jnp.sin(ang)


def _ssm_kernel(u_ref, perm_ref, permt_ref, b_ref, c_ref, pr_ref, pi_ref, d_ref, gw_ref, bg_ref, o_ref,
                xr_scr, xi_scr, er_scr, ei_scr, cr_scr, ci_scr, *, tm):
    gp = SSM_GROUPS * SSM_STATE
    nchunk = tm // SSM_CHUNK

    @pl.when(pl.program_id(1) == 0)
    def _():
        cr_scr[...] = jnp.zeros_like(cr_scr)
        ci_scr[...] = jnp.zeros_like(ci_scr)

    def pos(s):
        return pl.ds(s * nchunk, nchunk)

    u = u_ref[0]
    up = _dot(perm_ref[...], u.astype(BF16)).astype(BF16)
    bu = _dot(up, b_ref[...])
    xr_scr[...] = bu[:, :gp]
    xi_scr[...] = bu[:, gp:]
    ar, ai = pr_ref[0:1, :], pi_ref[0:1, :]
    xr, xi = xr_scr[pos(0), :], xi_scr[pos(0), :]
    for s in range(1, SSM_CHUNK):
        xr, xi = (ar * xr - ai * xi + xr_scr[pos(s), :], ar * xi + ai * xr + xi_scr[pos(s), :])
        xr_scr[pos(s), :] = xr
        xi_scr[pos(s), :] = xi
    nr, ni = pr_ref[SSM_CHUNK - 1:SSM_CHUNK, :], pi_ref[SSM_CHUNK - 1:SSM_CHUNK, :]
    er, ei = cr_scr[...], ci_scr[...]
    for c in range(nchunk):
        er_scr[c:c + 1, :] = er
        ei_scr[c:c + 1, :] = ei
        er, ei = (nr * er - ni * ei + xr[c:c + 1, :], nr * ei + ni * er + xi[c:c + 1, :])
    cr_scr[...] = er
    ci_scr[...] = ei
    er, ei = er_scr[...], ei_scr[...]
    for s in range(SSM_CHUNK):
        sr, si = pr_ref[s:s + 1, :], pi_ref[s:s + 1, :]
        xr_scr[pos(s), :] += sr * er - si * ei
        xi_scr[pos(s), :] += sr * ei + si * er
    yp = _dot(xr_scr[...].astype(BF16), c_ref[0:gp, :]) + _dot(xi_scr[...].astype(BF16), c_ref[gp:, :])
    hi = yp.astype(BF16)
    lo = (yp - hi.astype(F32)).astype(BF16)
    y = _dot(permt_ref[...], hi) + _dot(permt_ref[...], lo) + d_ref[...] * u
    z = _dot(y.astype(BF16), gw_ref[...])
    o_ref[0] = _rms_rows(z[:, :SSM_CH] * _sigmoid(z[:, SSM_CH:]), bg_ref[...])


def _ssm_branch(u, ops, d, gw_bf, bg, tm):
    b_op, c_op, pow_r, pow_i = ops
    b, l, ch = u.shape
    gp = SSM_GROUPS * SSM_STATE
    nchunk = tm // SSM_CHUNK
    full = lambda a: pl.BlockSpec(a.shape, lambda bb, i: (0,) * a.ndim)
    r = jnp.arange(tm)
    perm = (r[None, :] == ((r % nchunk) * SSM_CHUNK + r // nchunk)[:, None]).astype(BF16)
    perm_t = perm.T
    nbytes = (2 * (2 * tm * ch * 4 + 2 * tm * tm * 2 + 2 * ch * gp * 2 * 2 + 2 * SSM_CHUNK * gp * 4 + ch * 2 * ch * 2)
              + 2 * tm * gp * 4 + 2 * nchunk * gp * 4 + 3 * tm * 2 * gp * 4)
    return pl.pallas_call(
        functools.partial(_ssm_kernel, tm=tm),
        grid=(b, l // tm),
        in_specs=[pl.BlockSpec((1, tm, ch), lambda bb, i: (bb, i, 0)), full(perm), full(perm_t),
                  full(b_op), full(c_op), full(pow_r), full(pow_i), full(d), full(gw_bf), full(bg)],
        out_specs=pl.BlockSpec((1, tm, ch), lambda bb, i: (bb, i, 0)),
        out_shape=jax.ShapeDtypeStruct((b, l, ch), F32),
        scratch_shapes=[pltpu.VMEM((tm, gp), F32), pltpu.VMEM((tm, gp), F32),
                        pltpu.VMEM((nchunk, gp), F32), pltpu.VMEM((nchunk, gp), F32),
                        pltpu.VMEM((1, gp), F32), pltpu.VMEM((1, gp), F32)],
        compiler_params=_params(("parallel", "arbitrary"), nbytes),
        name="ssm_branch",
    )(u, perm, perm_t, b_op, c_op, pow_r, pow_i, d, gw_bf, bg)


def _mix_out_kernel(x_ref, sb_ref, cv_ref, sm_ref, bg_ref, w_ref, o_ref):
    sb = _rms_rows(sb_ref[...], bg_ref[...]).astype(BF16)
    s1 = SB_WIDTH
    s2 = SB_WIDTH + CONV_CH
    y = _dot(sb, w_ref[0:s1, :])
    y = y + _dot(cv_ref[...].astype(BF16), w_ref[s1:s2, :])
    y = y + _dot(sm_ref[...].astype(BF16), w_ref[s2:, :])
    o_ref[...] = x_ref[...] + y


def _mix_out(x2, sb2, cv2, sm2, bg_sb, w_bf, tm):
    t, d = x2.shape
    row = lambda i: (i, 0)
    const = lambda i: (0, 0)
    nbytes = 2 * (2 * tm * d * 4 + tm * (SB_WIDTH + CONV_CH + SSM_CH) * 4 + d * d * 2) + 2 * tm * d * 4
    return pl.pallas_call(
        _mix_out_kernel,
        grid=(t // tm,),
        in_specs=[pl.BlockSpec((tm, d), row), pl.BlockSpec((tm, SB_WIDTH), row),
                  pl.BlockSpec((tm, CONV_CH), row), pl.BlockSpec((tm, SSM_CH), row),
                  pl.BlockSpec((1, SB_WIDTH), const), pl.BlockSpec(w_bf.shape, const)],
        out_specs=pl.BlockSpec((tm, d), row),
        out_shape=jax.ShapeDtypeStruct((t, d), F32),
        compiler_params=_params(("parallel",), nbytes),
        name="mix_out",
    )(x2, sb2, cv2, sm2, bg_sb, w_bf)


def _mem_kv_kernel(m_ref, g_ref, wk_ref, wv_ref, kg_ref, k_ref, v_ref):
    hm = _rms_rows(m_ref[0], g_ref[...]).astype(BF16)
    kk = _dot(hm, wk_ref[...])
    for hh in range(XA_HEADS):
        hs = slice(hh * XA_HEAD_DIM, (hh + 1) * XA_HEAD_DIM)
        k_ref[0, :, hs] = _rms_rows(kk[:, hs], kg_ref[...]).astype(BF16)
    v_ref[0] = _dot(hm, wv_ref[...]).astype(BF16)


def _mem_kv(mem, g, wk_bf, wv_bf, kg):
    b, n, d = mem.shape
    const = lambda bb: (0, 0)
    blk = pl.BlockSpec((1, n, d), lambda bb: (bb, 0, 0))
    nbytes = 2 * (n * d * 4 + 2 * d * d * 2 + 2 * n * d * 2) + 4 * n * d * 4
    return pl.pallas_call(
        _mem_kv_kernel,
        grid=(b,),
        in_specs=[blk, pl.BlockSpec((1, d), const), pl.BlockSpec((d, d), const),
                  pl.BlockSpec((d, d), const), pl.BlockSpec((1, XA_HEAD_DIM), const)],
        out_specs=[blk, blk],
        out_shape=[jax.ShapeDtypeStruct((b, n, d), BF16), jax.ShapeDtypeStruct((b, n, d), BF16)],
        compiler_params=_params(("parallel",), nbytes),
        name="mem_kv",
    )(mem, g, wk_bf, wv_bf, kg)


def _xattn_kernel(x_ref, g_ref, wq_ref, qg_ref, k_ref, v_ref, wo_ref, o_ref, ob_scr):
    x = x_ref[0]
    hx = _rms_rows(x, g_ref[...]).astype(BF16)
    q = _dot(hx, wq_ref[...])
    scale = XA_HEAD_DIM ** -0.5
    for hh in range(XA_HEADS):
        hs = slice(hh * XA_HEAD_DIM, (hh + 1) * XA_HEAD_DIM)
        qh = (_rms_rows(q[:, hs], qg_ref[...]) * scale).astype(BF16)
        s = _dot_nt(qh, k_ref[0, :, hs])
        s = s - jnp.max(s, axis=-1, keepdims=True)
        e = jnp.exp(s)
        p = e / jnp.sum(e, axis=-1, keepdims=True)
        ob_scr[:, hs] = _dot(p.astype(BF16), v_ref[0, :, hs]).astype(BF16)
    o_ref[0] = x + _dot(ob_scr[...], wo_ref[...])


def _xattn(x, g, wq_bf, qg, k_bf, v_bf, wo_bf, tm):
    b, l, d = x.shape
    n = k_bf.shape[1]
    const = lambda bb, i: (0, 0)
    row = pl.BlockSpec((1, tm, d), lambda bb, i: (bb, i, 0))
    kv = pl.BlockSpec((1, n, d), lambda bb, i: (bb, 0, 0))
    nbytes = 2 * (2 * tm * d * 4 + 2 * d * d * 2 + 2 * n * d * 2) + 4 * tm * d * 4
    return pl.pallas_call(
        _xattn_kernel,
        grid=(b, l // tm),
        in_specs=[row, pl.BlockSpec((1, d), const), pl.BlockSpec((d, d), const),
                  pl.BlockSpec((1, XA_HEAD_DIM), const), kv, kv, pl.BlockSpec((d, d), const)],
        out_specs=row,
        out_shape=jax.ShapeDtypeStruct((b, l, d), F32),
        scratch_shapes=[pltpu.VMEM((tm, d), BF16)],
        compiler_params=_params(("parallel", "parallel"), nbytes),
        name="xattn",
    )(x, g, wq_bf, qg, k_bf, v_bf, wo_bf)


def _ffn_kernel(x_ref, g_ref, wg_ref, wu_ref, wo_ref, o_ref, h_scr, acc_scr):
    j = pl.program_id(1)

    @pl.when(j == 0)
    def _():
        h_scr[...] = _rms_rows(x_ref[...], g_ref[...]).astype(BF16)
        acc_scr[...] = x_ref[...]

    h = h_scr[...]
    gate = _dot(h, wg_ref[...])
    up = _dot(h, wu_ref[...])
    act = (gate * _sigmoid(gate) * up).astype(BF16)
    acc_scr[...] += _dot(act, wo_ref[...])

    @pl.when(j == pl.num_programs(1) - 1)
    def _():
        o_ref[...] = acc_scr[...]


def _ffn(x2, g, w_in_bf, w_out_bf, tm, th):
    t, d = x2.shape
    hidden = w_out_bf.shape[0]
    nh = hidden // th
    nbytes = 2 * (2 * tm * d * 4 + 2 * d * th * 2 + th * d * 2) + tm * d * 6 + 4 * tm * th * 4
    return pl.pallas_call(
        _ffn_kernel,
        grid=(t // tm, nh),
        in_specs=[pl.BlockSpec((tm, d), lambda i, j: (i, 0)), pl.BlockSpec((1, d), lambda i, j: (0, 0)),
                  pl.BlockSpec((d, th), lambda i, j: (0, j)),
                  pl.BlockSpec((d, th), lambda i, j: (0, j + nh)),
                  pl.BlockSpec((th, d), lambda i, j: (j, 0))],
        out_specs=pl.BlockSpec((tm, d), lambda i, j: (i, 0)),
        out_shape=jax.ShapeDtypeStruct((t, d), F32),
        scratch_shapes=[pltpu.VMEM((tm, d), BF16), pltpu.VMEM((tm, d), F32)],
        compiler_params=_params(("parallel", "arbitrary"), nbytes),
        name="ffn",
    )(x2, g, w_in_bf, w_in_bf, w_out_bf)


def _tile(n, want):
    want = min(want, n)
    for cand in range(want, 0, -1):
        if n % cand == 0 and (cand % V7X_SUBLANES == 0 or cand == n):
            return cand
    return n


def kernel(x, mem, norm_mix_g, w_in, sb_q_norm_g, sb_k_norm_g, conv_dw_w, conv_dw_b, conv_ln_g, conv_ln_b, conv_pw2_w, ssm_lam_re, ssm_lam_im, ssm_log_dt, ssm_b_re, ssm_b_im, ssm_c_re, ssm_c_im, ssm_d, ssm_glu_w, branch_norm_g, w_out, norm_xa_g, norm_mem_g, xa_wq, xa_wk, xa_wv, xa_q_norm_g, xa_k_norm_g, xa_wo, norm_ffn_g, ffn_w_in, ffn_w_out):
    bsz, seq, d = x.shape
    depth = w_in.shape[0]
    t = bsz * seq
    tm = _tile(t, 512)
    tq = _tile(seq, 256)
    tc = _tile(seq, 512)
    ts = _tile(seq, 512)
    tm_ffn = _tile(t, 1024)
    th = 256
    assert ts % (SSM_CHUNK * V7X_SUBLANES) == 0 and tc % CONV_HALO == 0

    seg = (jnp.arange(SB_WIDTH)[:, None] // SB_HEAD_DIM == jnp.arange(SB_WIDTH)[None, :] // SB_HEAD_DIM).astype(BF16)
    ntri = -(jnp.arange(tq)[:, None] > jnp.arange(tq)[None, :]).astype(BF16)
    row = lambda a: a.reshape(1, -1).astype(F32)

    x2 = x.reshape(t, d)
    for l in range(depth):
        qg = row(jnp.tile(sb_q_norm_g[l], SB_HEADS)) * (SB_HEAD_DIM ** -0.5)
        kg = row(jnp.tile(sb_k_norm_g[l], SB_HEADS))
        q, k, v, c, u = _mix_in(x2, row(norm_mix_g[l]), w_in[l].astype(BF16), qg, kg, seg, tm)
        o_sb = _sb_attn(q.reshape(bsz, seq, SB_WIDTH), k.reshape(bsz, seq, SB_WIDTH),
                        v.reshape(bsz, seq, SB_WIDTH), ntri, tq)
        bg = branch_norm_g[l].astype(F32)
        o_conv = _conv_branch(c.reshape(bsz, seq, 2 * CONV_CH), conv_dw_w[l].astype(F32), row(conv_dw_b[l]),
                              row(conv_ln_g[l]), row(conv_ln_b[l]), conv_pw2_w[l].astype(BF16),
                              row(bg[SB_WIDTH:SB_WIDTH + CONV_CH]), tc)
        ops = _ssm_operators(ssm_lam_re[l], ssm_lam_im[l], ssm_log_dt[l], ssm_b_re[l], ssm_b_im[l],
                             ssm_c_re[l], ssm_c_im[l])
        o_ssm = _ssm_branch(u.reshape(bsz, seq, SSM_CH), ops, row(ssm_d[l]), ssm_glu_w[l].astype(BF16),
                            row(bg[SB_WIDTH + CONV_CH:]), ts)
        x2 = _mix_out(x2, o_sb.reshape(t, SB_WIDTH), o_conv.reshape(t, CONV_CH), o_ssm.reshape(t, SSM_CH),
                      row(bg[:SB_WIDTH]), w_out[l].astype(BF16), tm)
        k_m, v_m = _mem_kv(mem, row(norm_mem_g[l]), xa_wk[l].astype(BF16), xa_wv[l].astype(BF16),
                           row(xa_k_norm_g[l]))
        x2 = _xattn(x2.reshape(bsz, seq, d), row(norm_xa_g[l]), xa_wq[l].astype(BF16), row(xa_q_norm_g[l]),
                    k_m, v_m, xa_wo[l].astype(BF16), tm if seq % tm == 0 else _tile(seq, 512)).reshape(t, d)
        x2 = _ffn(x2, row(norm_ffn_g[l]), ffn_w_in[l].astype(BF16), ffn_w_out[l].astype(BF16), tm_ffn, th)
    return x2.reshape(bsz, seq, d)
```

```python
import functools
import math

import jax
import jax.numpy as jnp
from jax import lax
from jax.experimental import pallas as pl
from jax.experimental.pallas import tpu as pltpu

F32 = jnp.float32
BF16 = jnp.bfloat16
EPS = 1e-6

V7X_LANES = 128
V7X_SUBLANES = 8
V7X_VMEM_BYTES = 64 * 1024 * 1024

SB_HEADS = 8
SB_HEAD_DIM = 64
SB_WIDTH = SB_HEADS * SB_HEAD_DIM
SEG_WIDTH = 256
CONV_CH = 256
CONV_WIDTH = 31
CONV_HALO = 32
SSM_CH = 256
SSM_GROUP = 16
SSM_GROUPS = SSM_CH // SSM_GROUP
SSM_STATE = 64
SSM_CHUNK = 16
XA_HEADS = 4
XA_HEAD_DIM = 256


def _vmem_limit(nbytes):
    return int(min(max(nbytes * 3 // 2, 16 * 1024 * 1024), V7X_VMEM_BYTES - 8 * 1024 * 1024))


def _params(semantics, nbytes):
    return pltpu.CompilerParams(dimension_semantics=semantics, vmem_limit_bytes=_vmem_limit(nbytes))


def _dot(a, b):
    return jnp.dot(a, b, preferred_element_type=F32)


def _dot_nt(a, b):
    return lax.dot_general(a, b, (((1,), (1,)), ((), ())), preferred_element_type=F32)


def _rms_rows(xf, g):
    return xf * lax.rsqrt(jnp.mean(xf * xf, axis=-1, keepdims=True) + EPS) * g


def _sigmoid(x):
    return 1.0 / (1.0 + jnp.exp(-x))


def _segment_mean_sq(p, seg):
    sq = p * p
    hi = sq.astype(BF16)
    lo = (sq - hi.astype(F32)).astype(BF16)
    parts = []
    for c0 in range(0, p.shape[1], SEG_WIDTH):
        cols = slice(c0, c0 + SEG_WIDTH)
        parts.append(_dot(hi[:, cols], seg) + _dot(lo[:, cols], seg))
    return jnp.concatenate(parts, axis=1) * (1.0 / SB_HEAD_DIM)


def _mix_in_kernel(x_ref, g_ref, w_ref, qg_ref, kg_ref, seg_ref,
                   q_ref, k_ref, v_ref, c_ref, u_ref):
    h = _rms_rows(x_ref[...], g_ref[...]).astype(BF16)
    s1, s2, s3 = SB_WIDTH, 2 * SB_WIDTH, 3 * SB_WIDTH
    s4 = s3 + 2 * CONV_CH
    seg = seg_ref[...]
    pq = _dot(h, w_ref[:, 0:s1])
    q_ref[...] = (pq * lax.rsqrt(_segment_mean_sq(pq, seg) + EPS) * qg_ref[...]).astype(BF16)
    pk = _dot(h, w_ref[:, s1:s2])
    k_ref[...] = (pk * lax.rsqrt(_segment_mean_sq(pk, seg) + EPS) * kg_ref[...]).astype(BF16)
    v_ref[...] = _dot(h, w_ref[:, s2:s3]).astype(BF16)
    c_ref[...] = _dot(h, w_ref[:, s3:s4])
    u_ref[...] = _dot(h, w_ref[:, s4:])


def _mix_in(x2, g, w_bf, qg, kg, seg, tm):
    t, d = x2.shape
    n_in = w_bf.shape[1]
    row = lambda i: (i, 0)
    const = lambda i: (0, 0)
    nbytes = 2 * (tm * d * 4 + d * n_in * 2 + tm * (3 * SB_WIDTH * 2 + 2 * CONV_CH * 4 + SSM_CH * 4)) + tm * n_in * 4
    return pl.pallas_call(
        _mix_in_kernel,
        grid=(t // tm,),
        in_specs=[pl.BlockSpec((tm, d), row), pl.BlockSpec((1, d), const),
                  pl.BlockSpec((d, n_in), const), pl.BlockSpec((1, SB_WIDTH), const),
                  pl.BlockSpec((1, SB_WIDTH), const), pl.BlockSpec((SEG_WIDTH, SEG_WIDTH), const)],
        out_specs=[pl.BlockSpec((tm, SB_WIDTH), row), pl.BlockSpec((tm, SB_WIDTH), row),
                   pl.BlockSpec((tm, SB_WIDTH), row), pl.BlockSpec((tm, 2 * CONV_CH), row),
                   pl.BlockSpec((tm, SSM_CH), row)],
        out_shape=[jax.ShapeDtypeStruct((t, SB_WIDTH), BF16), jax.ShapeDtypeStruct((t, SB_WIDTH), BF16),
                   jax.ShapeDtypeStruct((t, SB_WIDTH), BF16), jax.ShapeDtypeStruct((t, 2 * CONV_CH), F32),
                   jax.ShapeDtypeStruct((t, SSM_CH), F32)],
        compiler_params=_params(("parallel",), nbytes),
        name="mix_in",
    )(x2, g, w_bf, qg, kg, seg)


SB_SLOTS = 2
SB_GROUPS_PER_STEP = 2
SB_SOFTPLUS_LINEAR = 40.0
SB_EXP_UNDERFLOW = -104.0
SB_NEVER = -1e30


def _sb_decay(z, run, ntri, mask):
    sp = jnp.maximum(jnp.log(1.0 + jnp.exp(jnp.minimum(z, SB_SOFTPLUS_LINEAR))), z)
    if mask is not None:
        sp = jnp.where(mask, sp, 0.0)
    later = _dot(sp.astype(BF16), ntri)
    return z - sp + run, later, run + later[:, 0:1] - sp[:, 0:1]


def _sb_weights(t0, later, mask):
    w = jnp.exp(t0 + later)
    if mask is not None:
        w = jnp.where(mask, w, 0.0)
    return w.astype(BF16)


def _sb_attn_kernel(q_ref, k_ref, v_ref, ntri_ref, o_ref, q2_scr, z_scr, t0_scr, lat_scr, run_scr, acc_scr, *, tq):
    i = pl.program_id(2)
    nh = V7X_LANES // SB_HEAD_DIM
    m = nh * tq
    groups = [slice(g * V7X_LANES, (g + 1) * V7X_LANES) for g in range(SB_GROUPS_PER_STEP)]

    def rows(j):
        return pl.ds(pl.multiple_of((i - jnp.minimum(j, i)) * tq, tq), tq)

    def scores(j, slot):
        for g, gs in enumerate(groups):
            z_scr[g, slot] = _dot_nt(q2_scr[g], k_ref[0, rows(j), gs])

    def decay(slot, mask=None, bias=None):
        for g in range(len(groups)):
            run = run_scr[g] if bias is None else run_scr[g] + bias
            t0, later, run = _sb_decay(z_scr[g, slot], run, ntri_ref[...], mask)
            t0_scr[g, slot] = t0
            lat_scr[g, slot] = later
            run_scr[g] = run

    def output(j, slot, mask=None):
        for g, gs in enumerate(groups):
            acc_scr[g] += _dot(_sb_weights(t0_scr[g, slot], lat_scr[g, slot], mask), v_ref[0, rows(j), gs])

    def live():
        run = run_scr[0]
        for g in range(1, len(groups)):
            run = jnp.maximum(run, run_scr[g])
        return (jnp.max(run) >= SB_EXP_UNDERFLOW).astype(jnp.int32)

    lane_head = lax.broadcasted_iota(jnp.int32, (tq, V7X_LANES), 1) // SB_HEAD_DIM
    for g, gs in enumerate(groups):
        q = q_ref[0, :, gs]
        for h in range(nh):
            q2_scr[g, h * tq:(h + 1) * tq, :] = jnp.where(lane_head == h, q, jnp.zeros_like(q))
    row = lax.broadcasted_iota(jnp.int32, (m, tq), 0) % tq
    col = lax.broadcasted_iota(jnp.int32, (m, tq), 1)
    diag_mask = col < row
    run_scr[...] = jnp.zeros_like(run_scr)
    acc_scr[...] = jnp.zeros_like(acc_scr)

    scores(0, 0)
    scores(1, 1)
    decay(0, mask=diag_mask)
    decay(1, bias=jnp.where(i >= 1, 0.0, SB_NEVER))
    output(0, 0, diag_mask)
    output(1, 1)

    @pl.when(jnp.logical_and(i >= 2, live() > 0))
    def _():
        scores(2, 0)
        decay(0)
        scores(3, 1)
        more = live()
        output(2, 0)
        scores(4, 0)
        decay(1)
        n_pairs = (i - 2) // 2

        def cond(carry):
            p, more = carry
            return jnp.logical_and(p < n_pairs, more > 0)

        def body(carry):
            p, _ = carry
            t = 2 * p + 3
            scores(t + 2, 1)
            output(t, 1)
            decay(0)
            more = live()
            scores(t + 3, 0)
            output(t + 1, 0)
            decay(1)
            return p + 1, more

        _, more = lax.while_loop(cond, body, (0, more))

        @pl.when(jnp.logical_and(more > 0, (i - 2) % 2 == 1))
        def _():
            output(i, 1)

    for g, gs in enumerate(groups):
        out = acc_scr[g, 0:tq, :]
        for h in range(1, nh):
            out = jnp.where(lane_head == h, acc_scr[g, h * tq:(h + 1) * tq, :], out)
        o_ref[0, :, gs] = out


def _sb_attn(q, k, v, ntri, tq):
    b, l, w = q.shape
    ng = SB_GROUPS_PER_STEP
    gw = ng * V7X_LANES
    m = (V7X_LANES // SB_HEAD_DIM) * tq
    blk = pltpu.VMEM((ng, SB_SLOTS, m, tq), F32)
    nbytes = (2 * (2 * l * gw * 2 + tq * gw * 2 + tq * gw * 4 + tq * tq * 2)
              + ng * (3 * SB_SLOTS * m * tq * 4 + 3 * m * V7X_LANES * 4) + 6 * m * tq * 4)
    return pl.pallas_call(
        functools.partial(_sb_attn_kernel, tq=tq),
        grid=(b, w // gw, l // tq),
        in_specs=[pl.BlockSpec((1, tq, gw), lambda bb, hp, i: (bb, i, hp)),
                  pl.BlockSpec((1, l, gw), lambda bb, hp, i: (bb, 0, hp)),
                  pl.BlockSpec((1, l, gw), lambda bb, hp, i: (bb, 0, hp)),
                  pl.BlockSpec((tq, tq), lambda bb, hp, i: (0, 0))],
        out_specs=pl.BlockSpec((1, tq, gw), lambda bb, hp, i: (bb, i, hp)),
        out_shape=jax.ShapeDtypeStruct((b, l, w), F32),
        scratch_shapes=[pltpu.VMEM((ng, m, V7X_LANES), BF16), blk, blk, blk,
                        pltpu.VMEM((ng, m, 1), F32), pltpu.VMEM((ng, m, V7X_LANES), F32)],
        compiler_params=_params(("parallel", "parallel", "arbitrary"), nbytes),
        name="sb_attn",
    )(q, k, v, ntri)


def _conv_kernel(cur_ref, prev_ref, dww_ref, dwb_ref, lng_ref, lnb_ref, pw_ref, bg_ref, o_ref, hbuf, sh_scr, *, tc):
    i = pl.program_id(1)
    cur = cur_ref[0]
    prev = prev_ref[0]
    hbuf[CONV_HALO:, :] = cur[:, :CONV_CH] * _sigmoid(cur[:, CONV_CH:])
    hprev = prev[:, :CONV_CH] * _sigmoid(prev[:, CONV_CH:])
    hbuf[:CONV_HALO, :] = jnp.where(i > 0, hprev, 0.0)
    span = tc + CONV_HALO - V7X_SUBLANES
    for ph in range(1, V7X_SUBLANES):
        sh_scr[ph - 1] = hbuf[ph:ph + span, :]
    acc = jnp.zeros((tc, CONV_CH), F32) + dwb_ref[...]
    off = CONV_HALO - (CONV_WIDTH - 1)
    for j in range(CONV_WIDTH):
        ph = (off + j) % V7X_SUBLANES
        base = off + j - ph
        tap = hbuf[base:base + tc, :] if ph == 0 else sh_scr[ph - 1, base:base + tc, :]
        acc = acc + dww_ref[j:j + 1, :] * tap
    mu = jnp.mean(acc, axis=-1, keepdims=True)
    cen = acc - mu
    var = jnp.mean(cen * cen, axis=-1, keepdims=True)
    y = cen * lax.rsqrt(var + EPS) * lng_ref[...] + lnb_ref[...]
    y = y * _sigmoid(y)
    o = _dot(y.astype(BF16), pw_ref[...])
    o_ref[0] = _rms_rows(o, bg_ref[...]).astype(BF16)


def _conv_branch(c, dww, dwb, lng, lnb, pw_bf, bg, tc):
    b, l, w = c.shape
    per_tile = tc // CONV_HALO
    const = lambda bb, i: (0, 0)
    nbytes = 2 * (tc * w * 4 + CONV_HALO * w * 4 + tc * CONV_CH * 4) + (tc + CONV_HALO) * CONV_CH * 4 * 14
    return pl.pallas_call(
        functools.partial(_conv_kernel, tc=tc),
        grid=(b, l // tc),
        in_specs=[pl.BlockSpec((1, tc, w), lambda bb, i: (bb, i, 0)),
                  pl.BlockSpec((1, CONV_HALO, w), lambda bb, i: (bb, jnp.maximum(i * per_tile - 1, 0), 0)),
                  pl.BlockSpec((CONV_WIDTH, CONV_CH), const), pl.BlockSpec((1, CONV_CH), const),
                  pl.BlockSpec((1, CONV_CH), const), pl.BlockSpec((1, CONV_CH), const),
                  pl.BlockSpec((CONV_CH, CONV_CH), const), pl.BlockSpec((1, CONV_CH), const)],
        out_specs=pl.BlockSpec((1, tc, CONV_CH), lambda bb, i: (bb, i, 0)),
        out_shape=jax.ShapeDtypeStruct((b, l, CONV_CH), BF16),
        scratch_shapes=[pltpu.VMEM((tc + CONV_HALO, CONV_CH), F32),
                        pltpu.VMEM((V7X_SUBLANES - 1, tc + CONV_HALO - V7X_SUBLANES, CONV_CH), F32)],
        compiler_params=_params(("parallel", "parallel"), nbytes),
        name="conv_branch",
    )(c, c, dww, dwb, lng, lnb, pw_bf, bg)


def _ssm_operators(lam_re, lam_im, log_dt, b_re, b_im, c_re, c_im):
    lr, li = lam_re.astype(F32), lam_im.astype(F32)
    dt = jnp.exp(log_dt.astype(F32))[:, None]
    mag = jnp.exp(lr * dt)
    ar, ai = mag * jnp.cos(li * dt), mag * jnp.sin(li * dt)
    den = lr * lr + li * li
    fr = ((ar - 1.0) * lr + ai * li) / den
    fi = (ai * lr - (ar - 1.0) * li) / den
    br, bi = b_re.astype(F32), b_im.astype(F32)
    bbr = fr[..., None] * br - fi[..., None] * bi
    bbi = fr[..., None] * bi + fi[..., None] * br
    cr, ci = c_re.astype(F32), c_im.astype(F32)
    eye = jnp.eye(SSM_GROUPS, dtype=F32)
    gp = SSM_GROUPS * SSM_STATE

    def rows_gh(w):
        return jnp.einsum('gph,gk->ghkp', w, eye).reshape(SSM_CH, gp)

    def rows_gp(w):
        return jnp.einsum('ghp,gk->gpkh', w, eye).reshape(gp, SSM_CH)

    b_op = jnp.concatenate([rows_gh(bbr), rows_gh(bbi)], axis=1)
    c_op = jnp.concatenate([rows_gp(cr), rows_gp(-ci)], axis=0)
    n = jnp.arange(1, SSM_CHUNK + 1, dtype=F32)[:, None]
    pmag = jnp.exp(n * (lr * dt).reshape(1, gp))
    ang = n * (li * dt).reshape(1, gp)
    return b_op.astype(BF16), c_op.astype(BF16), pmag * jnp.cos(ang), pmag * jnp.sin(ang)


def _ssm_kernel(u_ref, perm_ref, permt_ref, b_ref, c_ref, pr_ref, pi_ref, d_ref, gw_ref, bg_ref, o_ref,
                xr_scr, xi_scr, er_scr, ei_scr, cr_scr, ci_scr, *, tm):
    gp = SSM_GROUPS * SSM_STATE
    nchunk = tm // SSM_CHUNK

    @pl.when(pl.program_id(0) == 0)
    def _():
        cr_scr[...] = jnp.zeros_like(cr_scr)
        ci_scr[...] = jnp.zeros_like(ci_scr)

    def pos(s):
        return pl.ds(s * nchunk, nchunk)

    def inject(b):
        up = _dot(perm_ref[...], u_ref[b].astype(BF16)).astype(BF16)
        bu = _dot(up, b_ref[...])
        xr_scr[b] = bu[:, :gp]
        xi_scr[b] = bu[:, gp:]

    def scan(b):
        ar, ai = pr_ref[0:1, :], pi_ref[0:1, :]
        xr, xi = xr_scr[b, pos(0), :], xi_scr[b, pos(0), :]
        for s in range(1, SSM_CHUNK):
            xr, xi = (ar * xr - ai * xi + xr_scr[b, pos(s), :], ar * xi + ai * xr + xi_scr[b, pos(s), :])
            xr_scr[b, pos(s), :] = xr
            xi_scr[b, pos(s), :] = xi
        nr, ni = pr_ref[SSM_CHUNK - 1:SSM_CHUNK, :], pi_ref[SSM_CHUNK - 1:SSM_CHUNK, :]
        er, ei = cr_scr[b], ci_scr[b]
        for c in range(nchunk):
            er_scr[b, c:c + 1, :] = er
            ei_scr[b, c:c + 1, :] = ei
            er, ei = (nr * er - ni * ei + xr[c:c + 1, :], nr * ei + ni * er + xi[c:c + 1, :])
        cr_scr[b] = er
        ci_scr[b] = ei
        er, ei = er_scr[b], ei_scr[b]
        for s in range(SSM_CHUNK):
            sr, si = pr_ref[s:s + 1, :], pi_ref[s:s + 1, :]
            xr_scr[b, pos(s), :] += sr * er - si * ei
            xi_scr[b, pos(s), :] += sr * ei + si * er

    def readout(b):
        yp = _dot(xr_scr[b].astype(BF16), c_ref[0:gp, :]) + _dot(xi_scr[b].astype(BF16), c_ref[gp:, :])
        hi = yp.astype(BF16)
        lo = (yp - hi.astype(F32)).astype(BF16)
        y = _dot(permt_ref[...], hi) + _dot(permt_ref[...], lo) + d_ref[...] * u_ref[b]
        z = _dot(y.astype(BF16), gw_ref[...])
        o_ref[b] = _rms_rows(z[:, :SSM_CH] * _sigmoid(z[:, SSM_CH:]), bg_ref[...]).astype(BF16)

    nb = u_ref.shape[0]
    for b in range(nb):
        inject(b)
    for b in range(nb):
        scan(b)
        readout(b)


def _ssm_branch(u, ops, d, gw_bf, bg, tm):
    b_op, c_op, pow_r, pow_i = ops
    b, l, ch = u.shape
    gp = SSM_GROUPS * SSM_STATE
    nchunk = tm // SSM_CHUNK
    full = lambda a: pl.BlockSpec(a.shape, lambda i: (0,) * a.ndim)
    r = jnp.arange(tm)
    perm = (r[None, :] == ((r % nchunk) * SSM_CHUNK + r // nchunk)[:, None]).astype(BF16)
    perm_t = perm.T
    nbytes = (2 * (2 * b * tm * ch * 4 + 2 * tm * tm * 2 + 2 * ch * gp * 2 * 2 + 2 * SSM_CHUNK * gp * 4 + ch * 2 * ch * 2)
              + b * (2 * tm * gp * 4 + 2 * nchunk * gp * 4 + 2 * tm * 2 * gp * 4))
    return pl.pallas_call(
        functools.partial(_ssm_kernel, tm=tm),
        grid=(l // tm,),
        in_specs=[pl.BlockSpec((b, tm, ch), lambda i: (0, i, 0)), full(perm), full(perm_t),
                  full(b_op), full(c_op), full(pow_r), full(pow_i), full(d), full(gw_bf), full(bg)],
        out_specs=pl.BlockSpec((b, tm, ch), lambda i: (0, i, 0)),
        out_shape=jax.ShapeDtypeStruct((b, l, ch), BF16),
        scratch_shapes=[pltpu.VMEM((b, tm, gp), F32), pltpu.VMEM((b, tm, gp), F32),
                        pltpu.VMEM((b, nchunk, gp), F32), pltpu.VMEM((b, nchunk, gp), F32),
                        pltpu.VMEM((b, 1, gp), F32), pltpu.VMEM((b, 1, gp), F32)],
        compiler_params=_params(("arbitrary",), nbytes),
        name="ssm_branch",
    )(u, perm, perm_t, b_op, c_op, pow_r, pow_i, d, gw_bf, bg)


def _mix_out(x, sb, cv, sm, bg, w_ref):
    s1 = SB_WIDTH
    s2 = SB_WIDTH + CONV_CH
    y = _dot(_rms_rows(sb, bg).astype(BF16), w_ref[0:s1, :])
    y = y + _dot(cv, w_ref[s1:s2, :])
    y = y + _dot(sm, w_ref[s2:, :])
    return x + y


def _mem_kv_kernel(m_ref, g_ref, wk_ref, wv_ref, kg_ref, k_ref, v_ref):
    hm = _rms_rows(m_ref[0], g_ref[...]).astype(BF16)
    kk = _dot(hm, wk_ref[...])
    for hh in range(XA_HEADS):
        hs = slice(hh * XA_HEAD_DIM, (hh + 1) * XA_HEAD_DIM)
        k_ref[0, :, hs] = _rms_rows(kk[:, hs], kg_ref[...]).astype(BF16)
    v_ref[0] = _dot(hm, wv_ref[...]).astype(BF16)


def _mem_kv(mem, g, wk_bf, wv_bf, kg):
    b, n, d = mem.shape
    const = lambda bb: (0, 0)
    blk = pl.BlockSpec((1, n, d), lambda bb: (bb, 0, 0))
    nbytes = 2 * (n * d * 4 + 2 * d * d * 2 + 2 * n * d * 2) + 4 * n * d * 4
    return pl.pallas_call(
        _mem_kv_kernel,
        grid=(b,),
        in_specs=[blk, pl.BlockSpec((1, d), const), pl.BlockSpec((d, d), const),
                  pl.BlockSpec((d, d), const), pl.BlockSpec((1, XA_HEAD_DIM), const)],
        out_specs=[blk, blk],
        out_shape=[jax.ShapeDtypeStruct((b, n, d), BF16), jax.ShapeDtypeStruct((b, n, d), BF16)],
        compiler_params=_params(("parallel",), nbytes),
        name="mem_kv",
    )(mem, g, wk_bf, wv_bf, kg)


def _xattn_kernel(x_ref, sb_ref, cv_ref, sm_ref, bg_ref, wm_ref, g_ref, wq_ref, qg_ref, k_ref, v_ref, wo_ref,
                  o_ref, ob_scr):
    x = _mix_out(x_ref[0], sb_ref[0], cv_ref[0], sm_ref[0], bg_ref[...], wm_ref)
    hx = _rms_rows(x, g_ref[...]).astype(BF16)
    q = _dot(hx, wq_ref[...])
    scale = XA_HEAD_DIM ** -0.5
    for hh in range(XA_HEADS):
        hs = slice(hh * XA_HEAD_DIM, (hh + 1) * XA_HEAD_DIM)
        qh = (_rms_rows(q[:, hs], qg_ref[...]) * scale).astype(BF16)
        s = _dot_nt(qh, k_ref[0, :, hs])
        s = s - jnp.max(s, axis=-1, keepdims=True)
        e = jnp.exp(s)
        p = e / jnp.sum(e, axis=-1, keepdims=True)
        ob_scr[:, hs] = _dot(p.astype(BF16), v_ref[0, :, hs]).astype(BF16)
    o_ref[0] = x + _dot(ob_scr[...], wo_ref[...])


def _mix_xattn(x, sb, cv, sm, bg_sb, wm_bf, g, wq_bf, qg, k_bf, v_bf, wo_bf, tm):
    b, l, d = x.shape
    n = k_bf.shape[1]
    const = lambda bb, i: (0, 0)
    rows = lambda w: pl.BlockSpec((1, tm, w), lambda bb, i: (bb, i, 0))
    kv = pl.BlockSpec((1, n, d), lambda bb, i: (bb, 0, 0))
    sq = pl.BlockSpec((d, d), const)
    nbytes = (2 * (2 * tm * d * 4 + tm * SB_WIDTH * 4 + tm * (CONV_CH + SSM_CH) * 2 + 3 * d * d * 2 + 2 * n * d * 2)
              + 5 * tm * d * 4)
    return pl.pallas_call(
        _xattn_kernel,
        grid=(b, l // tm),
        in_specs=[rows(d), rows(SB_WIDTH), rows(CONV_CH), rows(SSM_CH), pl.BlockSpec((1, SB_WIDTH), const), sq,
                  pl.BlockSpec((1, d), const), sq, pl.BlockSpec((1, XA_HEAD_DIM), const), kv, kv, sq],
        out_specs=rows(d),
        out_shape=jax.ShapeDtypeStruct((b, l, d), F32),
        scratch_shapes=[pltpu.VMEM((tm, d), BF16)],
        compiler_params=_params(("parallel", "parallel"), nbytes),
        name="mix_xattn",
    )(x, sb, cv, sm, bg_sb, wm_bf, g, wq_bf, qg, k_bf, v_bf, wo_bf)


def _ffn_kernel(x_ref, g_ref, wg_ref, wu_ref, wo_ref, o_ref, h_scr, acc_scr):
    j = pl.program_id(1)

    @pl.when(j == 0)
    def _():
        h_scr[...] = _rms_rows(x_ref[...], g_ref[...]).astype(BF16)
        acc_scr[...] = x_ref[...]

    h = h_scr[...]
    gate = _dot(h, wg_ref[...])
    up = _dot(h, wu_ref[...])
    act = (gate * _sigmoid(gate) * up).astype(BF16)
    acc_scr[...] += _dot(act, wo_ref[...])

    @pl.when(j == pl.num_programs(1) - 1)
    def _():
        o_ref[...] = acc_scr[...]


def _ffn(x2, g, w_in_bf, w_out_bf, tm, th):
    t, d = x2.shape
    hidden = w_out_bf.shape[0]
    nh = hidden // th
    nbytes = 2 * (2 * tm * d * 4 + 2 * d * th * 2 + th * d * 2) + tm * d * 6 + 4 * tm * th * 4
    return pl.pallas_call(
        _ffn_kernel,
        grid=(t // tm, nh),
        in_specs=[pl.BlockSpec((tm, d), lambda i, j: (i, 0)), pl.BlockSpec((1, d), lambda i, j: (0, 0)),
                  pl.BlockSpec((d, th), lambda i, j: (0, j)),
                  pl.BlockSpec((d, th), lambda i, j: (0, j + nh)),
                  pl.BlockSpec((th, d), lambda i, j: (j, 0))],
        out_specs=pl.BlockSpec((tm, d), lambda i, j: (i, 0)),
        out_shape=jax.ShapeDtypeStruct((t, d), F32),
        scratch_shapes=[pltpu.VMEM((tm, d), BF16), pltpu.VMEM((tm, d), F32)],
        compiler_params=_params(("parallel", "arbitrary"), nbytes),
        name="ffn",
    )(x2, g, w_in_bf, w_in_bf, w_out_bf)


def _tile(n, want):
    want = min(want, n)
    for cand in range(want, 0, -1):
        if n % cand == 0 and (cand % V7X_SUBLANES == 0 or cand == n):
            return cand
    return n


def kernel(x, mem, norm_mix_g, w_in, sb_q_norm_g, sb_k_norm_g, conv_dw_w, conv_dw_b, conv_ln_g, conv_ln_b, conv_pw2_w, ssm_lam_re, ssm_lam_im, ssm_log_dt, ssm_b_re, ssm_b_im, ssm_c_re, ssm_c_im, ssm_d, ssm_glu_w, branch_norm_g, w_out, norm_xa_g, norm_mem_g, xa_wq, xa_wk, xa_wv, xa_q_norm_g, xa_k_norm_g, xa_wo, norm_ffn_g, ffn_w_in, ffn_w_out):
    bsz, seq, d = x.shape
    depth = w_in.shape[0]
    t = bsz * seq
    tm = _tile(t, 512)
    tq = _tile(seq, 256)
    tc = _tile(seq, 512)
    ts = _tile(seq, 512)
    tm_ffn = _tile(t, 1024)
    th = 256
    assert ts % (SSM_CHUNK * V7X_SUBLANES) == 0 and tc % CONV_HALO == 0

    seg = (jnp.arange(SEG_WIDTH)[:, None] // SB_HEAD_DIM == jnp.arange(SEG_WIDTH)[None, :] // SB_HEAD_DIM).astype(BF16)
    ntri = -(jnp.arange(tq)[:, None] > jnp.arange(tq)[None, :]).astype(BF16)
    row = lambda a: a.reshape(1, -1).astype(F32)

    x2 = x.reshape(t, d)
    for l in range(depth):
        qg = row(jnp.tile(sb_q_norm_g[l], SB_HEADS)) * (SB_HEAD_DIM ** -0.5)
        kg = row(jnp.tile(sb_k_norm_g[l], SB_HEADS))
        q, k, v, c, u = _mix_in(x2, row(norm_mix_g[l]), w_in[l].astype(BF16), qg, kg, seg, tm)
        o_sb = _sb_attn(q.reshape(bsz, seq, SB_WIDTH), k.reshape(bsz, seq, SB_WIDTH),
                        v.reshape(bsz, seq, SB_WIDTH), ntri, tq)
        bg = branch_norm_g[l].astype(F32)
        o_conv = _conv_branch(c.reshape(bsz, seq, 2 * CONV_CH), conv_dw_w[l].astype(F32), row(conv_dw_b[l]),
                              row(conv_ln_g[l]), row(conv_ln_b[l]), conv_pw2_w[l].astype(BF16),
                              row(bg[SB_WIDTH:SB_WIDTH + CONV_CH]), tc)
        ops = _ssm_operators(ssm_lam_re[l], ssm_lam_im[l], ssm_log_dt[l], ssm_b_re[l], ssm_b_im[l],
                             ssm_c_re[l], ssm_c_im[l])
        o_ssm = _ssm_branch(u.reshape(bsz, seq, SSM_CH), ops, row(ssm_d[l]), ssm_glu_w[l].astype(BF16),
                            row(bg[SB_WIDTH + CONV_CH:]), ts)
        k_m, v_m = _mem_kv(mem, row(norm_mem_g[l]), xa_wk[l].astype(BF16), xa_wv[l].astype(BF16),
                           row(xa_k_norm_g[l]))
        x2 = _mix_xattn(x2.reshape(bsz, seq, d), o_sb, o_conv, o_ssm, row(bg[:SB_WIDTH]), w_out[l].astype(BF16),
                        row(norm_xa_g[l]), xa_wq[l].astype(BF16), row(xa_q_norm_g[l]), k_m, v_m,
                        xa_wo[l].astype(BF16), ts).reshape(t, d)
        x2 = _ffn(x2, row(norm_ffn_g[l]), ffn_w_in[l].astype(BF16), ffn_w_out[l].astype(BF16), tm_ffn, th)
    return x2.reshape(bsz, seq, d)
```

```python
import functools
import math

import jax
import jax.numpy as jnp
from jax import lax
from jax.experimental import pallas as pl
from jax.experimental.pallas import tpu as pltpu

F32 = jnp.float32
BF16 = jnp.bfloat16
EPS = 1e-6

V7X_LANES = 128
V7X_SUBLANES = 8
V7X_VMEM_BYTES = 64 * 1024 * 1024

SB_HEADS = 8
SB_HEAD_DIM = 64
SB_WIDTH = SB_HEADS * SB_HEAD_DIM
SEG_WIDTH = 256
CONV_CH = 256
CONV_WIDTH = 31
CONV_HALO = 32
SSM_CH = 256
SSM_GROUP = 16
SSM_GROUPS = SSM_CH // SSM_GROUP
SSM_STATE = 64
SSM_CHUNK = 16
XA_HEADS = 4
XA_HEAD_DIM = 256


def _vmem_limit(nbytes):
    return int(min(max(nbytes * 3 // 2, 16 * 1024 * 1024), V7X_VMEM_BYTES - 8 * 1024 * 1024))


def _params(semantics, nbytes):
    return pltpu.CompilerParams(dimension_semantics=semantics, vmem_limit_bytes=_vmem_limit(nbytes))


def _dot(a, b):
    return jnp.dot(a, b, preferred_element_type=F32)


def _dot_nt(a, b):
    return lax.dot_general(a, b, (((1,), (1,)), ((), ())), preferred_element_type=F32)


def _rms_rows(xf, g):
    return xf * lax.rsqrt(jnp.mean(xf * xf, axis=-1, keepdims=True) + EPS) * g


def _sigmoid(x):
    return 1.0 / (1.0 + jnp.exp(-x))


def _segment_mean_sq(p, seg):
    sq = p * p
    hi = sq.astype(BF16)
    lo = (sq - hi.astype(F32)).astype(BF16)
    parts = []
    for c0 in range(0, p.shape[1], SEG_WIDTH):
        cols = slice(c0, c0 + SEG_WIDTH)
        parts.append(_dot(hi[:, cols], seg) + _dot(lo[:, cols], seg))
    return jnp.concatenate(parts, axis=1) * (1.0 / SB_HEAD_DIM)


def _mix_in_kernel(x_ref, g_ref, w_ref, qg_ref, kg_ref, seg_ref,
                   q_ref, k_ref, v_ref, c_ref, u_ref):
    h = _rms_rows(x_ref[...], g_ref[...]).astype(BF16)
    s1, s2, s3 = SB_WIDTH, 2 * SB_WIDTH, 3 * SB_WIDTH
    s4 = s3 + 2 * CONV_CH
    seg = seg_ref[...]
    pq = _dot(h, w_ref[:, 0:s1])
    q_ref[...] = (pq * lax.rsqrt(_segment_mean_sq(pq, seg) + EPS) * qg_ref[...]).astype(BF16)
    pk = _dot(h, w_ref[:, s1:s2])
    k_ref[...] = (pk * lax.rsqrt(_segment_mean_sq(pk, seg) + EPS) * kg_ref[...]).astype(BF16)
    v_ref[...] = _dot(h, w_ref[:, s2:s3]).astype(BF16)
    c_ref[...] = _dot(h, w_ref[:, s3:s4])
    u_ref[...] = _dot(h, w_ref[:, s4:])


def _mix_in(x2, g, w_bf, layer, qg, kg, seg, tm):
    t, d = x2.shape
    n_in = w_bf.shape[2]
    row = lambda i: (i, 0)
    const = lambda i: (0, 0)
    nbytes = 2 * (tm * d * 4 + d * n_in * 2 + tm * (3 * SB_WIDTH * 2 + 2 * CONV_CH * 4 + SSM_CH * 4)) + tm * n_in * 4
    return pl.pallas_call(
        _mix_in_kernel,
        grid=(t // tm,),
        in_specs=[pl.BlockSpec((tm, d), row), pl.BlockSpec((1, d), const),
                  pl.BlockSpec((None, d, n_in), lambda i: (layer, 0, 0)), pl.BlockSpec((1, SB_WIDTH), const),
                  pl.BlockSpec((1, SB_WIDTH), const), pl.BlockSpec((SEG_WIDTH, SEG_WIDTH), const)],
        out_specs=[pl.BlockSpec((tm, SB_WIDTH), row), pl.BlockSpec((tm, SB_WIDTH), row),
                   pl.BlockSpec((tm, SB_WIDTH), row), pl.BlockSpec((tm, 2 * CONV_CH), row),
                   pl.BlockSpec((tm, SSM_CH), row)],
        out_shape=[jax.ShapeDtypeStruct((t, SB_WIDTH), BF16), jax.ShapeDtypeStruct((t, SB_WIDTH), BF16),
                   jax.ShapeDtypeStruct((t, SB_WIDTH), BF16), jax.ShapeDtypeStruct((t, 2 * CONV_CH), F32),
                   jax.ShapeDtypeStruct((t, SSM_CH), F32)],
        compiler_params=_params(("parallel",), nbytes),
        name="mix_in",
    )(x2, g, w_bf, qg, kg, seg)


SB_SLOTS = 2
SB_GROUPS_PER_STEP = 2
SB_SOFTPLUS_LINEAR = 40.0
SB_EXP_UNDERFLOW = -104.0
SB_NEVER = -1e30


def _sb_decay(z, run, ntri, mask):
    sp = jnp.maximum(jnp.log(1.0 + jnp.exp(jnp.minimum(z, SB_SOFTPLUS_LINEAR))), z)
    if mask is not None:
        sp = jnp.where(mask, sp, 0.0)
    later = _dot(sp.astype(BF16), ntri)
    return z - sp + run, later, run + later[:, 0:1] - sp[:, 0:1]


def _sb_weights(t0, later, mask):
    w = jnp.exp(t0 + later)
    if mask is not None:
        w = jnp.where(mask, w, 0.0)
    return w.astype(BF16)


def _sb_attn_kernel(q_ref, k_ref, v_ref, ntri_ref, o_ref, q2_scr, z_scr, t0_scr, lat_scr, run_scr, acc_scr, *, tq):
    i = pl.program_id(2)
    nh = V7X_LANES // SB_HEAD_DIM
    m = nh * tq
    groups = [slice(g * V7X_LANES, (g + 1) * V7X_LANES) for g in range(SB_GROUPS_PER_STEP)]

    def rows(j):
        return pl.ds(pl.multiple_of((i - jnp.minimum(j, i)) * tq, tq), tq)

    def scores(j, slot):
        for g, gs in enumerate(groups):
            z_scr[g, slot] = _dot_nt(q2_scr[g], k_ref[0, rows(j), gs])

    def decay(slot, mask=None, bias=None):
        for g in range(len(groups)):
            run = run_scr[g] if bias is None else run_scr[g] + bias
            t0, later, run = _sb_decay(z_scr[g, slot], run, ntri_ref[...], mask)
            t0_scr[g, slot] = t0
            lat_scr[g, slot] = later
            run_scr[g] = run

    def output(j, slot, mask=None):
        for g, gs in enumerate(groups):
            acc_scr[g] += _dot(_sb_weights(t0_scr[g, slot], lat_scr[g, slot], mask), v_ref[0, rows(j), gs])

    def live():
        run = run_scr[0]
        for g in range(1, len(groups)):
            run = jnp.maximum(run, run_scr[g])
        return (jnp.max(run) >= SB_EXP_UNDERFLOW).astype(jnp.int32)

    lane_head = lax.broadcasted_iota(jnp.int32, (tq, V7X_LANES), 1) // SB_HEAD_DIM
    for g, gs in enumerate(groups):
        q = q_ref[0, :, gs]
        for h in range(nh):
            q2_scr[g, h * tq:(h + 1) * tq, :] = jnp.where(lane_head == h, q, jnp.zeros_like(q))
    row = lax.broadcasted_iota(jnp.int32, (m, tq), 0) % tq
    col = lax.broadcasted_iota(jnp.int32, (m, tq), 1)
    diag_mask = col < row
    run_scr[...] = jnp.zeros_like(run_scr)
    acc_scr[...] = jnp.zeros_like(acc_scr)

    scores(0, 0)
    scores(1, 1)
    decay(0, mask=diag_mask)
    decay(1, bias=jnp.where(i >= 1, 0.0, SB_NEVER))
    output(0, 0, diag_mask)
    output(1, 1)

    @pl.when(jnp.logical_and(i >= 2, live() > 0))
    def _():
        scores(2, 0)
        decay(0)
        scores(3, 1)
        more = live()
        output(2, 0)
        scores(4, 0)
        decay(1)
        n_pairs = (i - 2) // 2

        def cond(carry):
            p, more = carry
            return jnp.logical_and(p < n_pairs, more > 0)

        def body(carry):
            p, _ = carry
            t = 2 * p + 3
            scores(t + 2, 1)
            output(t, 1)
            decay(0)
            more = live()
            scores(t + 3, 0)
            output(t + 1, 0)
            decay(1)
            return p + 1, more

        _, more = lax.while_loop(cond, body, (0, more))

        @pl.when(jnp.logical_and(more > 0, (i - 2) % 2 == 1))
        def _():
            output(i, 1)

    for g, gs in enumerate(groups):
        out = acc_scr[g, 0:tq, :]
        for h in range(1, nh):
            out = jnp.where(lane_head == h, acc_scr[g, h * tq:(h + 1) * tq, :], out)
        o_ref[0, :, gs] = out


def _sb_attn(q, k, v, ntri, tq):
    b, l, w = q.shape
    ng = SB_GROUPS_PER_STEP
    gw = ng * V7X_LANES
    m = (V7X_LANES // SB_HEAD_DIM) * tq
    blk = pltpu.VMEM((ng, SB_SLOTS, m, tq), F32)
    nbytes = (2 * (2 * l * gw * 2 + tq * gw * 2 + tq * gw * 4 + tq * tq * 2)
              + ng * (3 * SB_SLOTS * m * tq * 4 + 3 * m * V7X_LANES * 4) + 6 * m * tq * 4)
    return pl.pallas_call(
        functools.partial(_sb_attn_kernel, tq=tq),
        grid=(b, w // gw, l // tq),
        in_specs=[pl.BlockSpec((1, tq, gw), lambda bb, hp, i: (bb, i, hp)),
                  pl.BlockSpec((1, l, gw), lambda bb, hp, i: (bb, 0, hp)),
                  pl.BlockSpec((1, l, gw), lambda bb, hp, i: (bb, 0, hp)),
                  pl.BlockSpec((tq, tq), lambda bb, hp, i: (0, 0))],
        out_specs=pl.BlockSpec((1, tq, gw), lambda bb, hp, i: (bb, i, hp)),
        out_shape=jax.ShapeDtypeStruct((b, l, w), F32),
        scratch_shapes=[pltpu.VMEM((ng, m, V7X_LANES), BF16), blk, blk, blk,
                        pltpu.VMEM((ng, m, 1), F32), pltpu.VMEM((ng, m, V7X_LANES), F32)],
        compiler_params=_params(("parallel", "parallel", "arbitrary"), nbytes),
        name="sb_attn",
    )(q, k, v, ntri)


def _conv_kernel(cur_ref, prev_ref, dww_ref, dwb_ref, lng_ref, lnb_ref, pw_ref, bg_ref, o_ref, hbuf, sh_scr, *, tc):
    i = pl.program_id(1)
    cur = cur_ref[0]
    prev = prev_ref[0]
    hbuf[CONV_HALO:, :] = cur[:, :CONV_CH] * _sigmoid(cur[:, CONV_CH:])
    hprev = prev[:, :CONV_CH] * _sigmoid(prev[:, CONV_CH:])
    hbuf[:CONV_HALO, :] = jnp.where(i > 0, hprev, 0.0)
    span = tc + CONV_HALO - V7X_SUBLANES
    for ph in range(1, V7X_SUBLANES):
        sh_scr[ph - 1] = hbuf[ph:ph + span, :]
    acc = jnp.zeros((tc, CONV_CH), F32) + dwb_ref[...]
    off = CONV_HALO - (CONV_WIDTH - 1)
    for j in range(CONV_WIDTH):
        ph = (off + j) % V7X_SUBLANES
        base = off + j - ph
        tap = hbuf[base:base + tc, :] if ph == 0 else sh_scr[ph - 1, base:base + tc, :]
        acc = acc + dww_ref[j:j + 1, :] * tap
    mu = jnp.mean(acc, axis=-1, keepdims=True)
    cen = acc - mu
    var = jnp.mean(cen * cen, axis=-1, keepdims=True)
    y = cen * lax.rsqrt(var + EPS) * lng_ref[...] + lnb_ref[...]
    y = y * _sigmoid(y)
    o = _dot(y.astype(BF16), pw_ref[...])
    o_ref[0] = _rms_rows(o, bg_ref[...]).astype(BF16)


def _conv_branch(c, dww, dwb, lng, lnb, pw_bf, bg, tc):
    b, l, w = c.shape
    per_tile = tc // CONV_HALO
    const = lambda bb, i: (0, 0)
    nbytes = 2 * (tc * w * 4 + CONV_HALO * w * 4 + tc * CONV_CH * 4) + (tc + CONV_HALO) * CONV_CH * 4 * 14
    return pl.pallas_call(
        functools.partial(_conv_kernel, tc=tc),
        grid=(b, l // tc),
        in_specs=[pl.BlockSpec((1, tc, w), lambda bb, i: (bb, i, 0)),
                  pl.BlockSpec((1, CONV_HALO, w), lambda bb, i: (bb, jnp.maximum(i * per_tile - 1, 0), 0)),
                  pl.BlockSpec((CONV_WIDTH, CONV_CH), const), pl.BlockSpec((1, CONV_CH), const),
                  pl.BlockSpec((1, CONV_CH), const), pl.BlockSpec((1, CONV_CH), const),
                  pl.BlockSpec((CONV_CH, CONV_CH), const), pl.BlockSpec((1, CONV_CH), const)],
        out_specs=pl.BlockSpec((1, tc, CONV_CH), lambda bb, i: (bb, i, 0)),
        out_shape=jax.ShapeDtypeStruct((b, l, CONV_CH), BF16),
        scratch_shapes=[pltpu.VMEM((tc + CONV_HALO, CONV_CH), F32),
                        pltpu.VMEM((V7X_SUBLANES - 1, tc + CONV_HALO - V7X_SUBLANES, CONV_CH), F32)],
        compiler_params=_params(("parallel", "parallel"), nbytes),
        name="conv_branch",
    )(c, c, dww, dwb, lng, lnb, pw_bf, bg)


def _ssm_operators(lam_re, lam_im, log_dt, b_re, b_im, c_re, c_im):
    lr, li = lam_re.astype(F32), lam_im.astype(F32)
    dt = jnp.exp(log_dt.astype(F32))[:, None]
    mag = jnp.exp(lr * dt)
    ar, ai = mag * jnp.cos(li * dt), mag * jnp.sin(li * dt)
    den = lr * lr + li * li
    fr = ((ar - 1.0) * lr + ai * li) / den
    fi = (ai * lr - (ar - 1.0) * li) / den
    br, bi = b_re.astype(F32), b_im.astype(F32)
    bbr = fr[..., None] * br - fi[..., None] * bi
    bbi = fr[..., None] * bi + fi[..., None] * br
    cr, ci = c_re.astype(F32), c_im.astype(F32)
    eye = jnp.eye(SSM_GROUPS, dtype=F32)
    gp = SSM_GROUPS * SSM_STATE

    def rows_gh(w):
        return jnp.einsum('gph,gk->ghkp', w, eye).reshape(SSM_CH, gp)

    def rows_gp(w):
        return jnp.einsum('ghp,gk->gpkh', w, eye).reshape(gp, SSM_CH)

    b_op = jnp.concatenate([rows_gh(bbr), rows_gh(bbi)], axis=1)
    c_op = jnp.concatenate([rows_gp(cr), rows_gp(-ci)], axis=0)
    n = jnp.arange(1, SSM_CHUNK + 1, dtype=F32)[:, None]
    pmag = jnp.exp(n * (lr * dt).reshape(1, gp))
    ang = n * (li * dt).reshape(1, gp)
    return b_op.astype(BF16), c_op.astype(BF16), pmag * jnp.cos(ang), pmag * jnp.sin(ang)


def _ssm_kernel(u_ref, perm_ref, permt_ref, b_ref, c_ref, pr_ref, pi_ref, d_ref, gw_ref, bg_ref, o_ref,
                xr_scr, xi_scr, er_scr, ei_scr, cr_scr, ci_scr, *, tm):
    gp = SSM_GROUPS * SSM_STATE
    nchunk = tm // SSM_CHUNK

    @pl.when(pl.program_id(0) == 0)
    def _():
        cr_scr[...] = jnp.zeros_like(cr_scr)
        ci_scr[...] = jnp.zeros_like(ci_scr)

    def pos(s):
        return pl.ds(s * nchunk, nchunk)

    def inject(b):
        up = _dot(perm_ref[...], u_ref[b].astype(BF16)).astype(BF16)
        bu = _dot(up, b_ref[...])
        xr_scr[b] = bu[:, :gp]
        xi_scr[b] = bu[:, gp:]

    def scan(b):
        ar, ai = pr_ref[0:1, :], pi_ref[0:1, :]
        xr, xi = xr_scr[b, pos(0), :], xi_scr[b, pos(0), :]
        for s in range(1, SSM_CHUNK):
            xr, xi = (ar * xr - ai * xi + xr_scr[b, pos(s), :], ar * xi + ai * xr + xi_scr[b, pos(s), :])
            xr_scr[b, pos(s), :] = xr
            xi_scr[b, pos(s), :] = xi
        nr, ni = pr_ref[SSM_CHUNK - 1:SSM_CHUNK, :], pi_ref[SSM_CHUNK - 1:SSM_CHUNK, :]
        er, ei = cr_scr[b], ci_scr[b]
        for c in range(nchunk):
            er_scr[b, c:c + 1, :] = er
            ei_scr[b, c:c + 1, :] = ei
            er, ei = (nr * er - ni * ei + xr[c:c + 1, :], nr * ei + ni * er + xi[c:c + 1, :])
        cr_scr[b] = er
        ci_scr[b] = ei
        er, ei = er_scr[b], ei_scr[b]
        for s in range(SSM_CHUNK):
            sr, si = pr_ref[s:s + 1, :], pi_ref[s:s + 1, :]
            xr_scr[b, pos(s), :] += sr * er - si * ei
            xi_scr[b, pos(s), :] += sr * ei + si * er

    def readout(b):
        yp = _dot(xr_scr[b].astype(BF16), c_ref[0:gp, :]) + _dot(xi_scr[b].astype(BF16), c_ref[gp:, :])
        hi = yp.astype(BF16)
        lo = (yp - hi.astype(F32)).astype(BF16)
        y = _dot(permt_ref[...], hi) + _dot(permt_ref[...], lo) + d_ref[...] * u_ref[b]
        z = _dot(y.astype(BF16), gw_ref[...])
        o_ref[b] = _rms_rows(z[:, :SSM_CH] * _sigmoid(z[:, SSM_CH:]), bg_ref[...]).astype(BF16)

    nb = u_ref.shape[0]
    for b in range(nb):
        inject(b)
    for b in range(nb):
        scan(b)
        readout(b)


def _ssm_branch(u, ops, d, gw_bf, bg, tm):
    b_op, c_op, pow_r, pow_i = ops
    b, l, ch = u.shape
    gp = SSM_GROUPS * SSM_STATE
    nchunk = tm // SSM_CHUNK
    full = lambda a: pl.BlockSpec(a.shape, lambda i: (0,) * a.ndim)
    r = jnp.arange(tm)
    perm = (r[None, :] == ((r % nchunk) * SSM_CHUNK + r // nchunk)[:, None]).astype(BF16)
    perm_t = perm.T
    nbytes = (2 * (2 * b * tm * ch * 4 + 2 * tm * tm * 2 + 2 * ch * gp * 2 * 2 + 2 * SSM_CHUNK * gp * 4 + ch * 2 * ch * 2)
              + b * (2 * tm * gp * 4 + 2 * nchunk * gp * 4 + 2 * tm * 2 * gp * 4))
    return pl.pallas_call(
        functools.partial(_ssm_kernel, tm=tm),
        grid=(l // tm,),
        in_specs=[pl.BlockSpec((b, tm, ch), lambda i: (0, i, 0)), full(perm), full(perm_t),
                  full(b_op), full(c_op), full(pow_r), full(pow_i), full(d), full(gw_bf), full(bg)],
        out_specs=pl.BlockSpec((b, tm, ch), lambda i: (0, i, 0)),
        out_shape=jax.ShapeDtypeStruct((b, l, ch), BF16),
        scratch_shapes=[pltpu.VMEM((b, tm, gp), F32), pltpu.VMEM((b, tm, gp), F32),
                        pltpu.VMEM((b, nchunk, gp), F32), pltpu.VMEM((b, nchunk, gp), F32),
                        pltpu.VMEM((b, 1, gp), F32), pltpu.VMEM((b, 1, gp), F32)],
        compiler_params=_params(("arbitrary",), nbytes),
        name="ssm_branch",
    )(u, perm, perm_t, b_op, c_op, pow_r, pow_i, d, gw_bf, bg)


def _mix_out(x, sb, cv, sm, bg, w_ref):
    s1 = SB_WIDTH
    s2 = SB_WIDTH + CONV_CH
    y = _dot(_rms_rows(sb, bg).astype(BF16), w_ref[0:s1, :])
    y = y + _dot(cv, w_ref[s1:s2, :])
    y = y + _dot(sm, w_ref[s2:, :])
    return x + y


def _mem_kv_kernel(m_ref, g_ref, wk_ref, wv_ref, kg_ref, k_ref, v_ref):
    hm = _rms_rows(m_ref[0], g_ref[...]).astype(BF16)
    kk = _dot(hm, wk_ref[...])
    for hh in range(XA_HEADS):
        hs = slice(hh * XA_HEAD_DIM, (hh + 1) * XA_HEAD_DIM)
        k_ref[0, :, hs] = _rms_rows(kk[:, hs], kg_ref[...]).astype(BF16)
    v_ref[0] = _dot(hm, wv_ref[...]).astype(BF16)


def _mem_kv(mem, g, wk_bf, wv_bf, layer, kg):
    b, n, d = mem.shape
    const = lambda bb: (0, 0)
    sq = pl.BlockSpec((None, d, d), lambda bb: (layer, 0, 0))
    blk = pl.BlockSpec((1, n, d), lambda bb: (bb, 0, 0))
    nbytes = 2 * (n * d * 4 + 2 * d * d * 2 + 2 * n * d * 2) + 4 * n * d * 4
    return pl.pallas_call(
        _mem_kv_kernel,
        grid=(b,),
        in_specs=[blk, pl.BlockSpec((1, d), const), sq, sq, pl.BlockSpec((1, XA_HEAD_DIM), const)],
        out_specs=[blk, blk],
        out_shape=[jax.ShapeDtypeStruct((b, n, d), BF16), jax.ShapeDtypeStruct((b, n, d), BF16)],
        compiler_params=_params(("parallel",), nbytes),
        name="mem_kv",
    )(mem, g, wk_bf, wv_bf, kg)


def _xattn_kernel(x_ref, sb_ref, cv_ref, sm_ref, bg_ref, wm_ref, g_ref, wq_ref, qg_ref, k_ref, v_ref, wo_ref,
                  o_ref, ob_scr):
    x = _mix_out(x_ref[0], sb_ref[0], cv_ref[0], sm_ref[0], bg_ref[...], wm_ref)
    hx = _rms_rows(x, g_ref[...]).astype(BF16)
    q = _dot(hx, wq_ref[...])
    scale = XA_HEAD_DIM ** -0.5
    for hh in range(XA_HEADS):
        hs = slice(hh * XA_HEAD_DIM, (hh + 1) * XA_HEAD_DIM)
        qh = (_rms_rows(q[:, hs], qg_ref[...]) * scale).astype(BF16)
        s = _dot_nt(qh, k_ref[0, :, hs])
        s = s - jnp.max(s, axis=-1, keepdims=True)
        e = jnp.exp(s)
        p = e / jnp.sum(e, axis=-1, keepdims=True)
        ob_scr[:, hs] = _dot(p.astype(BF16), v_ref[0, :, hs]).astype(BF16)
    o_ref[0] = x + _dot(ob_scr[...], wo_ref[...])


def _mix_xattn(x, sb, cv, sm, bg_sb, wm_bf, g, wq_bf, qg, k_bf, v_bf, wo_bf, layer, tm):
    b, l, d = x.shape
    n = k_bf.shape[1]
    const = lambda bb, i: (0, 0)
    rows = lambda w: pl.BlockSpec((1, tm, w), lambda bb, i: (bb, i, 0))
    kv = pl.BlockSpec((1, n, d), lambda bb, i: (bb, 0, 0))
    sq = pl.BlockSpec((None, d, d), lambda bb, i: (layer, 0, 0))
    nbytes = (2 * (2 * tm * d * 4 + tm * SB_WIDTH * 4 + tm * (CONV_CH + SSM_CH) * 2 + 3 * d * d * 2 + 2 * n * d * 2)
              + 5 * tm * d * 4)
    return pl.pallas_call(
        _xattn_kernel,
        grid=(b, l // tm),
        in_specs=[rows(d), rows(SB_WIDTH), rows(CONV_CH), rows(SSM_CH), pl.BlockSpec((1, SB_WIDTH), const), sq,
                  pl.BlockSpec((1, d), const), sq, pl.BlockSpec((1, XA_HEAD_DIM), const), kv, kv, sq],
        out_specs=rows(d),
        out_shape=jax.ShapeDtypeStruct((b, l, d), F32),
        scratch_shapes=[pltpu.VMEM((tm, d), BF16)],
        compiler_params=_params(("parallel", "parallel"), nbytes),
        name="mix_xattn",
    )(x, sb, cv, sm, bg_sb, wm_bf, g, wq_bf, qg, k_bf, v_bf, wo_bf)


def _ffn_kernel(x_ref, g_ref, wi_ref, wo_ref, o_ref, *, th):
    hidden = wo_ref.shape[0]
    x = x_ref[...]
    h = _rms_rows(x, g_ref[...]).astype(BF16)
    o_ref[...] = x
    for c0 in range(0, hidden, th):
        c1 = min(c0 + th, hidden)
        gate = _dot(h, wi_ref[:, c0:c1])
        up = _dot(h, wi_ref[:, hidden + c0:hidden + c1])
        act = (gate * _sigmoid(gate) * up).astype(BF16)
        o_ref[...] += _dot(act, wo_ref[c0:c1, :])


def _ffn(x2, g, w_in_bf, w_out_bf, layer, tm, th):
    t, d = x2.shape
    hidden = w_out_bf.shape[1]
    once = pl.Buffered(1)
    nbytes = 3 * d * hidden * 2 + 2 * 2 * tm * d * 4 + tm * d * 2 + 6 * tm * th * 4
    return pl.pallas_call(
        functools.partial(_ffn_kernel, th=th),
        grid=(t // tm,),
        in_specs=[pl.BlockSpec((tm, d), lambda i: (i, 0)), pl.BlockSpec((1, d), lambda i: (0, 0)),
                  pl.BlockSpec((None, d, 2 * hidden), lambda i: (layer, 0, 0), pipeline_mode=once),
                  pl.BlockSpec((None, hidden, d), lambda i: (layer, 0, 0), pipeline_mode=once)],
        out_specs=pl.BlockSpec((tm, d), lambda i: (i, 0)),
        out_shape=jax.ShapeDtypeStruct((t, d), F32),
        compiler_params=_params(("parallel",), nbytes),
        name="ffn",
    )(x2, g, w_in_bf, w_out_bf)


def _tile(n, want):
    want = min(want, n)
    for cand in range(want, 0, -1):
        if n % cand == 0 and (cand % V7X_SUBLANES == 0 or cand == n):
            return cand
    return n


def kernel(x, mem, norm_mix_g, w_in, sb_q_norm_g, sb_k_norm_g, conv_dw_w, conv_dw_b, conv_ln_g, conv_ln_b, conv_pw2_w, ssm_lam_re, ssm_lam_im, ssm_log_dt, ssm_b_re, ssm_b_im, ssm_c_re, ssm_c_im, ssm_d, ssm_glu_w, branch_norm_g, w_out, norm_xa_g, norm_mem_g, xa_wq, xa_wk, xa_wv, xa_q_norm_g, xa_k_norm_g, xa_wo, norm_ffn_g, ffn_w_in, ffn_w_out):
    bsz, seq, d = x.shape
    depth = w_in.shape[0]
    t = bsz * seq
    tm = _tile(t, 512)
    tq = _tile(seq, 256)
    tc = _tile(seq, 512)
    ts = _tile(seq, 512)
    th = 512
    assert ts % (SSM_CHUNK * V7X_SUBLANES) == 0 and tc % CONV_HALO == 0
    w_in_bf, w_out_bf = w_in.astype(BF16), w_out.astype(BF16)
    xa_wq_bf, xa_wk_bf, xa_wv_bf, xa_wo_bf = (w.astype(BF16) for w in (xa_wq, xa_wk, xa_wv, xa_wo))
    ffn_w_in_bf, ffn_w_out_bf = ffn_w_in.astype(BF16), ffn_w_out.astype(BF16)

    seg = (jnp.arange(SEG_WIDTH)[:, None] // SB_HEAD_DIM == jnp.arange(SEG_WIDTH)[None, :] // SB_HEAD_DIM).astype(BF16)
    ntri = -(jnp.arange(tq)[:, None] > jnp.arange(tq)[None, :]).astype(BF16)
    row = lambda a: a.reshape(1, -1).astype(F32)

    x2 = x.reshape(t, d)
    for l in range(depth):
        qg = row(jnp.tile(sb_q_norm_g[l], SB_HEADS)) * (SB_HEAD_DIM ** -0.5)
        kg = row(jnp.tile(sb_k_norm_g[l], SB_HEADS))
        q, k, v, c, u = _mix_in(x2, row(norm_mix_g[l]), w_in_bf, l, qg, kg, seg, tm)
        o_sb = _sb_attn(q.reshape(bsz, seq, SB_WIDTH), k.reshape(bsz, seq, SB_WIDTH),
                        v.reshape(bsz, seq, SB_WIDTH), ntri, tq)
        bg = branch_norm_g[l].astype(F32)
        o_conv = _conv_branch(c.reshape(bsz, seq, 2 * CONV_CH), conv_dw_w[l].astype(F32), row(conv_dw_b[l]),
                              row(conv_ln_g[l]), row(conv_ln_b[l]), conv_pw2_w[l].astype(BF16),
                              row(bg[SB_WIDTH:SB_WIDTH + CONV_CH]), tc)
        ops = _ssm_operators(ssm_lam_re[l], ssm_lam_im[l], ssm_log_dt[l], ssm_b_re[l], ssm_b_im[l],
                             ssm_c_re[l], ssm_c_im[l])
        o_ssm = _ssm_branch(u.reshape(bsz, seq, SSM_CH), ops, row(ssm_d[l]), ssm_glu_w[l].astype(BF16),
                            row(bg[SB_WIDTH + CONV_CH:]), ts)
        k_m, v_m = _mem_kv(mem, row(norm_mem_g[l]), xa_wk_bf, xa_wv_bf, l, row(xa_k_norm_g[l]))
        x2 = _mix_xattn(x2.reshape(bsz, seq, d), o_sb, o_conv, o_ssm, row(bg[:SB_WIDTH]), w_out_bf,
                        row(norm_xa_g[l]), xa_wq_bf, row(xa_q_norm_g[l]), k_m, v_m, xa_wo_bf, l, ts).reshape(t, d)
        x2 = _ffn(x2, row(norm_ffn_g[l]), ffn_w_in_bf, ffn_w_out_bf, l, tm, th)
    return x2.reshape(bsz, seq, d)
```

```python
import functools
import math

import jax
import jax.numpy as jnp
from jax import lax
from jax.experimental import pallas as pl
from jax.experimental.pallas import tpu as pltpu

F32 = jnp.float32
BF16 = jnp.bfloat16
EPS = 1e-6

V7X_LANES = 128
V7X_SUBLANES = 8
V7X_VMEM_BYTES = 64 * 1024 * 1024

SB_HEADS = 8
SB_HEAD_DIM = 64
SB_WIDTH = SB_HEADS * SB_HEAD_DIM
SEG_WIDTH = 256
CONV_CH = 256
CONV_WIDTH = 31
CONV_HALO = 32
SSM_CH = 256
SSM_GROUP = 16
SSM_GROUPS = SSM_CH // SSM_GROUP
SSM_STATE = 64
SSM_CHUNK = 16
XA_HEADS = 4
XA_HEAD_DIM = 256


def _vmem_limit(nbytes):
    return int(min(max(nbytes * 3 // 2, 16 * 1024 * 1024), V7X_VMEM_BYTES - 8 * 1024 * 1024))


def _params(semantics, nbytes):
    return pltpu.CompilerParams(dimension_semantics=semantics, vmem_limit_bytes=_vmem_limit(nbytes))


def _dot(a, b):
    return jnp.dot(a, b, preferred_element_type=F32)


def _dot_nt(a, b):
    return lax.dot_general(a, b, (((1,), (1,)), ((), ())), preferred_element_type=F32)


def _rms_rows(xf, g):
    return xf * lax.rsqrt(jnp.mean(xf * xf, axis=-1, keepdims=True) + EPS) * g


def _sigmoid(x):
    return 1.0 / (1.0 + jnp.exp(-x))


def _segment_mean_sq(p, seg):
    sq = p * p
    hi = sq.astype(BF16)
    lo = (sq - hi.astype(F32)).astype(BF16)
    parts = []
    for c0 in range(0, p.shape[1], SEG_WIDTH):
        cols = slice(c0, c0 + SEG_WIDTH)
        parts.append(_dot(hi[:, cols], seg) + _dot(lo[:, cols], seg))
    return jnp.concatenate(parts, axis=1) * (1.0 / SB_HEAD_DIM)


def _mix_in_kernel(x_ref, g_ref, w_ref, qg_ref, kg_ref, seg_ref, dww_ref, dwb_ref, lng_ref, lnb_ref, pw_ref, bgc_ref,
                   q_ref, k_ref, v_ref, cv_ref, u_ref, hbuf, sh_scr, *, tm, tiles_per_seq):
    i = pl.program_id(0)

    @pl.when(i == 0)
    def _():
        hbuf[...] = jnp.zeros_like(hbuf)

    h = _rms_rows(x_ref[...], g_ref[...]).astype(BF16)
    s1, s2, s3 = SB_WIDTH, 2 * SB_WIDTH, 3 * SB_WIDTH
    s4 = s3 + 2 * CONV_CH
    seg = seg_ref[...]
    pc = _dot(h, w_ref[:, s3:s4])
    history = hbuf[tm:tm + CONV_HALO, :]
    hbuf[:CONV_HALO, :] = jnp.where(i % tiles_per_seq == 0, 0.0, history)
    hbuf[CONV_HALO:, :] = pc[:, :CONV_CH] * _sigmoid(pc[:, CONV_CH:])
    quarter = -(-CONV_WIDTH // 4)
    taps = [range(n * quarter, min((n + 1) * quarter, CONV_WIDTH)) for n in range(4)]
    pq = _dot(h, w_ref[:, 0:s1])
    _conv_shift(hbuf, sh_scr, tm)
    acc = _conv_taps(hbuf, sh_scr, dww_ref, jnp.zeros((tm, CONV_CH), F32) + dwb_ref[...], taps[0], tm)
    q_ref[...] = (pq * lax.rsqrt(_segment_mean_sq(pq, seg) + EPS) * qg_ref[...]).astype(BF16)
    pk = _dot(h, w_ref[:, s1:s2])
    acc = _conv_taps(hbuf, sh_scr, dww_ref, acc, taps[1], tm)
    k_ref[...] = (pk * lax.rsqrt(_segment_mean_sq(pk, seg) + EPS) * kg_ref[...]).astype(BF16)
    pv = _dot(h, w_ref[:, s2:s3])
    acc = _conv_taps(hbuf, sh_scr, dww_ref, acc, taps[2], tm)
    v_ref[...] = pv.astype(BF16)
    pu = _dot(h, w_ref[:, s4:])
    acc = _conv_taps(hbuf, sh_scr, dww_ref, acc, taps[3], tm)
    u_ref[...] = pu
    cv_ref[...] = _conv_tail(acc, lng_ref, lnb_ref, pw_ref, bgc_ref)


def _mix_in(x2, seq, g, w_bf, layer, qg, kg, seg, conv, tm):
    t, d = x2.shape
    n_in = w_bf.shape[2]
    row = lambda i: (i, 0)
    const = lambda i: (0, 0)
    vec = pl.BlockSpec((1, CONV_CH), const)
    hist = tm + CONV_HALO
    nbytes = (2 * (tm * d * 4 + d * n_in * 2 + tm * (3 * SB_WIDTH * 2 + CONV_CH * 2 + SSM_CH * 4)) + tm * n_in * 4
              + hist * CONV_CH * 4 * 14)
    return pl.pallas_call(
        functools.partial(_mix_in_kernel, tm=tm, tiles_per_seq=seq // tm),
        grid=(t // tm,),
        in_specs=[pl.BlockSpec((tm, d), row), pl.BlockSpec((1, d), const),
                  pl.BlockSpec((None, d, n_in), lambda i: (layer, 0, 0)), pl.BlockSpec((1, SB_WIDTH), const),
                  pl.BlockSpec((1, SB_WIDTH), const), pl.BlockSpec((SEG_WIDTH, SEG_WIDTH), const),
                  pl.BlockSpec((CONV_WIDTH, CONV_CH), const), vec, vec, vec,
                  pl.BlockSpec((CONV_CH, CONV_CH), const), vec],
        out_specs=[pl.BlockSpec((tm, SB_WIDTH), row), pl.BlockSpec((tm, SB_WIDTH), row),
                   pl.BlockSpec((tm, SB_WIDTH), row), pl.BlockSpec((tm, CONV_CH), row),
                   pl.BlockSpec((tm, SSM_CH), row)],
        out_shape=[jax.ShapeDtypeStruct((t, SB_WIDTH), BF16), jax.ShapeDtypeStruct((t, SB_WIDTH), BF16),
                   jax.ShapeDtypeStruct((t, SB_WIDTH), BF16), jax.ShapeDtypeStruct((t, CONV_CH), BF16),
                   jax.ShapeDtypeStruct((t, SSM_CH), F32)],
        scratch_shapes=[pltpu.VMEM((hist, CONV_CH), F32),
                        pltpu.VMEM((V7X_SUBLANES - 1, hist - V7X_SUBLANES, CONV_CH), F32)],
        compiler_params=_params(("arbitrary",), nbytes),
        name="mix_in",
    )(x2, g, w_bf, qg, kg, seg, *conv)


SB_SLOTS = 2
SB_GROUPS_PER_STEP = 2
SB_SOFTPLUS_LINEAR = 40.0
SB_EXP_UNDERFLOW = -104.0
SB_NEVER = -1e30


def _sb_decay(z, run, ntri, mask):
    sp = jnp.maximum(jnp.log(1.0 + jnp.exp(jnp.minimum(z, SB_SOFTPLUS_LINEAR))), z)
    if mask is not None:
        sp = jnp.where(mask, sp, 0.0)
    later = _dot(sp.astype(BF16), ntri)
    return z - sp + run, later, run + later[:, 0:1] - sp[:, 0:1]


def _sb_weights(t0, later, mask):
    w = jnp.exp(t0 + later)
    if mask is not None:
        w = jnp.where(mask, w, 0.0)
    return w.astype(BF16)


def _sb_attn_kernel(q_ref, k_ref, v_ref, ntri_ref, o_ref, q2_scr, z_scr, t0_scr, lat_scr, run_scr, acc_scr, *, tq):
    i = pl.program_id(2)
    nh = V7X_LANES // SB_HEAD_DIM
    m = nh * tq
    groups = [slice(g * V7X_LANES, (g + 1) * V7X_LANES) for g in range(SB_GROUPS_PER_STEP)]

    def rows(j):
        return pl.ds(pl.multiple_of((i - jnp.minimum(j, i)) * tq, tq), tq)

    def scores(j, slot):
        for g, gs in enumerate(groups):
            z_scr[g, slot] = _dot_nt(q2_scr[g], k_ref[0, rows(j), gs])

    def decay(slot, mask=None, bias=None):
        for g in range(len(groups)):
            run = run_scr[g] if bias is None else run_scr[g] + bias
            t0, later, run = _sb_decay(z_scr[g, slot], run, ntri_ref[...], mask)
            t0_scr[g, slot] = t0
            lat_scr[g, slot] = later
            run_scr[g] = run

    def output(j, slot, mask=None):
        for g, gs in enumerate(groups):
            acc_scr[g] += _dot(_sb_weights(t0_scr[g, slot], lat_scr[g, slot], mask), v_ref[0, rows(j), gs])

    def live():
        run = run_scr[0]
        for g in range(1, len(groups)):
            run = jnp.maximum(run, run_scr[g])
        return (jnp.max(run) >= SB_EXP_UNDERFLOW).astype(jnp.int32)

    lane_head = lax.broadcasted_iota(jnp.int32, (tq, V7X_LANES), 1) // SB_HEAD_DIM
    for g, gs in enumerate(groups):
        q = q_ref[0, :, gs]
        for h in range(nh):
            q2_scr[g, h * tq:(h + 1) * tq, :] = jnp.where(lane_head == h, q, jnp.zeros_like(q))
    row = lax.broadcasted_iota(jnp.int32, (m, tq), 0) % tq
    col = lax.broadcasted_iota(jnp.int32, (m, tq), 1)
    diag_mask = col < row
    run_scr[...] = jnp.zeros_like(run_scr)
    acc_scr[...] = jnp.zeros_like(acc_scr)

    scores(0, 0)
    scores(1, 1)
    decay(0, mask=diag_mask)
    decay(1, bias=jnp.where(i >= 1, 0.0, SB_NEVER))
    output(0, 0, diag_mask)
    output(1, 1)

    @pl.when(jnp.logical_and(i >= 2, live() > 0))
    def _():
        scores(2, 0)
        decay(0)
        scores(3, 1)
        more = live()
        output(2, 0)
        scores(4, 0)
        decay(1)
        n_pairs = (i - 2) // 2

        def cond(carry):
            p, more = carry
            return jnp.logical_and(p < n_pairs, more > 0)

        def body(carry):
            p, _ = carry
            t = 2 * p + 3
            scores(t + 2, 1)
            output(t, 1)
            decay(0)
            more = live()
            scores(t + 3, 0)
            output(t + 1, 0)
            decay(1)
            return p + 1, more

        _, more = lax.while_loop(cond, body, (0, more))

        @pl.when(jnp.logical_and(more > 0, (i - 2) % 2 == 1))
        def _():
            output(i, 1)

    for g, gs in enumerate(groups):
        out = acc_scr[g, 0:tq, :]
        for h in range(1, nh):
            out = jnp.where(lane_head == h, acc_scr[g, h * tq:(h + 1) * tq, :], out)
        o_ref[0, :, gs] = out


def _sb_attn(q, k, v, ntri, tq):
    b, l, w = q.shape
    ng = SB_GROUPS_PER_STEP
    gw = ng * V7X_LANES
    m = (V7X_LANES // SB_HEAD_DIM) * tq
    blk = pltpu.VMEM((ng, SB_SLOTS, m, tq), F32)
    nbytes = (2 * (2 * l * gw * 2 + tq * gw * 2 + tq * gw * 4 + tq * tq * 2)
              + ng * (3 * SB_SLOTS * m * tq * 4 + 3 * m * V7X_LANES * 4) + 6 * m * tq * 4)
    return pl.pallas_call(
        functools.partial(_sb_attn_kernel, tq=tq),
        grid=(b, w // gw, l // tq),
        in_specs=[pl.BlockSpec((1, tq, gw), lambda bb, hp, i: (bb, i, hp)),
                  pl.BlockSpec((1, l, gw), lambda bb, hp, i: (bb, 0, hp)),
                  pl.BlockSpec((1, l, gw), lambda bb, hp, i: (bb, 0, hp)),
                  pl.BlockSpec((tq, tq), lambda bb, hp, i: (0, 0))],
        out_specs=pl.BlockSpec((1, tq, gw), lambda bb, hp, i: (bb, i, hp)),
        out_shape=jax.ShapeDtypeStruct((b, l, w), F32),
        scratch_shapes=[pltpu.VMEM((ng, m, V7X_LANES), BF16), blk, blk, blk,
                        pltpu.VMEM((ng, m, 1), F32), pltpu.VMEM((ng, m, V7X_LANES), F32)],
        compiler_params=_params(("parallel", "parallel", "arbitrary"), nbytes),
        name="sb_attn",
    )(q, k, v, ntri)


def _conv_shift(hbuf, sh_scr, tc):
    span = tc + CONV_HALO - V7X_SUBLANES
    for ph in range(1, V7X_SUBLANES):
        sh_scr[ph - 1] = hbuf[ph:ph + span, :]


def _conv_taps(hbuf, sh_scr, dww_ref, acc, taps, tc):
    off = CONV_HALO - (CONV_WIDTH - 1)
    for j in taps:
        ph = (off + j) % V7X_SUBLANES
        base = off + j - ph
        tap = hbuf[base:base + tc, :] if ph == 0 else sh_scr[ph - 1, base:base + tc, :]
        acc = acc + dww_ref[j:j + 1, :] * tap
    return acc


def _conv_tail(acc, lng_ref, lnb_ref, pw_ref, bg_ref):
    mu = jnp.mean(acc, axis=-1, keepdims=True)
    cen = acc - mu
    var = jnp.mean(cen * cen, axis=-1, keepdims=True)
    y = cen * lax.rsqrt(var + EPS) * lng_ref[...] + lnb_ref[...]
    y = y * _sigmoid(y)
    o = _dot(y.astype(BF16), pw_ref[...])
    return _rms_rows(o, bg_ref[...]).astype(BF16)


def _ssm_operators(lam_re, lam_im, log_dt, b_re, b_im, c_re, c_im):
    lr, li = lam_re.astype(F32), lam_im.astype(F32)
    dt = jnp.exp(log_dt.astype(F32))[:, None]
    mag = jnp.exp(lr * dt)
    ar, ai = mag * jnp.cos(li * dt), mag * jnp.sin(li * dt)
    den = lr * lr + li * li
    fr = ((ar - 1.0) * lr + ai * li) / den
    fi = (ai * lr - (ar - 1.0) * li) / den
    br, bi = b_re.astype(F32), b_im.astype(F32)
    bbr = fr[..., None] * br - fi[..., None] * bi
    bbi = fr[..., None] * bi + fi[..., None] * br
    cr, ci = c_re.astype(F32), c_im.astype(F32)
    eye = jnp.eye(SSM_GROUPS, dtype=F32)
    gp = SSM_GROUPS * SSM_STATE

    def rows_gh(w):
        return jnp.einsum('gph,gk->ghkp', w, eye).reshape(SSM_CH, gp)

    def rows_gp(w):
        return jnp.einsum('ghp,gk->gpkh', w, eye).reshape(gp, SSM_CH)

    b_op = jnp.concatenate([rows_gh(bbr), rows_gh(bbi)], axis=1)
    c_op = jnp.concatenate([rows_gp(cr), rows_gp(-ci)], axis=0)
    n = jnp.arange(1, SSM_CHUNK + 1, dtype=F32)[:, None]
    pmag = jnp.exp(n * (lr * dt).reshape(1, gp))
    ang = n * (li * dt).reshape(1, gp)
    return b_op.astype(BF16), c_op.astype(BF16), pmag * jnp.cos(ang), pmag * jnp.sin(ang)


def _ssm_kernel(u_ref, perm_ref, permt_ref, b_ref, c_ref, pr_ref, pi_ref, d_ref, gw_ref, bg_ref, o_ref,
                xr_scr, xi_scr, er_scr, ei_scr, cr_scr, ci_scr, *, tm):
    gp = SSM_GROUPS * SSM_STATE
    nchunk = tm // SSM_CHUNK

    @pl.when(pl.program_id(0) == 0)
    def _():
        cr_scr[...] = jnp.zeros_like(cr_scr)
        ci_scr[...] = jnp.zeros_like(ci_scr)

    def pos(s):
        return pl.ds(s * nchunk, nchunk)

    def inject(b):
        up = _dot(perm_ref[...], u_ref[b].astype(BF16)).astype(BF16)
        bu = _dot(up, b_ref[...])
        xr_scr[b] = bu[:, :gp]
        xi_scr[b] = bu[:, gp:]

    def scan(b):
        ar, ai = pr_ref[0:1, :], pi_ref[0:1, :]
        xr, xi = xr_scr[b, pos(0), :], xi_scr[b, pos(0), :]
        for s in range(1, SSM_CHUNK):
            xr, xi = (ar * xr - ai * xi + xr_scr[b, pos(s), :], ar * xi + ai * xr + xi_scr[b, pos(s), :])
            xr_scr[b, pos(s), :] = xr
            xi_scr[b, pos(s), :] = xi
        nr, ni = pr_ref[SSM_CHUNK - 1:SSM_CHUNK, :], pi_ref[SSM_CHUNK - 1:SSM_CHUNK, :]
        er, ei = cr_scr[b], ci_scr[b]
        for c in range(nchunk):
            er_scr[b, c:c + 1, :] = er
            ei_scr[b, c:c + 1, :] = ei
            er, ei = (nr * er - ni * ei + xr[c:c + 1, :], nr * ei + ni * er + xi[c:c + 1, :])
        cr_scr[b] = er
        ci_scr[b] = ei
        er, ei = er_scr[b], ei_scr[b]
        for s in range(SSM_CHUNK):
            sr, si = pr_ref[s:s + 1, :], pi_ref[s:s + 1, :]
            xr_scr[b, pos(s), :] += sr * er - si * ei
            xi_scr[b, pos(s), :] += sr * ei + si * er

    def readout(b):
        yp = _dot(xr_scr[b].astype(BF16), c_ref[0:gp, :]) + _dot(xi_scr[b].astype(BF16), c_ref[gp:, :])
        hi = yp.astype(BF16)
        lo = (yp - hi.astype(F32)).astype(BF16)
        y = _dot(permt_ref[...], hi) + _dot(permt_ref[...], lo) + d_ref[...] * u_ref[b]
        z = _dot(y.astype(BF16), gw_ref[...])
        o_ref[b] = _rms_rows(z[:, :SSM_CH] * _sigmoid(z[:, SSM_CH:]), bg_ref[...]).astype(BF16)

    nb = u_ref.shape[0]
    for b in range(nb):
        inject(b)
    for b in range(nb):
        scan(b)
        readout(b)


def _ssm_branch(u, ops, d, gw_bf, bg, tm):
    b_op, c_op, pow_r, pow_i = ops
    b, l, ch = u.shape
    gp = SSM_GROUPS * SSM_STATE
    nchunk = tm // SSM_CHUNK
    full = lambda a: pl.BlockSpec(a.shape, lambda i: (0,) * a.ndim)
    r = jnp.arange(tm)
    perm = (r[None, :] == ((r % nchunk) * SSM_CHUNK + r // nchunk)[:, None]).astype(BF16)
    perm_t = perm.T
    nbytes = (2 * (2 * b * tm * ch * 4 + 2 * tm * tm * 2 + 2 * ch * gp * 2 * 2 + 2 * SSM_CHUNK * gp * 4 + ch * 2 * ch * 2)
              + b * (2 * tm * gp * 4 + 2 * nchunk * gp * 4 + 2 * tm * 2 * gp * 4))
    return pl.pallas_call(
        functools.partial(_ssm_kernel, tm=tm),
        grid=(l // tm,),
        in_specs=[pl.BlockSpec((b, tm, ch), lambda i: (0, i, 0)), full(perm), full(perm_t),
                  full(b_op), full(c_op), full(pow_r), full(pow_i), full(d), full(gw_bf), full(bg)],
        out_specs=pl.BlockSpec((b, tm, ch), lambda i: (0, i, 0)),
        out_shape=jax.ShapeDtypeStruct((b, l, ch), BF16),
        scratch_shapes=[pltpu.VMEM((b, tm, gp), F32), pltpu.VMEM((b, tm, gp), F32),
                        pltpu.VMEM((b, nchunk, gp), F32), pltpu.VMEM((b, nchunk, gp), F32),
                        pltpu.VMEM((b, 1, gp), F32), pltpu.VMEM((b, 1, gp), F32)],
        compiler_params=_params(("arbitrary",), nbytes),
        name="ssm_branch",
    )(u, perm, perm_t, b_op, c_op, pow_r, pow_i, d, gw_bf, bg)


def _mix_out(x, sb, cv, sm, bg, w_ref):
    s1 = SB_WIDTH
    s2 = SB_WIDTH + CONV_CH
    y = _dot(_rms_rows(sb, bg).astype(BF16), w_ref[0:s1, :])
    y = y + _dot(cv, w_ref[s1:s2, :])
    y = y + _dot(sm, w_ref[s2:, :])
    return x + y


def _mem_kv_kernel(m_ref, g_ref, wk_ref, wv_ref, kg_ref, k_ref, v_ref):
    hm = _rms_rows(m_ref[0], g_ref[...]).astype(BF16)
    kk = _dot(hm, wk_ref[...])
    for hh in range(XA_HEADS):
        hs = slice(hh * XA_HEAD_DIM, (hh + 1) * XA_HEAD_DIM)
        k_ref[0, :, hs] = _rms_rows(kk[:, hs], kg_ref[...]).astype(BF16)
    v_ref[0] = _dot(hm, wv_ref[...]).astype(BF16)


def _mem_kv(mem, g, wk_bf, wv_bf, layer, kg):
    b, n, d = mem.shape
    const = lambda bb: (0, 0)
    sq = pl.BlockSpec((None, d, d), lambda bb: (layer, 0, 0))
    blk = pl.BlockSpec((1, n, d), lambda bb: (bb, 0, 0))
    nbytes = 2 * (n * d * 4 + 2 * d * d * 2 + 2 * n * d * 2) + 4 * n * d * 4
    return pl.pallas_call(
        _mem_kv_kernel,
        grid=(b,),
        in_specs=[blk, pl.BlockSpec((1, d), const), sq, sq, pl.BlockSpec((1, XA_HEAD_DIM), const)],
        out_specs=[blk, blk],
        out_shape=[jax.ShapeDtypeStruct((b, n, d), BF16), jax.ShapeDtypeStruct((b, n, d), BF16)],
        compiler_params=_params(("parallel",), nbytes),
        name="mem_kv",
    )(mem, g, wk_bf, wv_bf, kg)


def _xattn_kernel(x_ref, sb_ref, cv_ref, sm_ref, bg_ref, wm_ref, g_ref, wq_ref, qg_ref, k_ref, v_ref, wo_ref,
                  o_ref, ob_scr):
    x = _mix_out(x_ref[0], sb_ref[0], cv_ref[0], sm_ref[0], bg_ref[...], wm_ref)
    hx = _rms_rows(x, g_ref[...]).astype(BF16)
    q = _dot(hx, wq_ref[...])
    scale = XA_HEAD_DIM ** -0.5
    for hh in range(XA_HEADS):
        hs = slice(hh * XA_HEAD_DIM, (hh + 1) * XA_HEAD_DIM)
        qh = (_rms_rows(q[:, hs], qg_ref[...]) * scale).astype(BF16)
        s = _dot_nt(qh, k_ref[0, :, hs])
        s = s - jnp.max(s, axis=-1, keepdims=True)
        e = jnp.exp(s)
        p = e / jnp.sum(e, axis=-1, keepdims=True)
        ob_scr[:, hs] = _dot(p.astype(BF16), v_ref[0, :, hs]).astype(BF16)
    o_ref[0] = x + _dot(ob_scr[...], wo_ref[...])


def _mix_xattn(x, sb, cv, sm, bg_sb, wm_bf, g, wq_bf, qg, k_bf, v_bf, wo_bf, layer, tm):
    b, l, d = x.shape
    n = k_bf.shape[1]
    const = lambda bb, i: (0, 0)
    rows = lambda w: pl.BlockSpec((1, tm, w), lambda bb, i: (bb, i, 0))
    kv = pl.BlockSpec((1, n, d), lambda bb, i: (bb, 0, 0))
    sq = pl.BlockSpec((None, d, d), lambda bb, i: (layer, 0, 0))
    nbytes = (2 * (2 * tm * d * 4 + tm * SB_WIDTH * 4 + tm * (CONV_CH + SSM_CH) * 2 + 3 * d * d * 2 + 2 * n * d * 2)
              + 5 * tm * d * 4)
    return pl.pallas_call(
        _xattn_kernel,
        grid=(b, l // tm),
        in_specs=[rows(d), rows(SB_WIDTH), rows(CONV_CH), rows(SSM_CH), pl.BlockSpec((1, SB_WIDTH), const), sq,
                  pl.BlockSpec((1, d), const), sq, pl.BlockSpec((1, XA_HEAD_DIM), const), kv, kv, sq],
        out_specs=rows(d),
        out_shape=jax.ShapeDtypeStruct((b, l, d), F32),
        scratch_shapes=[pltpu.VMEM((tm, d), BF16)],
        compiler_params=_params(("parallel", "parallel"), nbytes),
        name="mix_xattn",
    )(x, sb, cv, sm, bg_sb, wm_bf, g, wq_bf, qg, k_bf, v_bf, wo_bf)


def _ffn_kernel(x_ref, g_ref, wi_ref, wo_ref, o_ref, *, th):
    hidden = wo_ref.shape[0]
    x = x_ref[...]
    h = _rms_rows(x, g_ref[...]).astype(BF16)
    o_ref[...] = x
    for c0 in range(0, hidden, th):
        c1 = min(c0 + th, hidden)
        gate = _dot(h, wi_ref[:, c0:c1])
        up = _dot(h, wi_ref[:, hidden + c0:hidden + c1])
        act = (gate * _sigmoid(gate) * up).astype(BF16)
        o_ref[...] += _dot(act, wo_ref[c0:c1, :])


def _ffn(x2, g, w_in_bf, w_out_bf, layer, tm, th):
    t, d = x2.shape
    hidden = w_out_bf.shape[1]
    once = pl.Buffered(1)
    nbytes = 3 * d * hidden * 2 + 2 * 2 * tm * d * 4 + tm * d * 2 + 6 * tm * th * 4
    return pl.pallas_call(
        functools.partial(_ffn_kernel, th=th),
        grid=(t // tm,),
        in_specs=[pl.BlockSpec((tm, d), lambda i: (i, 0)), pl.BlockSpec((1, d), lambda i: (0, 0)),
                  pl.BlockSpec((None, d, 2 * hidden), lambda i: (layer, 0, 0), pipeline_mode=once),
                  pl.BlockSpec((None, hidden, d), lambda i: (layer, 0, 0), pipeline_mode=once)],
        out_specs=pl.BlockSpec((tm, d), lambda i: (i, 0)),
        out_shape=jax.ShapeDtypeStruct((t, d), F32),
        compiler_params=_params(("parallel",), nbytes),
        name="ffn",
    )(x2, g, w_in_bf, w_out_bf)


def _tile(n, want):
    want = min(want, n)
    for cand in range(want, 0, -1):
        if n % cand == 0 and (cand % V7X_SUBLANES == 0 or cand == n):
            return cand
    return n


def kernel(x, mem, norm_mix_g, w_in, sb_q_norm_g, sb_k_norm_g, conv_dw_w, conv_dw_b, conv_ln_g, conv_ln_b, conv_pw2_w, ssm_lam_re, ssm_lam_im, ssm_log_dt, ssm_b_re, ssm_b_im, ssm_c_re, ssm_c_im, ssm_d, ssm_glu_w, branch_norm_g, w_out, norm_xa_g, norm_mem_g, xa_wq, xa_wk, xa_wv, xa_q_norm_g, xa_k_norm_g, xa_wo, norm_ffn_g, ffn_w_in, ffn_w_out):
    bsz, seq, d = x.shape
    depth = w_in.shape[0]
    t = bsz * seq
    tm = _tile(seq, 512)
    tq = _tile(seq, 256)
    ts = _tile(seq, 512)
    th = 512
    assert ts % (SSM_CHUNK * V7X_SUBLANES) == 0 and tm >= CONV_HALO
    w_in_bf, w_out_bf = w_in.astype(BF16), w_out.astype(BF16)
    xa_wq_bf, xa_wk_bf, xa_wv_bf, xa_wo_bf = (w.astype(BF16) for w in (xa_wq, xa_wk, xa_wv, xa_wo))
    ffn_w_in_bf, ffn_w_out_bf = ffn_w_in.astype(BF16), ffn_w_out.astype(BF16)

    seg = (jnp.arange(SEG_WIDTH)[:, None] // SB_HEAD_DIM == jnp.arange(SEG_WIDTH)[None, :] // SB_HEAD_DIM).astype(BF16)
    ntri = -(jnp.arange(tq)[:, None] > jnp.arange(tq)[None, :]).astype(BF16)
    row = lambda a: a.reshape(1, -1).astype(F32)

    x2 = x.reshape(t, d)
    for l in range(depth):
        qg = row(jnp.tile(sb_q_norm_g[l], SB_HEADS)) * (SB_HEAD_DIM ** -0.5)
        kg = row(jnp.tile(sb_k_norm_g[l], SB_HEADS))
        bg = branch_norm_g[l].astype(F32)
        conv = (conv_dw_w[l].astype(F32), row(conv_dw_b[l]), row(conv_ln_g[l]), row(conv_ln_b[l]),
                conv_pw2_w[l].astype(BF16), row(bg[SB_WIDTH:SB_WIDTH + CONV_CH]))
        q, k, v, o_conv, u = _mix_in(x2, seq, row(norm_mix_g[l]), w_in_bf, l, qg, kg, seg, conv, tm)
        o_conv = o_conv.reshape(bsz, seq, CONV_CH)
        o_sb = _sb_attn(q.reshape(bsz, seq, SB_WIDTH), k.reshape(bsz, seq, SB_WIDTH),
                        v.reshape(bsz, seq, SB_WIDTH), ntri, tq)
        ops = _ssm_operators(ssm_lam_re[l], ssm_lam_im[l], ssm_log_dt[l], ssm_b_re[l], ssm_b_im[l],
                             ssm_c_re[l], ssm_c_im[l])
        o_ssm = _ssm_branch(u.reshape(bsz, seq, SSM_CH), ops, row(ssm_d[l]), ssm_glu_w[l].astype(BF16),
                            row(bg[SB_WIDTH + CONV_CH:]), ts)
        k_m, v_m = _mem_kv(mem, row(norm_mem_g[l]), xa_wk_bf, xa_wv_bf, l, row(xa_k_norm_g[l]))
        x2 = _mix_xattn(x2.reshape(bsz, seq, d), o_sb, o_conv, o_ssm, row(bg[:SB_WIDTH]), w_out_bf,
                        row(norm_xa_g[l]), xa_wq_bf, row(xa_q_norm_g[l]), k_m, v_m, xa_wo_bf, l, ts).reshape(t, d)
        x2 = _ffn(x2, row(norm_ffn_g[l]), ffn_w_in_bf, ffn_w_out_bf, l, tm, th)
    return x2.reshape(bsz, seq, d)
```

```python
import functools
import math

import jax
import jax.numpy as jnp
from jax import lax
from jax.experimental import pallas as pl
from jax.experimental.pallas import tpu as pltpu

F32 = jnp.float32
BF16 = jnp.bfloat16
EPS = 1e-6

V7X_LANES = 128
V7X_SUBLANES = 8
V7X_VMEM_BYTES = 64 * 1024 * 1024

SB_HEADS = 8
SB_HEAD_DIM = 64
SB_WIDTH = SB_HEADS * SB_HEAD_DIM
SEG_WIDTH = 256
CONV_CH = 256
CONV_WIDTH = 31
CONV_HALO = 32
SSM_CH = 256
SSM_GROUP = 16
SSM_GROUPS = SSM_CH // SSM_GROUP
SSM_STATE = 64
SSM_CHUNK = 16
XA_HEADS = 4
XA_HEAD_DIM = 256


def _vmem_limit(nbytes):
    return int(min(max(nbytes * 3 // 2, 16 * 1024 * 1024), V7X_VMEM_BYTES - 8 * 1024 * 1024))


def _params(semantics, nbytes):
    return pltpu.CompilerParams(dimension_semantics=semantics, vmem_limit_bytes=_vmem_limit(nbytes))


def _dot(a, b):
    return jnp.dot(a, b, preferred_element_type=F32)


def _dot_nt(a, b):
    return lax.dot_general(a, b, (((1,), (1,)), ((), ())), preferred_element_type=F32)


def _rms_rows(xf, g):
    return xf * lax.rsqrt(jnp.mean(xf * xf, axis=-1, keepdims=True) + EPS) * g


def _sigmoid(x):
    return 1.0 / (1.0 + jnp.exp(-x))


def _segment_mean_sq(p, seg):
    sq = p * p
    hi = sq.astype(BF16)
    lo = (sq - hi.astype(F32)).astype(BF16)
    parts = []
    for c0 in range(0, p.shape[1], SEG_WIDTH):
        cols = slice(c0, c0 + SEG_WIDTH)
        parts.append(_dot(hi[:, cols], seg) + _dot(lo[:, cols], seg))
    return jnp.concatenate(parts, axis=1) * (1.0 / SB_HEAD_DIM)


def _mix_in_kernel(x_ref, g_ref, w_ref, qg_ref, kg_ref, seg_ref, dww_ref, dwb_ref, lng_ref, lnb_ref, pw_ref, bgn_ref,
                   q_ref, k_ref, v_ref, cv_ref, u_ref, hbuf, sh_scr, *, tm, tiles_per_seq):
    i = pl.program_id(0)

    @pl.when(i == 0)
    def _():
        hbuf[...] = jnp.zeros_like(hbuf)

    h = _rms_rows(x_ref[...], g_ref[...]).astype(BF16)
    s1, s2, s3 = SB_WIDTH, 2 * SB_WIDTH, 3 * SB_WIDTH
    s4 = s3 + 2 * CONV_CH
    seg = seg_ref[...]
    pc = _dot(h, w_ref[:, s3:s4])
    history = hbuf[tm:tm + CONV_HALO, :]
    hbuf[:CONV_HALO, :] = jnp.where(i % tiles_per_seq == 0, 0.0, history)
    hbuf[CONV_HALO:, :] = pc[:, :CONV_CH] * _sigmoid(pc[:, CONV_CH:])
    quarter = -(-CONV_WIDTH // 4)
    taps = [range(n * quarter, min((n + 1) * quarter, CONV_WIDTH)) for n in range(4)]
    pq = _dot(h, w_ref[:, 0:s1])
    _conv_shift(hbuf, sh_scr, tm)
    acc = _conv_taps(hbuf, sh_scr, dww_ref, jnp.zeros((tm, CONV_CH), F32) + dwb_ref[...], taps[0], tm)
    q_ref[...] = (pq * lax.rsqrt(_segment_mean_sq(pq, seg) + EPS) * qg_ref[...]).astype(BF16)
    pk = _dot(h, w_ref[:, s1:s2])
    acc = _conv_taps(hbuf, sh_scr, dww_ref, acc, taps[1], tm)
    k_ref[...] = (pk * lax.rsqrt(_segment_mean_sq(pk, seg) + EPS) * kg_ref[...]).astype(BF16)
    pv = _dot(h, w_ref[:, s2:s3])
    acc = _conv_taps(hbuf, sh_scr, dww_ref, acc, taps[2], tm)
    v_ref[...] = pv.astype(BF16)
    pu = _dot(h, w_ref[:, s4:])
    acc = _conv_taps(hbuf, sh_scr, dww_ref, acc, taps[3], tm)
    u_ref[...] = pu
    cv_ref[...] = _conv_tail(acc, lng_ref, lnb_ref, pw_ref, bgn_ref[:, SB_WIDTH:SB_WIDTH + CONV_CH])


def _layer_block(a, layer):
    tail = a.shape[1:]
    return pl.BlockSpec((None,) + tail, lambda *_: (layer,) + (0,) * len(tail))


def _mix_in(x2, seq, layer, g, w_bf, qg, kg, seg, conv, bgn, tm):
    t, d = x2.shape
    n_in = w_bf.shape[2]
    row = lambda i: (i, 0)
    hist = tm + CONV_HALO
    nbytes = (2 * (tm * d * 4 + d * n_in * 2 + tm * (3 * SB_WIDTH * 2 + CONV_CH * 2 + SSM_CH * 4)) + tm * n_in * 4
              + hist * CONV_CH * 4 * 14)
    return pl.pallas_call(
        functools.partial(_mix_in_kernel, tm=tm, tiles_per_seq=seq // tm),
        grid=(t // tm,),
        in_specs=[pl.BlockSpec((tm, d), row)]
        + [_layer_block(a, layer) for a in (g, w_bf, qg, kg)]
        + [pl.BlockSpec((SEG_WIDTH, SEG_WIDTH), lambda i: (0, 0))]
        + [_layer_block(a, layer) for a in (*conv, bgn)],
        out_specs=[pl.BlockSpec((tm, SB_WIDTH), row), pl.BlockSpec((tm, SB_WIDTH), row),
                   pl.BlockSpec((tm, SB_WIDTH), row), pl.BlockSpec((tm, CONV_CH), row),
                   pl.BlockSpec((tm, SSM_CH), row)],
        out_shape=[jax.ShapeDtypeStruct((t, SB_WIDTH), BF16), jax.ShapeDtypeStruct((t, SB_WIDTH), BF16),
                   jax.ShapeDtypeStruct((t, SB_WIDTH), BF16), jax.ShapeDtypeStruct((t, CONV_CH), BF16),
                   jax.ShapeDtypeStruct((t, SSM_CH), F32)],
        scratch_shapes=[pltpu.VMEM((hist, CONV_CH), F32),
                        pltpu.VMEM((V7X_SUBLANES - 1, hist - V7X_SUBLANES, CONV_CH), F32)],
        compiler_params=_params(("arbitrary",), nbytes),
        name="mix_in",
    )(x2, g, w_bf, qg, kg, seg, *conv, bgn)


SB_SLOTS = 2
SB_GROUPS_PER_STEP = 2
SB_SOFTPLUS_LINEAR = 40.0
SB_EXP_UNDERFLOW = -104.0
SB_NEVER = -1e30


def _sb_decay(z, run, ntri, mask):
    sp = jnp.maximum(jnp.log(1.0 + jnp.exp(jnp.minimum(z, SB_SOFTPLUS_LINEAR))), z)
    if mask is not None:
        sp = jnp.where(mask, sp, 0.0)
    later = _dot(sp.astype(BF16), ntri)
    return z - sp + run, later, run + later[:, 0:1] - sp[:, 0:1]


def _sb_weights(t0, later, mask):
    w = jnp.exp(t0 + later)
    if mask is not None:
        w = jnp.where(mask, w, 0.0)
    return w.astype(BF16)


def _sb_attn_kernel(q_ref, k_ref, v_ref, ntri_ref, o_ref, q2_scr, z_scr, t0_scr, lat_scr, run_scr, acc_scr, *, tq):
    i = pl.program_id(2)
    nh = V7X_LANES // SB_HEAD_DIM
    m = nh * tq
    groups = [slice(g * V7X_LANES, (g + 1) * V7X_LANES) for g in range(SB_GROUPS_PER_STEP)]

    def rows(j):
        return pl.ds(pl.multiple_of((i - jnp.minimum(j, i)) * tq, tq), tq)

    def scores(j, slot):
        for g, gs in enumerate(groups):
            z_scr[g, slot] = _dot_nt(q2_scr[g], k_ref[0, rows(j), gs])

    def decay(slot, mask=None, bias=None):
        for g in range(len(groups)):
            run = run_scr[g] if bias is None else run_scr[g] + bias
            t0, later, run = _sb_decay(z_scr[g, slot], run, ntri_ref[...], mask)
            t0_scr[g, slot] = t0
            lat_scr[g, slot] = later
            run_scr[g] = run

    def output(j, slot, mask=None):
        for g, gs in enumerate(groups):
            acc_scr[g] += _dot(_sb_weights(t0_scr[g, slot], lat_scr[g, slot], mask), v_ref[0, rows(j), gs])

    def live():
        run = run_scr[0]
        for g in range(1, len(groups)):
            run = jnp.maximum(run, run_scr[g])
        return (jnp.max(run) >= SB_EXP_UNDERFLOW).astype(jnp.int32)

    lane_head = lax.broadcasted_iota(jnp.int32, (tq, V7X_LANES), 1) // SB_HEAD_DIM
    for g, gs in enumerate(groups):
        q = q_ref[0, :, gs]
        for h in range(nh):
            q2_scr[g, h * tq:(h + 1) * tq, :] = jnp.where(lane_head == h, q, jnp.zeros_like(q))
    row = lax.broadcasted_iota(jnp.int32, (m, tq), 0) % tq
    col = lax.broadcasted_iota(jnp.int32, (m, tq), 1)
    diag_mask = col < row
    run_scr[...] = jnp.zeros_like(run_scr)
    acc_scr[...] = jnp.zeros_like(acc_scr)

    scores(0, 0)
    scores(1, 1)
    decay(0, mask=diag_mask)
    decay(1, bias=jnp.where(i >= 1, 0.0, SB_NEVER))
    output(0, 0, diag_mask)
    output(1, 1)

    @pl.when(jnp.logical_and(i >= 2, live() > 0))
    def _():
        scores(2, 0)
        decay(0)
        scores(3, 1)
        more = live()
        output(2, 0)
        scores(4, 0)
        decay(1)
        n_pairs = (i - 2) // 2

        def cond(carry):
            p, more = carry
            return jnp.logical_and(p < n_pairs, more > 0)

        def body(carry):
            p, _ = carry
            t = 2 * p + 3
            scores(t + 2, 1)
            output(t, 1)
            decay(0)
            more = live()
            scores(t + 3, 0)
            output(t + 1, 0)
            decay(1)
            return p + 1, more

        _, more = lax.while_loop(cond, body, (0, more))

        @pl.when(jnp.logical_and(more > 0, (i - 2) % 2 == 1))
        def _():
            output(i, 1)

    for g, gs in enumerate(groups):
        out = acc_scr[g, 0:tq, :]
        for h in range(1, nh):
            out = jnp.where(lane_head == h, acc_scr[g, h * tq:(h + 1) * tq, :], out)
        o_ref[0, :, gs] = out


def _sb_attn(q, k, v, ntri, tq):
    b, l, w = q.shape
    ng = SB_GROUPS_PER_STEP
    gw = ng * V7X_LANES
    m = (V7X_LANES // SB_HEAD_DIM) * tq
    blk = pltpu.VMEM((ng, SB_SLOTS, m, tq), F32)
    nbytes = (2 * (2 * l * gw * 2 + tq * gw * 2 + tq * gw * 4 + tq * tq * 2)
              + ng * (3 * SB_SLOTS * m * tq * 4 + 3 * m * V7X_LANES * 4) + 6 * m * tq * 4)
    return pl.pallas_call(
        functools.partial(_sb_attn_kernel, tq=tq),
        grid=(b, w // gw, l // tq),
        in_specs=[pl.BlockSpec((1, tq, gw), lambda bb, hp, i: (bb, i, hp)),
                  pl.BlockSpec((1, l, gw), lambda bb, hp, i: (bb, 0, hp)),
                  pl.BlockSpec((1, l, gw), lambda bb, hp, i: (bb, 0, hp)),
                  pl.BlockSpec((tq, tq), lambda bb, hp, i: (0, 0))],
        out_specs=pl.BlockSpec((1, tq, gw), lambda bb, hp, i: (bb, i, hp)),
        out_shape=jax.ShapeDtypeStruct((b, l, w), F32),
        scratch_shapes=[pltpu.VMEM((ng, m, V7X_LANES), BF16), blk, blk, blk,
                        pltpu.VMEM((ng, m, 1), F32), pltpu.VMEM((ng, m, V7X_LANES), F32)],
        compiler_params=_params(("parallel", "parallel", "arbitrary"), nbytes),
        name="sb_attn",
    )(q, k, v, ntri)


def _conv_shift(hbuf, sh_scr, tc):
    span = tc + CONV_HALO - V7X_SUBLANES
    for ph in range(1, V7X_SUBLANES):
        sh_scr[ph - 1] = hbuf[ph:ph + span, :]


def _conv_taps(hbuf, sh_scr, dww_ref, acc, taps, tc):
    off = CONV_HALO - (CONV_WIDTH - 1)
    for j in taps:
        ph = (off + j) % V7X_SUBLANES
        base = off + j - ph
        tap = hbuf[base:base + tc, :] if ph == 0 else sh_scr[ph - 1, base:base + tc, :]
        acc = acc + dww_ref[j:j + 1, :] * tap
    return acc


def _conv_tail(acc, lng_ref, lnb_ref, pw_ref, bg):
    mu = jnp.mean(acc, axis=-1, keepdims=True)
    cen = acc - mu
    var = jnp.mean(cen * cen, axis=-1, keepdims=True)
    y = cen * lax.rsqrt(var + EPS) * lng_ref[...] + lnb_ref[...]
    y = y * _sigmoid(y)
    o = _dot(y.astype(BF16), pw_ref[...])
    return _rms_rows(o, bg).astype(BF16)


def _ssm_operators(lam_re, lam_im, log_dt, b_re, b_im, c_re, c_im):
    lr, li = lam_re.astype(F32), lam_im.astype(F32)
    dt = jnp.exp(log_dt.astype(F32))[:, None]
    mag = jnp.exp(lr * dt)
    ar, ai = mag * jnp.cos(li * dt), mag * jnp.sin(li * dt)
    den = lr * lr + li * li
    fr = ((ar - 1.0) * lr + ai * li) / den
    fi = (ai * lr - (ar - 1.0) * li) / den
    br, bi = b_re.astype(F32), b_im.astype(F32)
    bbr = fr[..., None] * br - fi[..., None] * bi
    bbi = fr[..., None] * bi + fi[..., None] * br
    cr, ci = c_re.astype(F32), c_im.astype(F32)
    eye = jnp.eye(SSM_GROUPS, dtype=F32)
    gp = SSM_GROUPS * SSM_STATE

    def rows_gh(w):
        return jnp.einsum('gph,gk->ghkp', w, eye).reshape(SSM_CH, gp)

    def rows_gp(w):
        return jnp.einsum('ghp,gk->gpkh', w, eye).reshape(gp, SSM_CH)

    b_op = jnp.concatenate([rows_gh(bbr), rows_gh(bbi)], axis=1)
    c_op = jnp.concatenate([rows_gp(cr), rows_gp(-ci)], axis=0)
    n = jnp.arange(1, SSM_CHUNK + 1, dtype=F32)[:, None]
    pmag = jnp.exp(n * (lr * dt).reshape(1, gp))
    ang = n * (li * dt).reshape(1, gp)
    return b_op.astype(BF16), c_op.astype(BF16), pmag * jnp.cos(ang), pmag * jnp.sin(ang)


def _ssm_kernel(u_ref, perm_ref, permt_ref, b_ref, c_ref, pr_ref, pi_ref, d_ref, gw_ref, bgn_ref, o_ref,
                xr_scr, xi_scr, er_scr, ei_scr, cr_scr, ci_scr, *, tm):
    gp = SSM_GROUPS * SSM_STATE
    nchunk = tm // SSM_CHUNK

    @pl.when(pl.program_id(0) == 0)
    def _():
        cr_scr[...] = jnp.zeros_like(cr_scr)
        ci_scr[...] = jnp.zeros_like(ci_scr)

    def pos(s):
        return pl.ds(s * nchunk, nchunk)

    def inject(b):
        up = _dot(perm_ref[...], u_ref[b].astype(BF16)).astype(BF16)
        bu = _dot(up, b_ref[...])
        xr_scr[b] = bu[:, :gp]
        xi_scr[b] = bu[:, gp:]

    def scan(b):
        ar, ai = pr_ref[0:1, :], pi_ref[0:1, :]
        xr, xi = xr_scr[b, pos(0), :], xi_scr[b, pos(0), :]
        for s in range(1, SSM_CHUNK):
            xr, xi = (ar * xr - ai * xi + xr_scr[b, pos(s), :], ar * xi + ai * xr + xi_scr[b, pos(s), :])
            xr_scr[b, pos(s), :] = xr
            xi_scr[b, pos(s), :] = xi
        nr, ni = pr_ref[SSM_CHUNK - 1:SSM_CHUNK, :], pi_ref[SSM_CHUNK - 1:SSM_CHUNK, :]
        er, ei = cr_scr[b], ci_scr[b]
        for c in range(nchunk):
            er_scr[b, c:c + 1, :] = er
            ei_scr[b, c:c + 1, :] = ei
            er, ei = (nr * er - ni * ei + xr[c:c + 1, :], nr * ei + ni * er + xi[c:c + 1, :])
        cr_scr[b] = er
        ci_scr[b] = ei
        er, ei = er_scr[b], ei_scr[b]
        for s in range(SSM_CHUNK):
            sr, si = pr_ref[s:s + 1, :], pi_ref[s:s + 1, :]
            xr_scr[b, pos(s), :] += sr * er - si * ei
            xi_scr[b, pos(s), :] += sr * ei + si * er

    def readout(b):
        yp = _dot(xr_scr[b].astype(BF16), c_ref[0:gp, :]) + _dot(xi_scr[b].astype(BF16), c_ref[gp:, :])
        hi = yp.astype(BF16)
        lo = (yp - hi.astype(F32)).astype(BF16)
        y = _dot(permt_ref[...], hi) + _dot(permt_ref[...], lo) + d_ref[...] * u_ref[b]
        z = _dot(y.astype(BF16), gw_ref[...])
        bg = bgn_ref[:, SB_WIDTH + CONV_CH:]
        o_ref[b] = _rms_rows(z[:, :SSM_CH] * _sigmoid(z[:, SSM_CH:]), bg).astype(BF16)

    nb = u_ref.shape[0]
    for b in range(nb):
        inject(b)
    for b in range(nb):
        scan(b)
        readout(b)


def _ssm_branch(u, layer, ops, d, gw_bf, bgn, tm):
    b_op, c_op, pow_r, pow_i = ops
    b, l, ch = u.shape
    gp = SSM_GROUPS * SSM_STATE
    nchunk = tm // SSM_CHUNK
    full = lambda a: pl.BlockSpec(a.shape, lambda i: (0,) * a.ndim)
    r = jnp.arange(tm)
    perm = (r[None, :] == ((r % nchunk) * SSM_CHUNK + r // nchunk)[:, None]).astype(BF16)
    perm_t = perm.T
    nbytes = (2 * (2 * b * tm * ch * 4 + 2 * tm * tm * 2 + 2 * ch * gp * 2 * 2 + 2 * SSM_CHUNK * gp * 4 + ch * 2 * ch * 2)
              + b * (2 * tm * gp * 4 + 2 * nchunk * gp * 4 + 2 * tm * 2 * gp * 4))
    return pl.pallas_call(
        functools.partial(_ssm_kernel, tm=tm),
        grid=(l // tm,),
        in_specs=[pl.BlockSpec((b, tm, ch), lambda i: (0, i, 0)), full(perm), full(perm_t)]
        + [_layer_block(a, layer) for a in (b_op, c_op, pow_r, pow_i, d, gw_bf, bgn)],
        out_specs=pl.BlockSpec((b, tm, ch), lambda i: (0, i, 0)),
        out_shape=jax.ShapeDtypeStruct((b, l, ch), BF16),
        scratch_shapes=[pltpu.VMEM((b, tm, gp), F32), pltpu.VMEM((b, tm, gp), F32),
                        pltpu.VMEM((b, nchunk, gp), F32), pltpu.VMEM((b, nchunk, gp), F32),
                        pltpu.VMEM((b, 1, gp), F32), pltpu.VMEM((b, 1, gp), F32)],
        compiler_params=_params(("arbitrary",), nbytes),
        name="ssm_branch",
    )(u, perm, perm_t, b_op, c_op, pow_r, pow_i, d, gw_bf, bgn)


def _mix_out(x, sb, cv, sm, bg, w_ref):
    s1 = SB_WIDTH
    s2 = SB_WIDTH + CONV_CH
    y = _dot(_rms_rows(sb, bg).astype(BF16), w_ref[0:s1, :])
    y = y + _dot(cv, w_ref[s1:s2, :])
    y = y + _dot(sm, w_ref[s2:, :])
    return x + y


def _mem_kv_kernel(m_ref, g_ref, wk_ref, wv_ref, kg_ref, k_ref, v_ref):
    hm = _rms_rows(m_ref[0], g_ref[...]).astype(BF16)
    kk = _dot(hm, wk_ref[...])
    for hh in range(XA_HEADS):
        hs = slice(hh * XA_HEAD_DIM, (hh + 1) * XA_HEAD_DIM)
        k_ref[0, :, hs] = _rms_rows(kk[:, hs], kg_ref[...]).astype(BF16)
    v_ref[0] = _dot(hm, wv_ref[...]).astype(BF16)


def _mem_kv(mem, layer, g, wk_bf, wv_bf, kg):
    b, n, d = mem.shape
    blk = pl.BlockSpec((1, n, d), lambda bb: (bb, 0, 0))
    nbytes = 2 * (n * d * 4 + 2 * d * d * 2 + 2 * n * d * 2) + 4 * n * d * 4
    return pl.pallas_call(
        _mem_kv_kernel,
        grid=(b,),
        in_specs=[blk] + [_layer_block(a, layer) for a in (g, wk_bf, wv_bf, kg)],
        out_specs=[blk, blk],
        out_shape=[jax.ShapeDtypeStruct((b, n, d), BF16), jax.ShapeDtypeStruct((b, n, d), BF16)],
        compiler_params=_params(("parallel",), nbytes),
        name="mem_kv",
    )(mem, g, wk_bf, wv_bf, kg)


def _xattn_kernel(x_ref, sb_ref, cv_ref, sm_ref, bgn_ref, wm_ref, g_ref, wq_ref, qg_ref, wo_ref, k_ref, v_ref,
                  o_ref, ob_scr):
    x = _mix_out(x_ref[0], sb_ref[0], cv_ref[0], sm_ref[0], bgn_ref[:, :SB_WIDTH], wm_ref)
    hx = _rms_rows(x, g_ref[...]).astype(BF16)
    q = _dot(hx, wq_ref[...])
    scale = XA_HEAD_DIM ** -0.5
    for hh in range(XA_HEADS):
        hs = slice(hh * XA_HEAD_DIM, (hh + 1) * XA_HEAD_DIM)
        qh = (_rms_rows(q[:, hs], qg_ref[...]) * scale).astype(BF16)
        s = _dot_nt(qh, k_ref[0, :, hs])
        s = s - jnp.max(s, axis=-1, keepdims=True)
        e = jnp.exp(s)
        p = e / jnp.sum(e, axis=-1, keepdims=True)
        ob_scr[:, hs] = _dot(p.astype(BF16), v_ref[0, :, hs]).astype(BF16)
    o_ref[0] = x + _dot(ob_scr[...], wo_ref[...])


def _mix_xattn(x, sb, cv, sm, layer, bgn, wm_bf, g, wq_bf, qg, wo_bf, k_bf, v_bf, tm):
    b, l, d = x.shape
    n = k_bf.shape[1]
    rows = lambda w: pl.BlockSpec((1, tm, w), lambda bb, i: (bb, i, 0))
    kv = pl.BlockSpec((1, n, d), lambda bb, i: (bb, 0, 0))
    nbytes = (2 * (2 * tm * d * 4 + tm * SB_WIDTH * 4 + tm * (CONV_CH + SSM_CH) * 2 + 3 * d * d * 2 + 2 * n * d * 2)
              + 5 * tm * d * 4)
    return pl.pallas_call(
        _xattn_kernel,
        grid=(b, l // tm),
        in_specs=[rows(d), rows(SB_WIDTH), rows(CONV_CH), rows(SSM_CH)]
        + [_layer_block(a, layer) for a in (bgn, wm_bf, g, wq_bf, qg, wo_bf)] + [kv, kv],
        out_specs=rows(d),
        out_shape=jax.ShapeDtypeStruct((b, l, d), F32),
        scratch_shapes=[pltpu.VMEM((tm, d), BF16)],
        compiler_params=_params(("parallel", "parallel"), nbytes),
        name="mix_xattn",
    )(x, sb, cv, sm, bgn, wm_bf, g, wq_bf, qg, wo_bf, k_bf, v_bf)


def _ffn_kernel(x_ref, g_ref, wi_ref, wo_ref, o_ref, *, th):
    hidden = wo_ref.shape[0]
    x = x_ref[...]
    h = _rms_rows(x, g_ref[...]).astype(BF16)
    o_ref[...] = x
    for c0 in range(0, hidden, th):
        c1 = min(c0 + th, hidden)
        gate = _dot(h, wi_ref[:, c0:c1])
        up = _dot(h, wi_ref[:, hidden + c0:hidden + c1])
        act = (gate * _sigmoid(gate) * up).astype(BF16)
        o_ref[...] += _dot(act, wo_ref[c0:c1, :])


def _ffn(x2, layer, g, w_in_bf, w_out_bf, tm, th):
    t, d = x2.shape
    hidden = w_out_bf.shape[1]
    once = pl.Buffered(1)
    nbytes = 3 * d * hidden * 2 + 2 * 2 * tm * d * 4 + tm * d * 2 + 6 * tm * th * 4
    return pl.pallas_call(
        functools.partial(_ffn_kernel, th=th),
        grid=(t // tm,),
        in_specs=[pl.BlockSpec((tm, d), lambda i: (i, 0)), _layer_block(g, layer),
                  pl.BlockSpec((None, d, 2 * hidden), lambda i: (layer, 0, 0), pipeline_mode=once),
                  pl.BlockSpec((None, hidden, d), lambda i: (layer, 0, 0), pipeline_mode=once)],
        out_specs=pl.BlockSpec((tm, d), lambda i: (i, 0)),
        out_shape=jax.ShapeDtypeStruct((t, d), F32),
        compiler_params=_params(("parallel",), nbytes),
        name="ffn",
    )(x2, g, w_in_bf, w_out_bf)


def _tile(n, want):
    want = min(want, n)
    for cand in range(want, 0, -1):
        if n % cand == 0 and (cand % V7X_SUBLANES == 0 or cand == n):
            return cand
    return n


def kernel(x, mem, norm_mix_g, w_in, sb_q_norm_g, sb_k_norm_g, conv_dw_w, conv_dw_b, conv_ln_g, conv_ln_b, conv_pw2_w, ssm_lam_re, ssm_lam_im, ssm_log_dt, ssm_b_re, ssm_b_im, ssm_c_re, ssm_c_im, ssm_d, ssm_glu_w, branch_norm_g, w_out, norm_xa_g, norm_mem_g, xa_wq, xa_wk, xa_wv, xa_q_norm_g, xa_k_norm_g, xa_wo, norm_ffn_g, ffn_w_in, ffn_w_out):
    bsz, seq, d = x.shape
    depth = w_in.shape[0]
    t = bsz * seq
    tm = _tile(seq, 512)
    tx = _tile(seq, 1024)
    tq = _tile(seq, 256)
    ts = _tile(seq, 512)
    th = 512
    assert ts % (SSM_CHUNK * V7X_SUBLANES) == 0 and tm >= CONV_HALO
    bf = lambda w: w.astype(BF16)
    vec = lambda a: a.astype(F32).reshape(depth, 1, -1)
    qg = vec(jnp.tile(sb_q_norm_g, (1, SB_HEADS))) * (SB_HEAD_DIM ** -0.5)
    kg = vec(jnp.tile(sb_k_norm_g, (1, SB_HEADS)))
    conv = (conv_dw_w.astype(F32), vec(conv_dw_b), vec(conv_ln_g), vec(conv_ln_b), bf(conv_pw2_w))
    ops = jax.vmap(_ssm_operators)(ssm_lam_re, ssm_lam_im, ssm_log_dt, ssm_b_re, ssm_b_im, ssm_c_re, ssm_c_im)
    bgn = vec(branch_norm_g)
    w_in_bf, w_out_bf, ssm_glu_bf = bf(w_in), bf(w_out), bf(ssm_glu_w)
    xa_wq_bf, xa_wk_bf, xa_wv_bf, xa_wo_bf = bf(xa_wq), bf(xa_wk), bf(xa_wv), bf(xa_wo)
    ffn_w_in_bf, ffn_w_out_bf = bf(ffn_w_in), bf(ffn_w_out)

    seg = (jnp.arange(SEG_WIDTH)[:, None] // SB_HEAD_DIM == jnp.arange(SEG_WIDTH)[None, :] // SB_HEAD_DIM).astype(BF16)
    ntri = -(jnp.arange(tq)[:, None] > jnp.arange(tq)[None, :]).astype(BF16)

    x2 = x.reshape(t, d)
    for l in range(depth):
        q, k, v, o_conv, u = _mix_in(x2, seq, l, vec(norm_mix_g), w_in_bf, qg, kg, seg, conv, bgn, tx)
        o_conv = o_conv.reshape(bsz, seq, CONV_CH)
        o_sb = _sb_attn(q.reshape(bsz, seq, SB_WIDTH), k.reshape(bsz, seq, SB_WIDTH),
                        v.reshape(bsz, seq, SB_WIDTH), ntri, tq)
        o_ssm = _ssm_branch(u.reshape(bsz, seq, SSM_CH), l, ops, vec(ssm_d), ssm_glu_bf, bgn, ts)
        k_m, v_m = _mem_kv(mem, l, vec(norm_mem_g), xa_wk_bf, xa_wv_bf, vec(xa_k_norm_g))
        x2 = _mix_xattn(x2.reshape(bsz, seq, d), o_sb, o_conv, o_ssm, l, bgn, w_out_bf, vec(norm_xa_g), xa_wq_bf,
                        vec(xa_q_norm_g), xa_wo_bf, k_m, v_m, tx).reshape(t, d)
        x2 = _ffn(x2, l, vec(norm_ffn_g), ffn_w_in_bf, ffn_w_out_bf, tm, th)
    return x2.reshape(bsz, seq, d)
```

```python
import functools
import math

import jax
import jax.numpy as jnp
from jax import lax
from jax.experimental import pallas as pl
from jax.experimental.pallas import tpu as pltpu

F32 = jnp.float32
BF16 = jnp.bfloat16
EPS = 1e-6

V7X_LANES = 128
V7X_SUBLANES = 8
V7X_VMEM_BYTES = 64 * 1024 * 1024

SB_HEADS = 8
SB_HEAD_DIM = 64
SB_WIDTH = SB_HEADS * SB_HEAD_DIM
SEG_WIDTH = 256
CONV_CH = 256
CONV_WIDTH = 31
CONV_HALO = 32
SSM_CH = 256
SSM_GROUP = 16
SSM_GROUPS = SSM_CH // SSM_GROUP
SSM_STATE = 64
SSM_CHUNK = 16
XA_HEADS = 4
XA_HEAD_DIM = 256


def _vmem_limit(nbytes):
    return int(min(max(nbytes * 3 // 2, 16 * 1024 * 1024), V7X_VMEM_BYTES - 8 * 1024 * 1024))


def _params(semantics, nbytes):
    return pltpu.CompilerParams(dimension_semantics=semantics, vmem_limit_bytes=_vmem_limit(nbytes))


def _dot(a, b):
    return jnp.dot(a, b, preferred_element_type=F32)


def _wdot(a, w):
    return jnp.dot(a, w.astype(BF16), preferred_element_type=F32)


def _dot_nt(a, b):
    return lax.dot_general(a, b, (((1,), (1,)), ((), ())), preferred_element_type=F32)


def _rms_rows(xf, g):
    return xf * lax.rsqrt(jnp.mean(xf * xf, axis=-1, keepdims=True) + EPS) * g


def _sigmoid(x):
    return 1.0 / (1.0 + jnp.exp(-x))


def _segment_mean_sq(p, seg):
    sq = p * p
    hi = sq.astype(BF16)
    lo = (sq - hi.astype(F32)).astype(BF16)
    parts = []
    for c0 in range(0, p.shape[1], SEG_WIDTH):
        cols = slice(c0, c0 + SEG_WIDTH)
        parts.append(_dot(hi[:, cols], seg) + _dot(lo[:, cols], seg))
    return jnp.concatenate(parts, axis=1) * (1.0 / SB_HEAD_DIM)


def _mix_in_kernel(x_ref, g_ref, w_ref, qg_ref, kg_ref, seg_ref, dww_ref, dwb_ref, lng_ref, lnb_ref, pw_ref, bgn_ref,
                   q_ref, k_ref, v_ref, cv_ref, u_ref, hbuf, sh_scr, *, tm, tiles_per_seq):
    i = pl.program_id(0)

    @pl.when(i == 0)
    def _():
        hbuf[...] = jnp.zeros_like(hbuf)

    h = _rms_rows(x_ref[...], g_ref[...]).astype(BF16)
    s1, s2, s3 = SB_WIDTH, 2 * SB_WIDTH, 3 * SB_WIDTH
    s4 = s3 + 2 * CONV_CH
    seg = seg_ref[...]
    pc = _wdot(h, w_ref[:, s3:s4])
    history = hbuf[tm:tm + CONV_HALO, :]
    hbuf[:CONV_HALO, :] = jnp.where(i % tiles_per_seq == 0, 0.0, history)
    hbuf[CONV_HALO:, :] = pc[:, :CONV_CH] * _sigmoid(pc[:, CONV_CH:])
    quarter = -(-CONV_WIDTH // 4)
    taps = [range(n * quarter, min((n + 1) * quarter, CONV_WIDTH)) for n in range(4)]
    pq = _wdot(h, w_ref[:, 0:s1])
    _conv_shift(hbuf, sh_scr, tm)
    acc = _conv_taps(hbuf, sh_scr, dww_ref, jnp.zeros((tm, CONV_CH), F32) + dwb_ref[...], taps[0], tm)
    q_ref[...] = (pq * lax.rsqrt(_segment_mean_sq(pq, seg) + EPS) * qg_ref[...]).astype(BF16)
    pk = _wdot(h, w_ref[:, s1:s2])
    acc = _conv_taps(hbuf, sh_scr, dww_ref, acc, taps[1], tm)
    k_ref[...] = (pk * lax.rsqrt(_segment_mean_sq(pk, seg) + EPS) * kg_ref[...]).astype(BF16)
    pv = _wdot(h, w_ref[:, s2:s3])
    acc = _conv_taps(hbuf, sh_scr, dww_ref, acc, taps[2], tm)
    v_ref[...] = pv.astype(BF16)
    pu = _wdot(h, w_ref[:, s4:])
    acc = _conv_taps(hbuf, sh_scr, dww_ref, acc, taps[3], tm)
    u_ref[...] = pu
    cv_ref[...] = _conv_tail(acc, lng_ref, lnb_ref, pw_ref, bgn_ref[:, SB_WIDTH:SB_WIDTH + CONV_CH])


def _layer_block(a, layer):
    tail = a.shape[1:]
    return pl.BlockSpec((None,) + tail, lambda *_: (layer,) + (0,) * len(tail), pipeline_mode=pl.Buffered(1))


def _mix_in(x2, seq, layer, g, w_bf, qg, kg, seg, conv, bgn, tm):
    t, d = x2.shape
    n_in = w_bf.shape[2]
    row = lambda i: (i, 0)
    hist = tm + CONV_HALO
    nbytes = (2 * (tm * d * 4 + d * n_in * 2 + tm * (3 * SB_WIDTH * 2 + CONV_CH * 2 + SSM_CH * 4)) + tm * n_in * 4
              + hist * CONV_CH * 4 * 14)
    return pl.pallas_call(
        functools.partial(_mix_in_kernel, tm=tm, tiles_per_seq=seq // tm),
        grid=(t // tm,),
        in_specs=[pl.BlockSpec((tm, d), row)]
        + [_layer_block(a, layer) for a in (g, w_bf, qg, kg)]
        + [pl.BlockSpec((SEG_WIDTH, SEG_WIDTH), lambda i: (0, 0))]
        + [_layer_block(a, layer) for a in (*conv, bgn)],
        out_specs=[pl.BlockSpec((tm, SB_WIDTH), row), pl.BlockSpec((tm, SB_WIDTH), row),
                   pl.BlockSpec((tm, SB_WIDTH), row), pl.BlockSpec((tm, CONV_CH), row),
                   pl.BlockSpec((tm, SSM_CH), row)],
        out_shape=[jax.ShapeDtypeStruct((t, SB_WIDTH), BF16), jax.ShapeDtypeStruct((t, SB_WIDTH), BF16),
                   jax.ShapeDtypeStruct((t, SB_WIDTH), BF16), jax.ShapeDtypeStruct((t, CONV_CH), BF16),
                   jax.ShapeDtypeStruct((t, SSM_CH), F32)],
        scratch_shapes=[pltpu.VMEM((hist, CONV_CH), F32),
                        pltpu.VMEM((V7X_SUBLANES - 1, hist - V7X_SUBLANES, CONV_CH), F32)],
        compiler_params=_params(("arbitrary",), nbytes),
        name="mix_in",
    )(x2, g, w_bf, qg, kg, seg, *conv, bgn)


SB_SLOTS = 2
SB_GROUPS_PER_STEP = 2
SB_SOFTPLUS_LINEAR = 40.0
SB_EXP_UNDERFLOW = -104.0
SB_NEVER = -1e30


def _sb_decay(z, run, ntri, mask):
    sp = jnp.maximum(jnp.log(1.0 + jnp.exp(jnp.minimum(z, SB_SOFTPLUS_LINEAR))), z)
    if mask is not None:
        sp = jnp.where(mask, sp, 0.0)
    later = _dot(sp.astype(BF16), ntri)
    return z - sp + run, later, run + later[:, 0:1] - sp[:, 0:1]


def _sb_weights(t0, later, mask):
    w = jnp.exp(t0 + later)
    if mask is not None:
        w = jnp.where(mask, w, 0.0)
    return w.astype(BF16)


def _sb_attn_kernel(q_ref, k_ref, v_ref, ntri_ref, o_ref, q2_scr, z_scr, t0_scr, lat_scr, run_scr, acc_scr, *, tq):
    i = pl.program_id(2)
    nh = V7X_LANES // SB_HEAD_DIM
    m = nh * tq
    groups = [slice(g * V7X_LANES, (g + 1) * V7X_LANES) for g in range(SB_GROUPS_PER_STEP)]

    def rows(j):
        return pl.ds(pl.multiple_of((i - jnp.minimum(j, i)) * tq, tq), tq)

    def scores(j, slot):
        for g, gs in enumerate(groups):
            z_scr[g, slot] = _dot_nt(q2_scr[g], k_ref[0, rows(j), gs])

    def decay(slot, mask=None, bias=None):
        for g in range(len(groups)):
            run = run_scr[g] if bias is None else run_scr[g] + bias
            t0, later, run = _sb_decay(z_scr[g, slot], run, ntri_ref[...], mask)
            t0_scr[g, slot] = t0
            lat_scr[g, slot] = later
            run_scr[g] = run

    def output(j, slot, mask=None):
        for g, gs in enumerate(groups):
            acc_scr[g] += _dot(_sb_weights(t0_scr[g, slot], lat_scr[g, slot], mask), v_ref[0, rows(j), gs])

    def live():
        run = run_scr[0]
        for g in range(1, len(groups)):
            run = jnp.maximum(run, run_scr[g])
        return (jnp.max(run) >= SB_EXP_UNDERFLOW).astype(jnp.int32)

    lane_head = lax.broadcasted_iota(jnp.int32, (tq, V7X_LANES), 1) // SB_HEAD_DIM
    for g, gs in enumerate(groups):
        q = q_ref[0, :, gs]
        for h in range(nh):
            q2_scr[g, h * tq:(h + 1) * tq, :] = jnp.where(lane_head == h, q, jnp.zeros_like(q))
    row = lax.broadcasted_iota(jnp.int32, (m, tq), 0) % tq
    col = lax.broadcasted_iota(jnp.int32, (m, tq), 1)
    diag_mask = col < row
    run_scr[...] = jnp.zeros_like(run_scr)
    acc_scr[...] = jnp.zeros_like(acc_scr)

    scores(0, 0)
    scores(1, 1)
    decay(0, mask=diag_mask)
    decay(1, bias=jnp.where(i >= 1, 0.0, SB_NEVER))
    output(0, 0, diag_mask)
    output(1, 1)

    @pl.when(jnp.logical_and(i >= 2, live() > 0))
    def _():
        scores(2, 0)
        decay(0)
        scores(3, 1)
        more = live()
        output(2, 0)
        scores(4, 0)
        decay(1)
        n_pairs = (i - 2) // 2

        def cond(carry):
            p, more = carry
            return jnp.logical_and(p < n_pairs, more > 0)

        def body(carry):
            p, _ = carry
            t = 2 * p + 3
            scores(t + 2, 1)
            output(t, 1)
            decay(0)
            more = live()
            scores(t + 3, 0)
            output(t + 1, 0)
            decay(1)
            return p + 1, more

        _, more = lax.while_loop(cond, body, (0, more))

        @pl.when(jnp.logical_and(more > 0, (i - 2) % 2 == 1))
        def _():
            output(i, 1)

    for g, gs in enumerate(groups):
        out = acc_scr[g, 0:tq, :]
        for h in range(1, nh):
            out = jnp.where(lane_head == h, acc_scr[g, h * tq:(h + 1) * tq, :], out)
        o_ref[0, :, gs] = out


def _sb_attn(q, k, v, ntri, tq):
    b, l, w = q.shape
    ng = SB_GROUPS_PER_STEP
    gw = ng * V7X_LANES
    m = (V7X_LANES // SB_HEAD_DIM) * tq
    blk = pltpu.VMEM((ng, SB_SLOTS, m, tq), F32)
    nbytes = (2 * (2 * l * gw * 2 + tq * gw * 2 + tq * gw * 4 + tq * tq * 2)
              + ng * (3 * SB_SLOTS * m * tq * 4 + 3 * m * V7X_LANES * 4) + 6 * m * tq * 4)
    return pl.pallas_call(
        functools.partial(_sb_attn_kernel, tq=tq),
        grid=(b, w // gw, l // tq),
        in_specs=[pl.BlockSpec((1, tq, gw), lambda bb, hp, i: (bb, i, hp)),
                  pl.BlockSpec((1, l, gw), lambda bb, hp, i: (bb, 0, hp)),
                  pl.BlockSpec((1, l, gw), lambda bb, hp, i: (bb, 0, hp)),
                  pl.BlockSpec((tq, tq), lambda bb, hp, i: (0, 0))],
        out_specs=pl.BlockSpec((1, tq, gw), lambda bb, hp, i: (bb, i, hp)),
        out_shape=jax.ShapeDtypeStruct((b, l, w), F32),
        scratch_shapes=[pltpu.VMEM((ng, m, V7X_LANES), BF16), blk, blk, blk,
                        pltpu.VMEM((ng, m, 1), F32), pltpu.VMEM((ng, m, V7X_LANES), F32)],
        compiler_params=_params(("parallel", "parallel", "arbitrary"), nbytes),
        name="sb_attn",
    )(q, k, v, ntri)


def _conv_shift(hbuf, sh_scr, tc):
    span = tc + CONV_HALO - V7X_SUBLANES
    for ph in range(1, V7X_SUBLANES):
        sh_scr[ph - 1] = hbuf[ph:ph + span, :]


def _conv_taps(hbuf, sh_scr, dww_ref, acc, taps, tc):
    off = CONV_HALO - (CONV_WIDTH - 1)
    for j in taps:
        ph = (off + j) % V7X_SUBLANES
        base = off + j - ph
        tap = hbuf[base:base + tc, :] if ph == 0 else sh_scr[ph - 1, base:base + tc, :]
        acc = acc + dww_ref[j:j + 1, :] * tap
    return acc


def _conv_tail(acc, lng_ref, lnb_ref, pw_ref, bg):
    mu = jnp.mean(acc, axis=-1, keepdims=True)
    cen = acc - mu
    var = jnp.mean(cen * cen, axis=-1, keepdims=True)
    y = cen * lax.rsqrt(var + EPS) * lng_ref[...] + lnb_ref[...]
    y = y * _sigmoid(y)
    o = _wdot(y.astype(BF16), pw_ref[...])
    return _rms_rows(o, bg).astype(BF16)


def _ssm_operators(lam_re, lam_im, log_dt, b_re, b_im, c_re, c_im):
    lr, li = lam_re.astype(F32), lam_im.astype(F32)
    dt = jnp.exp(log_dt.astype(F32))[:, None]
    mag = jnp.exp(lr * dt)
    ar, ai = mag * jnp.cos(li * dt), mag * jnp.sin(li * dt)
    den = lr * lr + li * li
    fr = ((ar - 1.0) * lr + ai * li) / den
    fi = (ai * lr - (ar - 1.0) * li) / den
    br, bi = b_re.astype(F32), b_im.astype(F32)
    bbr = fr[..., None] * br - fi[..., None] * bi
    bbi = fr[..., None] * bi + fi[..., None] * br
    cr, ci = c_re.astype(F32), c_im.astype(F32)
    eye = jnp.eye(SSM_GROUPS, dtype=F32)
    gp = SSM_GROUPS * SSM_STATE

    def rows_gh(w):
        return jnp.einsum('gph,gk->ghkp', w, eye).reshape(SSM_CH, gp)

    def rows_gp(w):
        return jnp.einsum('ghp,gk->gpkh', w, eye).reshape(gp, SSM_CH)

    b_op = jnp.concatenate([rows_gh(bbr), rows_gh(bbi)], axis=1)
    c_op = jnp.concatenate([rows_gp(cr), rows_gp(-ci)], axis=0)
    n = jnp.arange(1, SSM_CHUNK + 1, dtype=F32)[:, None]
    pmag = jnp.exp(n * (lr * dt).reshape(1, gp))
    ang = n * (li * dt).reshape(1, gp)
    return b_op.astype(BF16), c_op.astype(BF16), pmag * jnp.cos(ang), pmag * jnp.sin(ang)


def _ssm_kernel(u_ref, perm_ref, permt_ref, b_ref, c_ref, pr_ref, pi_ref, d_ref, gw_ref, bgn_ref, o_ref,
                xr_scr, xi_scr, er_scr, ei_scr, cr_scr, ci_scr, *, tm):
    gp = SSM_GROUPS * SSM_STATE
    nchunk = tm // SSM_CHUNK

    @pl.when(pl.program_id(0) == 0)
    def _():
        cr_scr[...] = jnp.zeros_like(cr_scr)
        ci_scr[...] = jnp.zeros_like(ci_scr)

    def pos(s):
        return pl.ds(s * nchunk, nchunk)

    def inject(b):
        up = _dot(perm_ref[...], u_ref[b].astype(BF16)).astype(BF16)
        bu = _dot(up, b_ref[...])
        xr_scr[b] = bu[:, :gp]
        xi_scr[b] = bu[:, gp:]

    def scan(b):
        ar, ai = pr_ref[0:1, :], pi_ref[0:1, :]
        xr, xi = xr_scr[b, pos(0), :], xi_scr[b, pos(0), :]
        for s in range(1, SSM_CHUNK):
            xr, xi = (ar * xr - ai * xi + xr_scr[b, pos(s), :], ar * xi + ai * xr + xi_scr[b, pos(s), :])
            xr_scr[b, pos(s), :] = xr
            xi_scr[b, pos(s), :] = xi
        nr, ni = pr_ref[SSM_CHUNK - 1:SSM_CHUNK, :], pi_ref[SSM_CHUNK - 1:SSM_CHUNK, :]
        er, ei = cr_scr[b], ci_scr[b]
        for c in range(nchunk):
            er_scr[b, c:c + 1, :] = er
            ei_scr[b, c:c + 1, :] = ei
            er, ei = (nr * er - ni * ei + xr[c:c + 1, :], nr * ei + ni * er + xi[c:c + 1, :])
        cr_scr[b] = er
        ci_scr[b] = ei
        er, ei = er_scr[b], ei_scr[b]
        for s in range(SSM_CHUNK):
            sr, si = pr_ref[s:s + 1, :], pi_ref[s:s + 1, :]
            xr_scr[b, pos(s), :] += sr * er - si * ei
            xi_scr[b, pos(s), :] += sr * ei + si * er

    def readout(b):
        yp = _dot(xr_scr[b].astype(BF16), c_ref[0:gp, :]) + _dot(xi_scr[b].astype(BF16), c_ref[gp:, :])
        hi = yp.astype(BF16)
        lo = (yp - hi.astype(F32)).astype(BF16)
        y = _dot(permt_ref[...], hi) + _dot(permt_ref[...], lo) + d_ref[...] * u_ref[b]
        z = _wdot(y.astype(BF16), gw_ref[...])
        bg = bgn_ref[:, SB_WIDTH + CONV_CH:]
        o_ref[b] = _rms_rows(z[:, :SSM_CH] * _sigmoid(z[:, SSM_CH:]), bg).astype(BF16)

    nb = u_ref.shape[0]
    for b in range(nb):
        inject(b)
    for b in range(nb):
        scan(b)
        readout(b)


def _ssm_branch(u, layer, ops, d, gw_bf, bgn, tm):
    b_op, c_op, pow_r, pow_i = ops
    b, l, ch = u.shape
    gp = SSM_GROUPS * SSM_STATE
    nchunk = tm // SSM_CHUNK
    full = lambda a: pl.BlockSpec(a.shape, lambda i: (0,) * a.ndim)
    r = jnp.arange(tm)
    perm = (r[None, :] == ((r % nchunk) * SSM_CHUNK + r // nchunk)[:, None]).astype(BF16)
    perm_t = perm.T
    nbytes = (2 * (2 * b * tm * ch * 4 + 2 * tm * tm * 2 + 2 * ch * gp * 2 * 2 + 2 * SSM_CHUNK * gp * 4 + ch * 2 * ch * 2)
              + b * (2 * tm * gp * 4 + 2 * nchunk * gp * 4 + 2 * tm * 2 * gp * 4))
    return pl.pallas_call(
        functools.partial(_ssm_kernel, tm=tm),
        grid=(l // tm,),
        in_specs=[pl.BlockSpec((b, tm, ch), lambda i: (0, i, 0)), full(perm), full(perm_t)]
        + [_layer_block(a, layer) for a in (b_op, c_op, pow_r, pow_i, d, gw_bf, bgn)],
        out_specs=pl.BlockSpec((b, tm, ch), lambda i: (0, i, 0)),
        out_shape=jax.ShapeDtypeStruct((b, l, ch), BF16),
        scratch_shapes=[pltpu.VMEM((b, tm, gp), F32), pltpu.VMEM((b, tm, gp), F32),
                        pltpu.VMEM((b, nchunk, gp), F32), pltpu.VMEM((b, nchunk, gp), F32),
                        pltpu.VMEM((b, 1, gp), F32), pltpu.VMEM((b, 1, gp), F32)],
        compiler_params=_params(("arbitrary",), nbytes),
        name="ssm_branch",
    )(u, perm, perm_t, b_op, c_op, pow_r, pow_i, d, gw_bf, bgn)


def _mix_out(x, sb, cv, sm, bg, w_ref):
    s1 = SB_WIDTH
    s2 = SB_WIDTH + CONV_CH
    y = _wdot(_rms_rows(sb, bg).astype(BF16), w_ref[0:s1, :])
    y = y + _wdot(cv, w_ref[s1:s2, :])
    y = y + _wdot(sm, w_ref[s2:, :])
    return x + y


def _mem_kv_kernel(m_ref, g_ref, wk_ref, wv_ref, kg_ref, k_ref, v_ref):
    hm = _rms_rows(m_ref[0], g_ref[...]).astype(BF16)
    kk = _wdot(hm, wk_ref[...])
    for hh in range(XA_HEADS):
        hs = slice(hh * XA_HEAD_DIM, (hh + 1) * XA_HEAD_DIM)
        k_ref[0, :, hs] = _rms_rows(kk[:, hs], kg_ref[...]).astype(BF16)
    v_ref[0] = _wdot(hm, wv_ref[...]).astype(BF16)


def _mem_kv(mem, layer, g, wk_bf, wv_bf, kg):
    b, n, d = mem.shape
    blk = pl.BlockSpec((1, n, d), lambda bb: (bb, 0, 0))
    nbytes = 2 * (n * d * 4 + 2 * d * d * 2 + 2 * n * d * 2) + 4 * n * d * 4
    return pl.pallas_call(
        _mem_kv_kernel,
        grid=(b,),
        in_specs=[blk] + [_layer_block(a, layer) for a in (g, wk_bf, wv_bf, kg)],
        out_specs=[blk, blk],
        out_shape=[jax.ShapeDtypeStruct((b, n, d), BF16), jax.ShapeDtypeStruct((b, n, d), BF16)],
        compiler_params=_params(("parallel",), nbytes),
        name="mem_kv",
    )(mem, g, wk_bf, wv_bf, kg)


def _xattn_kernel(x_ref, sb_ref, cv_ref, sm_ref, bgn_ref, wm_ref, g_ref, wq_ref, qg_ref, wo_ref, k_ref, v_ref,
                  o_ref, ob_scr):
    x = _mix_out(x_ref[0], sb_ref[0], cv_ref[0], sm_ref[0], bgn_ref[:, :SB_WIDTH], wm_ref)
    hx = _rms_rows(x, g_ref[...]).astype(BF16)
    q = _wdot(hx, wq_ref[...])
    scale = XA_HEAD_DIM ** -0.5
    for hh in range(XA_HEADS):
        hs = slice(hh * XA_HEAD_DIM, (hh + 1) * XA_HEAD_DIM)
        qh = (_rms_rows(q[:, hs], qg_ref[...]) * scale).astype(BF16)
        s = _dot_nt(qh, k_ref[0, :, hs])
        s = s - jnp.max(s, axis=-1, keepdims=True)
        e = jnp.exp(s)
        p = e / jnp.sum(e, axis=-1, keepdims=True)
        ob_scr[:, hs] = _dot(p.astype(BF16), v_ref[0, :, hs]).astype(BF16)
    o_ref[0] = x + _wdot(ob_scr[...], wo_ref[...])


def _mix_xattn(x, sb, cv, sm, layer, bgn, wm_bf, g, wq_bf, qg, wo_bf, k_bf, v_bf, tm):
    b, l, d = x.shape
    n = k_bf.shape[1]
    rows = lambda w: pl.BlockSpec((1, tm, w), lambda bb, i: (bb, i, 0))
    kv = pl.BlockSpec((1, n, d), lambda bb, i: (bb, 0, 0))
    nbytes = (2 * (2 * tm * d * 4 + tm * SB_WIDTH * 4 + tm * (CONV_CH + SSM_CH) * 2 + 3 * d * d * 2 + 2 * n * d * 2)
              + 5 * tm * d * 4)
    return pl.pallas_call(
        _xattn_kernel,
        grid=(b, l // tm),
        in_specs=[rows(d), rows(SB_WIDTH), rows(CONV_CH), rows(SSM_CH)]
        + [_layer_block(a, layer) for a in (bgn, wm_bf, g, wq_bf, qg, wo_bf)] + [kv, kv],
        out_specs=rows(d),
        out_shape=jax.ShapeDtypeStruct((b, l, d), F32),
        scratch_shapes=[pltpu.VMEM((tm, d), BF16)],
        compiler_params=_params(("parallel", "parallel"), nbytes),
        name="mix_xattn",
    )(x, sb, cv, sm, bgn, wm_bf, g, wq_bf, qg, wo_bf, k_bf, v_bf)


def _ffn_kernel(x_ref, g_ref, wi_ref, wo_ref, o_ref, *, th):
    hidden = wo_ref.shape[0]
    x = x_ref[...]
    h = _rms_rows(x, g_ref[...]).astype(BF16)
    o_ref[...] = x
    for c0 in range(0, hidden, th):
        c1 = min(c0 + th, hidden)
        gate = _wdot(h, wi_ref[:, c0:c1])
        up = _wdot(h, wi_ref[:, hidden + c0:hidden + c1])
        act = (gate * _sigmoid(gate) * up).astype(BF16)
        o_ref[...] += _wdot(act, wo_ref[c0:c1, :])


def _ffn(x2, layer, g, w_in, w_out, tm, th):
    t, d = x2.shape
    hidden = w_out.shape[1]
    nbytes = 3 * d * hidden * 4 + 2 * 2 * tm * d * 4 + tm * d * 2 + 6 * tm * th * 4
    return pl.pallas_call(
        functools.partial(_ffn_kernel, th=th),
        grid=(t // tm,),
        in_specs=[pl.BlockSpec((tm, d), lambda i: (i, 0))] + [_layer_block(a, layer) for a in (g, w_in, w_out)],
        out_specs=pl.BlockSpec((tm, d), lambda i: (i, 0)),
        out_shape=jax.ShapeDtypeStruct((t, d), F32),
        compiler_params=_params(("parallel",), nbytes),
        name="ffn",
    )(x2, g, w_in, w_out)


def _tile(n, want):
    want = min(want, n)
    for cand in range(want, 0, -1):
        if n % cand == 0 and (cand % V7X_SUBLANES == 0 or cand == n):
            return cand
    return n


def kernel(x, mem, norm_mix_g, w_in, sb_q_norm_g, sb_k_norm_g, conv_dw_w, conv_dw_b, conv_ln_g, conv_ln_b, conv_pw2_w, ssm_lam_re, ssm_lam_im, ssm_log_dt, ssm_b_re, ssm_b_im, ssm_c_re, ssm_c_im, ssm_d, ssm_glu_w, branch_norm_g, w_out, norm_xa_g, norm_mem_g, xa_wq, xa_wk, xa_wv, xa_q_norm_g, xa_k_norm_g, xa_wo, norm_ffn_g, ffn_w_in, ffn_w_out):
    bsz, seq, d = x.shape
    depth = w_in.shape[0]
    t = bsz * seq
    tm = _tile(seq, 512)
    tx = _tile(seq, 1024)
    tq = _tile(seq, 256)
    ts = _tile(seq, 512)
    th = 512
    assert ts % (SSM_CHUNK * V7X_SUBLANES) == 0 and tm >= CONV_HALO
    vec = lambda a: a.astype(F32).reshape(depth, 1, -1)
    qg = vec(jnp.tile(sb_q_norm_g, (1, SB_HEADS))) * (SB_HEAD_DIM ** -0.5)
    kg = vec(jnp.tile(sb_k_norm_g, (1, SB_HEADS)))
    conv = (conv_dw_w.astype(F32), vec(conv_dw_b), vec(conv_ln_g), vec(conv_ln_b), conv_pw2_w.astype(F32))
    ops = jax.vmap(_ssm_operators)(ssm_lam_re, ssm_lam_im, ssm_log_dt, ssm_b_re, ssm_b_im, ssm_c_re, ssm_c_im)
    bgn = vec(branch_norm_g)

    seg = (jnp.arange(SEG_WIDTH)[:, None] // SB_HEAD_DIM == jnp.arange(SEG_WIDTH)[None, :] // SB_HEAD_DIM).astype(BF16)
    ntri = -(jnp.arange(tq)[:, None] > jnp.arange(tq)[None, :]).astype(BF16)

    x2 = x.reshape(t, d)
    for l in range(depth):
        q, k, v, o_conv, u = _mix_in(x2, seq, l, vec(norm_mix_g), w_in.astype(F32), qg, kg, seg, conv, bgn, tx)
        o_conv = o_conv.reshape(bsz, seq, CONV_CH)
        o_sb = _sb_attn(q.reshape(bsz, seq, SB_WIDTH), k.reshape(bsz, seq, SB_WIDTH),
                        v.reshape(bsz, seq, SB_WIDTH), ntri, tq)
        o_ssm = _ssm_branch(u.reshape(bsz, seq, SSM_CH), l, ops, vec(ssm_d), ssm_glu_w.astype(F32), bgn, ts)
        k_m, v_m = _mem_kv(mem, l, vec(norm_mem_g), xa_wk.astype(F32), xa_wv.astype(F32), vec(xa_k_norm_g))
        x2 = _mix_xattn(x2.reshape(bsz, seq, d), o_sb, o_conv, o_ssm, l, bgn, w_out.astype(F32), vec(norm_xa_g),
                        xa_wq.astype(F32), vec(xa_q_norm_g), xa_wo.astype(F32), k_m, v_m, tx).reshape(t, d)
        x2 = _ffn(x2, l, vec(norm_ffn_g), ffn_w_in.astype(F32), ffn_w_out.astype(F32), tm, th)
    return x2.reshape(bsz, seq, d)
```

```python
import functools

import jax
import jax.numpy as jnp
from jax import lax
from jax.experimental import pallas as pl
from jax.experimental.pallas import tpu as pltpu

F32 = jnp.float32
BF16 = jnp.bfloat16
EPS = 1e-6

V7X_LANES = 128
V7X_SUBLANES = 8
V7X_VMEM_BYTES = 64 * 1024 * 1024

SB_HEADS = 8
SB_HEAD_DIM = 64
SB_WIDTH = SB_HEADS * SB_HEAD_DIM
SEG_WIDTH = 256
CONV_CH = 256
CONV_WIDTH = 31
CONV_HALO = 32
SSM_CH = 256
SSM_GROUP = 16
SSM_GROUPS = SSM_CH // SSM_GROUP
SSM_STATE = 64
SSM_CHUNK = 16
XA_HEADS = 4
XA_HEAD_DIM = 256


def _vmem_limit(nbytes):
    return int(min(max(nbytes * 3 // 2, 16 * 1024 * 1024), V7X_VMEM_BYTES - 8 * 1024 * 1024))


def _params(semantics, nbytes):
    return pltpu.CompilerParams(dimension_semantics=semantics, vmem_limit_bytes=_vmem_limit(nbytes))


def _dot(a, b):
    return jnp.dot(a, b, preferred_element_type=F32)


def _wdot(a, w):
    return jnp.dot(a, w.astype(BF16), preferred_element_type=F32)


def _dot_nt(a, b):
    return lax.dot_general(a, b, (((1,), (1,)), ((), ())), preferred_element_type=F32)


def _rms_rows(xf, g):
    return xf * lax.rsqrt(jnp.mean(xf * xf, axis=-1, keepdims=True) + EPS) * g


def _sigmoid(x):
    return 1.0 / (1.0 + jnp.exp(-x))


def _segment_mean_sq(p, seg):
    sq = p * p
    hi = sq.astype(BF16)
    lo = (sq - hi.astype(F32)).astype(BF16)
    parts = []
    for c0 in range(0, p.shape[1], SEG_WIDTH):
        cols = slice(c0, c0 + SEG_WIDTH)
        parts.append(_dot(hi[:, cols], seg) + _dot(lo[:, cols], seg))
    return jnp.concatenate(parts, axis=1) * (1.0 / SB_HEAD_DIM)


def _mix_in_kernel(x_ref, g_ref, w_ref, qg_ref, kg_ref, seg_ref, dww_ref, dwb_ref, lng_ref, lnb_ref, pw_ref, bgn_ref,
                   q_ref, k_ref, v_ref, cv_ref, u_ref, hbuf, sh_scr, *, tm, tiles_per_seq):
    i = pl.program_id(0)

    @pl.when(i == 0)
    def _():
        hbuf[...] = jnp.zeros_like(hbuf)

    h = _rms_rows(x_ref[...], g_ref[...]).astype(BF16)
    s1, s2, s3 = SB_WIDTH, 2 * SB_WIDTH, 3 * SB_WIDTH
    s4 = s3 + 2 * CONV_CH
    seg = seg_ref[...]
    pc = _wdot(h, w_ref[:, s3:s4])
    history = hbuf[tm:tm + CONV_HALO, :]
    hbuf[:CONV_HALO, :] = jnp.where(i % tiles_per_seq == 0, 0.0, history)
    hbuf[CONV_HALO:, :] = pc[:, :CONV_CH] * _sigmoid(pc[:, CONV_CH:])
    quarter = -(-CONV_WIDTH // 4)
    taps = [range(n * quarter, min((n + 1) * quarter, CONV_WIDTH)) for n in range(4)]
    pq = _wdot(h, w_ref[:, 0:s1])
    _conv_shift(hbuf, sh_scr, tm)
    acc = _conv_taps(hbuf, sh_scr, dww_ref, jnp.zeros((tm, CONV_CH), F32) + dwb_ref[...], taps[0], tm)
    q_ref[...] = (pq * lax.rsqrt(_segment_mean_sq(pq, seg) + EPS) * qg_ref[...]).astype(BF16)
    pk = _wdot(h, w_ref[:, s1:s2])
    acc = _conv_taps(hbuf, sh_scr, dww_ref, acc, taps[1], tm)
    k_ref[...] = (pk * lax.rsqrt(_segment_mean_sq(pk, seg) + EPS) * kg_ref[...]).astype(BF16)
    pv = _wdot(h, w_ref[:, s2:s3])
    acc = _conv_taps(hbuf, sh_scr, dww_ref, acc, taps[2], tm)
    v_ref[...] = pv.astype(BF16)
    pu = _wdot(h, w_ref[:, s4:])
    acc = _conv_taps(hbuf, sh_scr, dww_ref, acc, taps[3], tm)
    u_ref[...] = pu
    cv_ref[...] = _conv_tail(acc, lng_ref, lnb_ref, pw_ref, bgn_ref[:, SB_WIDTH:SB_WIDTH + CONV_CH])


def _layer_block(a, layer):
    tail = a.shape[1:]
    return pl.BlockSpec((None,) + tail, lambda *_: (layer,) + (0,) * len(tail), pipeline_mode=pl.Buffered(1))


def _mix_in(x2, seq, layer, g, w_bf, qg, kg, seg, conv, bgn, tm):
    t, d = x2.shape
    n_in = w_bf.shape[2]
    row = lambda i: (i, 0)
    hist = tm + CONV_HALO
    nbytes = (2 * (tm * d * 4 + d * n_in * 2 + tm * (3 * SB_WIDTH * 2 + CONV_CH * 2 + SSM_CH * 4)) + tm * n_in * 4
              + hist * CONV_CH * 4 * 14)
    return pl.pallas_call(
        functools.partial(_mix_in_kernel, tm=tm, tiles_per_seq=seq // tm),
        grid=(t // tm,),
        in_specs=[pl.BlockSpec((tm, d), row)]
        + [_layer_block(a, layer) for a in (g, w_bf, qg, kg)]
        + [pl.BlockSpec((SEG_WIDTH, SEG_WIDTH), lambda i: (0, 0))]
        + [_layer_block(a, layer) for a in (*conv, bgn)],
        out_specs=[pl.BlockSpec((tm, SB_WIDTH), row), pl.BlockSpec((tm, SB_WIDTH), row),
                   pl.BlockSpec((tm, SB_WIDTH), row), pl.BlockSpec((tm, CONV_CH), row),
                   pl.BlockSpec((tm, SSM_CH), row)],
        out_shape=[jax.ShapeDtypeStruct((t, SB_WIDTH), BF16), jax.ShapeDtypeStruct((t, SB_WIDTH), BF16),
                   jax.ShapeDtypeStruct((t, SB_WIDTH), BF16), jax.ShapeDtypeStruct((t, CONV_CH), BF16),
                   jax.ShapeDtypeStruct((t, SSM_CH), F32)],
        scratch_shapes=[pltpu.VMEM((hist, CONV_CH), F32),
                        pltpu.VMEM((V7X_SUBLANES - 1, hist - V7X_SUBLANES, CONV_CH), F32)],
        compiler_params=_params(("arbitrary",), nbytes),
        name="mix_in",
    )(x2, g, w_bf, qg, kg, seg, *conv, bgn)


SB_SLOTS = 2
SB_GROUPS_PER_STEP = 2
SB_QBLOCKS_PER_STEP = 2
SB_SOFTPLUS_LINEAR = 40.0
SB_EXP_UNDERFLOW = -104.0
SB_NEVER = -1e30


def _sb_decay(z, run, ntri, mask):
    sp = jnp.maximum(jnp.log(1.0 + jnp.exp(jnp.minimum(z, SB_SOFTPLUS_LINEAR))), z)
    if mask is not None:
        sp = jnp.where(mask, sp, 0.0)
    later = _dot(sp.astype(BF16), ntri)
    return z - sp + run, later, run + later[:, 0:1] - sp[:, 0:1]


def _sb_weights(t0, later, mask):
    w = jnp.exp(t0 + later)
    if mask is not None:
        w = jnp.where(mask, w, 0.0)
    return w.astype(BF16)


def _sb_attn_kernel(q_ref, k_ref, v_ref, ntri_ref, o_ref, q2_scr, z_scr, t0_scr, lat_scr, run_scr, acc_scr, *, tq):
    step = pl.program_id(2)
    nh = V7X_LANES // SB_HEAD_DIM
    m = nh * tq
    lanes = [slice(g * V7X_LANES, (g + 1) * V7X_LANES) for g in range(SB_GROUPS_PER_STEP)]
    qblocks = [step * SB_QBLOCKS_PER_STEP + s for s in range(SB_QBLOCKS_PER_STEP)]
    units = [(s, g) for s in range(SB_QBLOCKS_PER_STEP) for g in range(SB_GROUPS_PER_STEP)]

    def rows(s, j):
        i = qblocks[s]
        return pl.ds(pl.multiple_of((i - jnp.minimum(j, i)) * tq, tq), tq)

    def scores(us, j, slot):
        for s, g in us:
            z_scr[s, g, slot] = _dot_nt(q2_scr[s, g], k_ref[0, rows(s, j), lanes[g]])

    def decay(us, slot, mask=None, bias=None):
        for s, g in us:
            run = run_scr[s, g] if bias is None else run_scr[s, g] + bias[s]
            t0, later, run = _sb_decay(z_scr[s, g, slot], run, ntri_ref[...], mask)
            t0_scr[s, g, slot] = t0
            lat_scr[s, g, slot] = later
            run_scr[s, g] = run

    def output(us, j, slot, mask=None):
        for s, g in us:
            w = _sb_weights(t0_scr[s, g, slot], lat_scr[s, g, slot], mask)
            acc_scr[s, g] += _dot(w, v_ref[0, rows(s, j), lanes[g]])

    def live(us):
        run = run_scr[us[0]]
        for u in us[1:]:
            run = jnp.maximum(run, run_scr[u])
        return (jnp.max(run) >= SB_EXP_UNDERFLOW).astype(jnp.int32)

    lane_head = lax.broadcasted_iota(jnp.int32, (tq, V7X_LANES), 1) // SB_HEAD_DIM
    for s, g in units:
        q = q_ref[0, s * tq:(s + 1) * tq, lanes[g]]
        for h in range(nh):
            q2_scr[s, g, h * tq:(h + 1) * tq, :] = jnp.where(lane_head == h, q, jnp.zeros_like(q))
    row = lax.broadcasted_iota(jnp.int32, (m, tq), 0) % tq
    col = lax.broadcasted_iota(jnp.int32, (m, tq), 1)
    diag_mask = col < row
    run_scr[...] = jnp.zeros_like(run_scr)
    acc_scr[...] = jnp.zeros_like(acc_scr)

    scores(units, 0, 0)
    scores(units, 1, 1)
    decay(units, 0, mask=diag_mask)
    decay(units, 1, bias=[jnp.where(i >= 1, 0.0, SB_NEVER) for i in qblocks])
    output(units, 0, 0, diag_mask)
    output(units, 1, 1)

    for s, i in enumerate(qblocks):
        us = [u for u in units if u[0] == s]

        @pl.when(jnp.logical_and(i >= 2, live(us) > 0))
        def _(us=us, i=i):
            scores(us, 2, 0)
            decay(us, 0)
            scores(us, 3, 1)
            more = live(us)
            output(us, 2, 0)
            scores(us, 4, 0)
            decay(us, 1)
            n_pairs = (i - 2) // 2

            def cond(carry):
                p, more = carry
                return jnp.logical_and(p < n_pairs, more > 0)

            def body(carry):
                p, _ = carry
                t = 2 * p + 3
                scores(us, t + 2, 1)
                output(us, t, 1)
                decay(us, 0)
                more = live(us)
                scores(us, t + 3, 0)
                output(us, t + 1, 0)
                decay(us, 1)
                return p + 1, more

            _, more = lax.while_loop(cond, body, (0, more))

            @pl.when(jnp.logical_and(more > 0, (i - 2) % 2 == 1))
            def _():
                output(us, i, 1)

    for s, g in units:
        out = acc_scr[s, g, 0:tq, :]
        for h in range(1, nh):
            out = jnp.where(lane_head == h, acc_scr[s, g, h * tq:(h + 1) * tq, :], out)
        o_ref[0, s * tq:(s + 1) * tq, lanes[g]] = out


def _sb_attn(q, k, v, ntri, tq):
    b, l, w = q.shape
    ns, ng = SB_QBLOCKS_PER_STEP, SB_GROUPS_PER_STEP
    gw = ng * V7X_LANES
    assert l % (ns * tq) == 0 and w % gw == 0
    m = (V7X_LANES // SB_HEAD_DIM) * tq
    blk = pltpu.VMEM((ns, ng, SB_SLOTS, m, tq), F32)
    nbytes = (2 * (2 * l * gw * 2 + ns * tq * gw * 2 + ns * tq * gw * 4 + tq * tq * 2)
              + ns * ng * (3 * SB_SLOTS * m * tq * 4 + 3 * m * V7X_LANES * 4) + 6 * m * tq * 4)
    return pl.pallas_call(
        functools.partial(_sb_attn_kernel, tq=tq),
        grid=(b, w // gw, l // (ns * tq)),
        in_specs=[pl.BlockSpec((1, ns * tq, gw), lambda bb, hp, i: (bb, i, hp)),
                  pl.BlockSpec((1, l, gw), lambda bb, hp, i: (bb, 0, hp)),
                  pl.BlockSpec((1, l, gw), lambda bb, hp, i: (bb, 0, hp)),
                  pl.BlockSpec((tq, tq), lambda bb, hp, i: (0, 0))],
        out_specs=pl.BlockSpec((1, ns * tq, gw), lambda bb, hp, i: (bb, i, hp)),
        out_shape=jax.ShapeDtypeStruct((b, l, w), F32),
        scratch_shapes=[pltpu.VMEM((ns, ng, m, V7X_LANES), BF16), blk, blk, blk,
                        pltpu.VMEM((ns, ng, m, 1), F32), pltpu.VMEM((ns, ng, m, V7X_LANES), F32)],
        compiler_params=_params(("parallel", "parallel", "arbitrary"), nbytes),
        name="sb_attn",
    )(q, k, v, ntri)


def _conv_shift(hbuf, sh_scr, tc):
    span = tc + CONV_HALO - V7X_SUBLANES
    for ph in range(1, V7X_SUBLANES):
        sh_scr[ph - 1] = hbuf[ph:ph + span, :]


def _conv_taps(hbuf, sh_scr, dww_ref, acc, taps, tc):
    off = CONV_HALO - (CONV_WIDTH - 1)
    for j in taps:
        ph = (off + j) % V7X_SUBLANES
        base = off + j - ph
        tap = hbuf[base:base + tc, :] if ph == 0 else sh_scr[ph - 1, base:base + tc, :]
        acc = acc + dww_ref[j:j + 1, :] * tap
    return acc


def _conv_tail(acc, lng_ref, lnb_ref, pw_ref, bg):
    mu = jnp.mean(acc, axis=-1, keepdims=True)
    cen = acc - mu
    var = jnp.mean(cen * cen, axis=-1, keepdims=True)
    y = cen * lax.rsqrt(var + EPS) * lng_ref[...] + lnb_ref[...]
    y = y * _sigmoid(y)
    o = _wdot(y.astype(BF16), pw_ref[...])
    return _rms_rows(o, bg).astype(BF16)


def _ssm_operators(lam_re, lam_im, log_dt, b_re, b_im, c_re, c_im):
    lr, li = lam_re.astype(F32), lam_im.astype(F32)
    dt = jnp.exp(log_dt.astype(F32))[:, None]
    mag = jnp.exp(lr * dt)
    ar, ai = mag * jnp.cos(li * dt), mag * jnp.sin(li * dt)
    den = lr * lr + li * li
    fr = ((ar - 1.0) * lr + ai * li) / den
    fi = (ai * lr - (ar - 1.0) * li) / den
    br, bi = b_re.astype(F32), b_im.astype(F32)
    bbr = fr[..., None] * br - fi[..., None] * bi
    bbi = fr[..., None] * bi + fi[..., None] * br
    cr, ci = c_re.astype(F32), c_im.astype(F32)
    eye = jnp.eye(SSM_GROUPS, dtype=F32)
    gp = SSM_GROUPS * SSM_STATE

    def rows_gh(w):
        return jnp.einsum('gph,gk->ghkp', w, eye).reshape(SSM_CH, gp)

    def rows_gp(w):
        return jnp.einsum('ghp,gk->gpkh', w, eye).reshape(gp, SSM_CH)

    b_op = jnp.concatenate([rows_gh(bbr), rows_gh(bbi)], axis=1)
    c_op = jnp.concatenate([rows_gp(cr), rows_gp(-ci)], axis=0)
    n = jnp.arange(1, SSM_CHUNK + 1, dtype=F32)[:, None]
    pmag = jnp.exp(n * (lr * dt).reshape(1, gp))
    ang = n * (li * dt).reshape(1, gp)
    return b_op.astype(BF16), c_op.astype(BF16), pmag * jnp.cos(ang), pmag * jnp.sin(ang)


def _ssm_kernel(u_ref, perm_ref, permt_ref, b_ref, c_ref, pr_ref, pi_ref, d_ref, gw_ref, bgn_ref, o_ref,
                xr_scr, xi_scr, er_scr, ei_scr, cr_scr, ci_scr, *, tm):
    gp = SSM_GROUPS * SSM_STATE
    nchunk = tm // SSM_CHUNK

    @pl.when(pl.program_id(0) == 0)
    def _():
        cr_scr[...] = jnp.zeros_like(cr_scr)
        ci_scr[...] = jnp.zeros_like(ci_scr)

    def pos(s):
        return pl.ds(s * nchunk, nchunk)

    def inject(b):
        up = _dot(perm_ref[...], u_ref[b].astype(BF16)).astype(BF16)
        bu = _dot(up, b_ref[...])
        xr_scr[b] = bu[:, :gp]
        xi_scr[b] = bu[:, gp:]

    def scan(b):
        ar, ai = pr_ref[0:1, :], pi_ref[0:1, :]
        xr, xi = xr_scr[b, pos(0), :], xi_scr[b, pos(0), :]
        for s in range(1, SSM_CHUNK):
            xr, xi = (ar * xr - ai * xi + xr_scr[b, pos(s), :], ar * xi + ai * xr + xi_scr[b, pos(s), :])
            xr_scr[b, pos(s), :] = xr
            xi_scr[b, pos(s), :] = xi
        nr, ni = pr_ref[SSM_CHUNK - 1:SSM_CHUNK, :], pi_ref[SSM_CHUNK - 1:SSM_CHUNK, :]
        er, ei = cr_scr[b], ci_scr[b]
        for c in range(nchunk):
            er_scr[b, c:c + 1, :] = er
            ei_scr[b, c:c + 1, :] = ei
            er, ei = (nr * er - ni * ei + xr[c:c + 1, :], nr * ei + ni * er + xi[c:c + 1, :])
        cr_scr[b] = er
        ci_scr[b] = ei
        er, ei = er_scr[b], ei_scr[b]
        for s in range(SSM_CHUNK):
            sr, si = pr_ref[s:s + 1, :], pi_ref[s:s + 1, :]
            xr_scr[b, pos(s), :] += sr * er - si * ei
            xi_scr[b, pos(s), :] += sr * ei + si * er

    def readout(b):
        yp = _dot(xr_scr[b].astype(BF16), c_ref[0:gp, :]) + _dot(xi_scr[b].astype(BF16), c_ref[gp:, :])
        hi = yp.astype(BF16)
        lo = (yp - hi.astype(F32)).astype(BF16)
        y = _dot(permt_ref[...], hi) + _dot(permt_ref[...], lo) + d_ref[...] * u_ref[b]
        z = _wdot(y.astype(BF16), gw_ref[...])
        bg = bgn_ref[:, SB_WIDTH + CONV_CH:]
        o_ref[b] = _rms_rows(z[:, :SSM_CH] * _sigmoid(z[:, SSM_CH:]), bg).astype(BF16)

    nb = u_ref.shape[0]
    for b in range(nb):
        inject(b)
    for b in range(nb):
        scan(b)
        readout(b)


def _ssm_branch(u, layer, ops, d, gw_bf, bgn, tm):
    b_op, c_op, pow_r, pow_i = ops
    b, l, ch = u.shape
    gp = SSM_GROUPS * SSM_STATE
    nchunk = tm // SSM_CHUNK
    full = lambda a: pl.BlockSpec(a.shape, lambda i: (0,) * a.ndim)
    r = jnp.arange(tm)
    perm = (r[None, :] == ((r % nchunk) * SSM_CHUNK + r // nchunk)[:, None]).astype(BF16)
    perm_t = perm.T
    nbytes = (2 * (2 * b * tm * ch * 4 + 2 * tm * tm * 2 + 2 * ch * gp * 2 * 2 + 2 * SSM_CHUNK * gp * 4 + ch * 2 * ch * 2)
              + b * (2 * tm * gp * 4 + 2 * nchunk * gp * 4 + 2 * tm * 2 * gp * 4))
    return pl.pallas_call(
        functools.partial(_ssm_kernel, tm=tm),
        grid=(l // tm,),
        in_specs=[pl.BlockSpec((b, tm, ch), lambda i: (0, i, 0)), full(perm), full(perm_t)]
        + [_layer_block(a, layer) for a in (b_op, c_op, pow_r, pow_i, d, gw_bf, bgn)],
        out_specs=pl.BlockSpec((b, tm, ch), lambda i: (0, i, 0)),
        out_shape=jax.ShapeDtypeStruct((b, l, ch), BF16),
        scratch_shapes=[pltpu.VMEM((b, tm, gp), F32), pltpu.VMEM((b, tm, gp), F32),
                        pltpu.VMEM((b, nchunk, gp), F32), pltpu.VMEM((b, nchunk, gp), F32),
                        pltpu.VMEM((b, 1, gp), F32), pltpu.VMEM((b, 1, gp), F32)],
        compiler_params=_params(("arbitrary",), nbytes),
        name="ssm_branch",
    )(u, perm, perm_t, b_op, c_op, pow_r, pow_i, d, gw_bf, bgn)


def _mix_out(x, sb, cv, sm, bg, w_ref):
    s1 = SB_WIDTH
    s2 = SB_WIDTH + CONV_CH
    y = _wdot(_rms_rows(sb, bg).astype(BF16), w_ref[0:s1, :])
    y = y + _wdot(cv, w_ref[s1:s2, :])
    y = y + _wdot(sm, w_ref[s2:, :])
    return x + y


def _mem_kv_kernel(m_ref, g_ref, wk_ref, wv_ref, kg_ref, k_ref, v_ref):
    hm = _rms_rows(m_ref[0], g_ref[...]).astype(BF16)
    kk = _wdot(hm, wk_ref[...])
    for hh in range(XA_HEADS):
        hs = slice(hh * XA_HEAD_DIM, (hh + 1) * XA_HEAD_DIM)
        k_ref[0, :, hs] = _rms_rows(kk[:, hs], kg_ref[...]).astype(BF16)
    v_ref[0] = _wdot(hm, wv_ref[...]).astype(BF16)


def _mem_kv(mem, layer, g, wk_bf, wv_bf, kg):
    b, n, d = mem.shape
    blk = pl.BlockSpec((1, n, d), lambda bb: (bb, 0, 0))
    nbytes = 2 * (n * d * 4 + 2 * d * d * 2 + 2 * n * d * 2) + 4 * n * d * 4
    return pl.pallas_call(
        _mem_kv_kernel,
        grid=(b,),
        in_specs=[blk] + [_layer_block(a, layer) for a in (g, wk_bf, wv_bf, kg)],
        out_specs=[blk, blk],
        out_shape=[jax.ShapeDtypeStruct((b, n, d), BF16), jax.ShapeDtypeStruct((b, n, d), BF16)],
        compiler_params=_params(("parallel",), nbytes),
        name="mem_kv",
    )(mem, g, wk_bf, wv_bf, kg)


def _xattn_kernel(x_ref, sb_ref, cv_ref, sm_ref, bgn_ref, wm_ref, g_ref, wq_ref, qg_ref, wo_ref, k_ref, v_ref,
                  o_ref, ob_scr):
    x = _mix_out(x_ref[0], sb_ref[0], cv_ref[0], sm_ref[0], bgn_ref[:, :SB_WIDTH], wm_ref)
    hx = _rms_rows(x, g_ref[...]).astype(BF16)
    q = _wdot(hx, wq_ref[...])
    scale = XA_HEAD_DIM ** -0.5
    for hh in range(XA_HEADS):
        hs = slice(hh * XA_HEAD_DIM, (hh + 1) * XA_HEAD_DIM)
        qh = (_rms_rows(q[:, hs], qg_ref[...]) * scale).astype(BF16)
        s = _dot_nt(qh, k_ref[0, :, hs])
        s = s - jnp.max(s, axis=-1, keepdims=True)
        e = jnp.exp(s)
        p = e / jnp.sum(e, axis=-1, keepdims=True)
        ob_scr[:, hs] = _dot(p.astype(BF16), v_ref[0, :, hs]).astype(BF16)
    o_ref[0] = x + _wdot(ob_scr[...], wo_ref[...])


def _mix_xattn(x, sb, cv, sm, layer, bgn, wm_bf, g, wq_bf, qg, wo_bf, k_bf, v_bf, tm):
    b, l, d = x.shape
    n = k_bf.shape[1]
    rows = lambda w: pl.BlockSpec((1, tm, w), lambda bb, i: (bb, i, 0))
    kv = pl.BlockSpec((1, n, d), lambda bb, i: (bb, 0, 0))
    nbytes = (2 * (2 * tm * d * 4 + tm * SB_WIDTH * 4 + tm * (CONV_CH + SSM_CH) * 2 + 3 * d * d * 2 + 2 * n * d * 2)
              + 5 * tm * d * 4)
    return pl.pallas_call(
        _xattn_kernel,
        grid=(b, l // tm),
        in_specs=[rows(d), rows(SB_WIDTH), rows(CONV_CH), rows(SSM_CH)]
        + [_layer_block(a, layer) for a in (bgn, wm_bf, g, wq_bf, qg, wo_bf)] + [kv, kv],
        out_specs=rows(d),
        out_shape=jax.ShapeDtypeStruct((b, l, d), F32),
        scratch_shapes=[pltpu.VMEM((tm, d), BF16)],
        compiler_params=_params(("parallel", "parallel"), nbytes),
        name="mix_xattn",
    )(x, sb, cv, sm, bgn, wm_bf, g, wq_bf, qg, wo_bf, k_bf, v_bf)


def _ffn_kernel(x_ref, g_ref, wi_ref, wo_ref, o_ref, *, th):
    hidden = wo_ref.shape[0]
    x = x_ref[...]
    h = _rms_rows(x, g_ref[...]).astype(BF16)
    o_ref[...] = x
    for c0 in range(0, hidden, th):
        c1 = min(c0 + th, hidden)
        gate = _wdot(h, wi_ref[:, c0:c1])
        up = _wdot(h, wi_ref[:, hidden + c0:hidden + c1])
        act = (gate * _sigmoid(gate) * up).astype(BF16)
        o_ref[...] += _wdot(act, wo_ref[c0:c1, :])


def _ffn(x2, layer, g, w_in, w_out, tm, th):
    t, d = x2.shape
    hidden = w_out.shape[1]
    nbytes = 3 * d * hidden * 4 + 2 * 2 * tm * d * 4 + tm * d * 2 + 6 * tm * th * 4
    return pl.pallas_call(
        functools.partial(_ffn_kernel, th=th),
        grid=(t // tm,),
        in_specs=[pl.BlockSpec((tm, d), lambda i: (i, 0))] + [_layer_block(a, layer) for a in (g, w_in, w_out)],
        out_specs=pl.BlockSpec((tm, d), lambda i: (i, 0)),
        out_shape=jax.ShapeDtypeStruct((t, d), F32),
        compiler_params=_params(("parallel",), nbytes),
        name="ffn",
    )(x2, g, w_in, w_out)


def _tile(n, want):
    want = min(want, n)
    for cand in range(want, 0, -1):
        if n % cand == 0 and (cand % V7X_SUBLANES == 0 or cand == n):
            return cand
    return n


def kernel(x, mem, norm_mix_g, w_in, sb_q_norm_g, sb_k_norm_g, conv_dw_w, conv_dw_b, conv_ln_g, conv_ln_b, conv_pw2_w, ssm_lam_re, ssm_lam_im, ssm_log_dt, ssm_b_re, ssm_b_im, ssm_c_re, ssm_c_im, ssm_d, ssm_glu_w, branch_norm_g, w_out, norm_xa_g, norm_mem_g, xa_wq, xa_wk, xa_wv, xa_q_norm_g, xa_k_norm_g, xa_wo, norm_ffn_g, ffn_w_in, ffn_w_out):
    bsz, seq, d = x.shape
    depth = w_in.shape[0]
    t = bsz * seq
    tm = _tile(seq, 512)
    tx = _tile(seq, 1024)
    tq = _tile(seq, 256)
    ts = _tile(seq, 512)
    th = 512
    assert ts % (SSM_CHUNK * V7X_SUBLANES) == 0 and tm >= CONV_HALO
    vec = lambda a: a.astype(F32).reshape(depth, 1, -1)
    qg = vec(jnp.tile(sb_q_norm_g, (1, SB_HEADS))) * (SB_HEAD_DIM ** -0.5)
    kg = vec(jnp.tile(sb_k_norm_g, (1, SB_HEADS)))
    conv = (conv_dw_w.astype(F32), vec(conv_dw_b), vec(conv_ln_g), vec(conv_ln_b), conv_pw2_w.astype(F32))
    ops = jax.vmap(_ssm_operators)(ssm_lam_re, ssm_lam_im, ssm_log_dt, ssm_b_re, ssm_b_im, ssm_c_re, ssm_c_im)
    bgn = vec(branch_norm_g)

    seg = (jnp.arange(SEG_WIDTH)[:, None] // SB_HEAD_DIM == jnp.arange(SEG_WIDTH)[None, :] // SB_HEAD_DIM).astype(BF16)
    ntri = -(jnp.arange(tq)[:, None] > jnp.arange(tq)[None, :]).astype(BF16)

    x2 = x.reshape(t, d)
    for l in range(depth):
        q, k, v, o_conv, u = _mix_in(x2, seq, l, vec(norm_mix_g), w_in.astype(F32), qg, kg, seg, conv, bgn, tx)
        o_conv = o_conv.reshape(bsz, seq, CONV_CH)
        o_sb = _sb_attn(q.reshape(bsz, seq, SB_WIDTH), k.reshape(bsz, seq, SB_WIDTH),
                        v.reshape(bsz, seq, SB_WIDTH), ntri, tq)
        o_ssm = _ssm_branch(u.reshape(bsz, seq, SSM_CH), l, ops, vec(ssm_d), ssm_glu_w.astype(F32), bgn, ts)
        k_m, v_m = _mem_kv(mem, l, vec(norm_mem_g), xa_wk.astype(F32), xa_wv.astype(F32), vec(xa_k_norm_g))
        x2 = _mix_xattn(x2.reshape(bsz, seq, d), o_sb, o_conv, o_ssm, l, bgn, w_out.astype(F32), vec(norm_xa_g),
                        xa_wq.astype(F32), vec(xa_q_norm_g), xa_wo.astype(F32), k_m, v_m, tx).reshape(t, d)
        x2 = _ffn(x2, l, vec(norm_ffn_g), ffn_w_in.astype(F32), ffn_w_out.astype(F32), tm, th)
    return x2.reshape(bsz, seq, d)
```

```python
import functools

import jax
import jax.numpy as jnp
from jax import lax
from jax.experimental import pallas as pl
from jax.experimental.pallas import tpu as pltpu

F32 = jnp.float32
BF16 = jnp.bfloat16
EPS = 1e-6

V7X_LANES = 128
V7X_SUBLANES = 8
V7X_VMEM_BYTES = 64 * 1024 * 1024

SB_HEADS = 8
SB_HEAD_DIM = 64
SB_WIDTH = SB_HEADS * SB_HEAD_DIM
SEG_WIDTH = 256
CONV_CH = 256
CONV_WIDTH = 31
CONV_HALO = 32
SSM_CH = 256
SSM_GROUP = 16
SSM_GROUPS = SSM_CH // SSM_GROUP
SSM_STATE = 64
SSM_CHUNK = 32
XA_HEADS = 4
XA_HEAD_DIM = 256


def _vmem_limit(nbytes):
    return int(min(max(nbytes * 3 // 2, 16 * 1024 * 1024), V7X_VMEM_BYTES - 8 * 1024 * 1024))


def _params(semantics, nbytes):
    return pltpu.CompilerParams(dimension_semantics=semantics, vmem_limit_bytes=_vmem_limit(nbytes))


def _dot(a, b):
    return jnp.dot(a, b, preferred_element_type=F32)


def _wdot(a, w):
    return jnp.dot(a, w.astype(BF16), preferred_element_type=F32)


def _dot_nt(a, b):
    return lax.dot_general(a, b, (((1,), (1,)), ((), ())), preferred_element_type=F32)


def _rms_rows(xf, g):
    return xf * lax.rsqrt(jnp.mean(xf * xf, axis=-1, keepdims=True) + EPS) * g


def _sigmoid(x):
    return 1.0 / (1.0 + jnp.exp(-x))


def _segment_mean_sq(p, seg):
    sq = p * p
    hi = sq.astype(BF16)
    lo = (sq - hi.astype(F32)).astype(BF16)
    parts = []
    for c0 in range(0, p.shape[1], SEG_WIDTH):
        cols = slice(c0, c0 + SEG_WIDTH)
        parts.append(_dot(hi[:, cols], seg) + _dot(lo[:, cols], seg))
    return jnp.concatenate(parts, axis=1) * (1.0 / SB_HEAD_DIM)


def _mix_in_kernel(x_ref, g_ref, w_ref, qg_ref, kg_ref, seg_ref, dww_ref, dwb_ref, lng_ref, lnb_ref, pw_ref, bgn_ref,
                   q_ref, k_ref, v_ref, cv_ref, u_ref, hbuf, sh_scr, *, tm, tiles_per_seq):
    i = pl.program_id(0)

    @pl.when(i == 0)
    def _():
        hbuf[...] = jnp.zeros_like(hbuf)

    h = _rms_rows(x_ref[...], g_ref[...]).astype(BF16)
    s1, s2, s3 = SB_WIDTH, 2 * SB_WIDTH, 3 * SB_WIDTH
    s4 = s3 + 2 * CONV_CH
    seg = seg_ref[...]
    pc = _wdot(h, w_ref[:, s3:s4])
    history = hbuf[tm:tm + CONV_HALO, :]
    hbuf[:CONV_HALO, :] = jnp.where(i % tiles_per_seq == 0, 0.0, history)
    hbuf[CONV_HALO:, :] = pc[:, :CONV_CH] * _sigmoid(pc[:, CONV_CH:])
    quarter = -(-CONV_WIDTH // 4)
    taps = [range(n * quarter, min((n + 1) * quarter, CONV_WIDTH)) for n in range(4)]
    pq = _wdot(h, w_ref[:, 0:s1])
    _conv_shift(hbuf, sh_scr, tm)
    acc = _conv_taps(hbuf, sh_scr, dww_ref, jnp.zeros((tm, CONV_CH), F32) + dwb_ref[...], taps[0], tm)
    q_ref[...] = (pq * lax.rsqrt(_segment_mean_sq(pq, seg) + EPS) * qg_ref[...]).astype(BF16)
    pk = _wdot(h, w_ref[:, s1:s2])
    acc = _conv_taps(hbuf, sh_scr, dww_ref, acc, taps[1], tm)
    k_ref[...] = (pk * lax.rsqrt(_segment_mean_sq(pk, seg) + EPS) * kg_ref[...]).astype(BF16)
    pv = _wdot(h, w_ref[:, s2:s3])
    acc = _conv_taps(hbuf, sh_scr, dww_ref, acc, taps[2], tm)
    v_ref[...] = pv.astype(BF16)
    pu = _wdot(h, w_ref[:, s4:])
    acc = _conv_taps(hbuf, sh_scr, dww_ref, acc, taps[3], tm)
    u_ref[...] = pu
    cv_ref[...] = _conv_tail(acc, lng_ref, lnb_ref, pw_ref, bgn_ref[:, SB_WIDTH:SB_WIDTH + CONV_CH])


def _layer_block(a, layer):
    tail = a.shape[1:]
    return pl.BlockSpec((None,) + tail, lambda *_: (layer,) + (0,) * len(tail), pipeline_mode=pl.Buffered(1))


def _mix_in(x2, seq, layer, g, w_bf, qg, kg, seg, conv, bgn, tm):
    t, d = x2.shape
    n_in = w_bf.shape[2]
    row = lambda i: (i, 0)
    hist = tm + CONV_HALO
    nbytes = (2 * (tm * d * 4 + d * n_in * 2 + tm * (3 * SB_WIDTH * 2 + CONV_CH * 2 + SSM_CH * 4)) + tm * n_in * 4
              + hist * CONV_CH * 4 * 14)
    return pl.pallas_call(
        functools.partial(_mix_in_kernel, tm=tm, tiles_per_seq=seq // tm),
        grid=(t // tm,),
        in_specs=[pl.BlockSpec((tm, d), row)]
        + [_layer_block(a, layer) for a in (g, w_bf, qg, kg)]
        + [pl.BlockSpec((SEG_WIDTH, SEG_WIDTH), lambda i: (0, 0))]
        + [_layer_block(a, layer) for a in (*conv, bgn)],
        out_specs=[pl.BlockSpec((tm, SB_WIDTH), row), pl.BlockSpec((tm, SB_WIDTH), row),
                   pl.BlockSpec((tm, SB_WIDTH), row), pl.BlockSpec((tm, CONV_CH), row),
                   pl.BlockSpec((tm, SSM_CH), row)],
        out_shape=[jax.ShapeDtypeStruct((t, SB_WIDTH), BF16), jax.ShapeDtypeStruct((t, SB_WIDTH), BF16),
                   jax.ShapeDtypeStruct((t, SB_WIDTH), BF16), jax.ShapeDtypeStruct((t, CONV_CH), BF16),
                   jax.ShapeDtypeStruct((t, SSM_CH), F32)],
        scratch_shapes=[pltpu.VMEM((hist, CONV_CH), F32),
                        pltpu.VMEM((V7X_SUBLANES - 1, hist - V7X_SUBLANES, CONV_CH), F32)],
        compiler_params=_params(("arbitrary",), nbytes),
        name="mix_in",
    )(x2, g, w_bf, qg, kg, seg, *conv, bgn)


SB_SLOTS = 2
SB_GROUPS_PER_STEP = 2
SB_QBLOCKS_PER_STEP = 2
SB_SOFTPLUS_LINEAR = 40.0
SB_EXP_UNDERFLOW = -104.0
SB_NEVER = -1e30


def _sb_decay(z, run, ntri, mask):
    sp = jnp.maximum(jnp.log(1.0 + jnp.exp(jnp.minimum(z, SB_SOFTPLUS_LINEAR))), z)
    if mask is not None:
        sp = jnp.where(mask, sp, 0.0)
    later = _dot(sp.astype(BF16), ntri)
    return z - sp + run, later, run + later[:, 0:1] - sp[:, 0:1]


def _sb_weights(t0, later, mask):
    w = jnp.exp(t0 + later)
    if mask is not None:
        w = jnp.where(mask, w, 0.0)
    return w.astype(BF16)


def _sb_attn_kernel(q_ref, k_ref, v_ref, ntri_ref, o_ref, q2_scr, z_scr, t0_scr, lat_scr, run_scr, acc_scr, *, tq):
    step = pl.program_id(2)
    nh = V7X_LANES // SB_HEAD_DIM
    m = nh * tq
    lanes = [slice(g * V7X_LANES, (g + 1) * V7X_LANES) for g in range(SB_GROUPS_PER_STEP)]
    qblocks = [step * SB_QBLOCKS_PER_STEP + s for s in range(SB_QBLOCKS_PER_STEP)]
    units = [(s, g) for s in range(SB_QBLOCKS_PER_STEP) for g in range(SB_GROUPS_PER_STEP)]

    def rows(s, j):
        i = qblocks[s]
        return pl.ds(pl.multiple_of((i - jnp.minimum(j, i)) * tq, tq), tq)

    def scores(us, j, slot):
        for s, g in us:
            z_scr[s, g, slot] = _dot_nt(q2_scr[s, g], k_ref[0, rows(s, j), lanes[g]])

    def decay(us, slot, mask=None, bias=None):
        for s, g in us:
            run = run_scr[s, g] if bias is None else run_scr[s, g] + bias[s]
            t0, later, run = _sb_decay(z_scr[s, g, slot], run, ntri_ref[...], mask)
            t0_scr[s, g, slot] = t0
            lat_scr[s, g, slot] = later
            run_scr[s, g] = run

    def output(us, j, slot, mask=None):
        for s, g in us:
            w = _sb_weights(t0_scr[s, g, slot], lat_scr[s, g, slot], mask)
            acc_scr[s, g] += _dot(w, v_ref[0, rows(s, j), lanes[g]])

    def live(us):
        run = run_scr[us[0]]
        for u in us[1:]:
            run = jnp.maximum(run, run_scr[u])
        return (jnp.max(run) >= SB_EXP_UNDERFLOW).astype(jnp.int32)

    lane_head = lax.broadcasted_iota(jnp.int32, (tq, V7X_LANES), 1) // SB_HEAD_DIM
    for s, g in units:
        q = q_ref[0, s * tq:(s + 1) * tq, lanes[g]]
        for h in range(nh):
            q2_scr[s, g, h * tq:(h + 1) * tq, :] = jnp.where(lane_head == h, q, jnp.zeros_like(q))
    row = lax.broadcasted_iota(jnp.int32, (m, tq), 0) % tq
    col = lax.broadcasted_iota(jnp.int32, (m, tq), 1)
    diag_mask = col < row
    run_scr[...] = jnp.zeros_like(run_scr)
    acc_scr[...] = jnp.zeros_like(acc_scr)

    scores(units, 0, 0)
    scores(units, 1, 1)
    decay(units, 0, mask=diag_mask)
    decay(units, 1, bias=[jnp.where(i >= 1, 0.0, SB_NEVER) for i in qblocks])
    output(units, 0, 0, diag_mask)
    output(units, 1, 1)

    for s, i in enumerate(qblocks):
        us = [u for u in units if u[0] == s]

        @pl.when(jnp.logical_and(i >= 2, live(us) > 0))
        def _(us=us, i=i):
            scores(us, 2, 0)
            decay(us, 0)
            scores(us, 3, 1)
            more = live(us)
            output(us, 2, 0)
            scores(us, 4, 0)
            decay(us, 1)
            n_pairs = (i - 2) // 2

            def cond(carry):
                p, more = carry
                return jnp.logical_and(p < n_pairs, more > 0)

            def body(carry):
                p, _ = carry
                t = 2 * p + 3
                scores(us, t + 2, 1)
                output(us, t, 1)
                decay(us, 0)
                more = live(us)
                scores(us, t + 3, 0)
                output(us, t + 1, 0)
                decay(us, 1)
                return p + 1, more

            _, more = lax.while_loop(cond, body, (0, more))

            @pl.when(jnp.logical_and(more > 0, (i - 2) % 2 == 1))
            def _():
                output(us, i, 1)

    for s, g in units:
        out = acc_scr[s, g, 0:tq, :]
        for h in range(1, nh):
            out = jnp.where(lane_head == h, acc_scr[s, g, h * tq:(h + 1) * tq, :], out)
        o_ref[0, s * tq:(s + 1) * tq, lanes[g]] = out


def _sb_attn(q, k, v, ntri, tq):
    b, l, w = q.shape
    ns, ng = SB_QBLOCKS_PER_STEP, SB_GROUPS_PER_STEP
    gw = ng * V7X_LANES
    assert l % (ns * tq) == 0 and w % gw == 0
    m = (V7X_LANES // SB_HEAD_DIM) * tq
    blk = pltpu.VMEM((ns, ng, SB_SLOTS, m, tq), F32)
    nbytes = (2 * (2 * l * gw * 2 + ns * tq * gw * 2 + ns * tq * gw * 4 + tq * tq * 2)
              + ns * ng * (3 * SB_SLOTS * m * tq * 4 + 3 * m * V7X_LANES * 4) + 6 * m * tq * 4)
    return pl.pallas_call(
        functools.partial(_sb_attn_kernel, tq=tq),
        grid=(b, w // gw, l // (ns * tq)),
        in_specs=[pl.BlockSpec((1, ns * tq, gw), lambda bb, hp, i: (bb, i, hp)),
                  pl.BlockSpec((1, l, gw), lambda bb, hp, i: (bb, 0, hp)),
                  pl.BlockSpec((1, l, gw), lambda bb, hp, i: (bb, 0, hp)),
                  pl.BlockSpec((tq, tq), lambda bb, hp, i: (0, 0))],
        out_specs=pl.BlockSpec((1, ns * tq, gw), lambda bb, hp, i: (bb, i, hp)),
        out_shape=jax.ShapeDtypeStruct((b, l, w), F32),
        scratch_shapes=[pltpu.VMEM((ns, ng, m, V7X_LANES), BF16), blk, blk, blk,
                        pltpu.VMEM((ns, ng, m, 1), F32), pltpu.VMEM((ns, ng, m, V7X_LANES), F32)],
        compiler_params=_params(("parallel", "parallel", "arbitrary"), nbytes),
        name="sb_attn",
    )(q, k, v, ntri)


def _conv_shift(hbuf, sh_scr, tc):
    span = tc + CONV_HALO - V7X_SUBLANES
    for ph in range(1, V7X_SUBLANES):
        sh_scr[ph - 1] = hbuf[ph:ph + span, :]


def _conv_taps(hbuf, sh_scr, dww_ref, acc, taps, tc):
    off = CONV_HALO - (CONV_WIDTH - 1)
    for j in taps:
        ph = (off + j) % V7X_SUBLANES
        base = off + j - ph
        tap = hbuf[base:base + tc, :] if ph == 0 else sh_scr[ph - 1, base:base + tc, :]
        acc = acc + dww_ref[j:j + 1, :] * tap
    return acc


def _conv_tail(acc, lng_ref, lnb_ref, pw_ref, bg):
    mu = jnp.mean(acc, axis=-1, keepdims=True)
    cen = acc - mu
    var = jnp.mean(cen * cen, axis=-1, keepdims=True)
    y = cen * lax.rsqrt(var + EPS) * lng_ref[...] + lnb_ref[...]
    y = y * _sigmoid(y)
    o = _wdot(y.astype(BF16), pw_ref[...])
    return _rms_rows(o, bg).astype(BF16)


def _ssm_operators(lam_re, lam_im, log_dt, b_re, b_im, c_re, c_im):
    lr, li = lam_re.astype(F32), lam_im.astype(F32)
    dt = jnp.exp(log_dt.astype(F32))[:, None]
    mag = jnp.exp(lr * dt)
    ar, ai = mag * jnp.cos(li * dt), mag * jnp.sin(li * dt)
    den = lr * lr + li * li
    fr = ((ar - 1.0) * lr + ai * li) / den
    fi = (ai * lr - (ar - 1.0) * li) / den
    br, bi = b_re.astype(F32), b_im.astype(F32)
    bbr = fr[..., None] * br - fi[..., None] * bi
    bbi = fr[..., None] * bi + fi[..., None] * br
    cr, ci = c_re.astype(F32), c_im.astype(F32)
    eye = jnp.eye(SSM_GROUPS, dtype=F32)
    gp = SSM_GROUPS * SSM_STATE

    def rows_gh(w):
        return jnp.einsum('gph,gk->ghkp', w, eye).reshape(SSM_CH, gp)

    def rows_gp(w):
        return jnp.einsum('ghp,gk->gpkh', w, eye).reshape(gp, SSM_CH)

    b_op = jnp.concatenate([rows_gh(bbr), rows_gh(bbi)], axis=1)
    c_op = jnp.concatenate([rows_gp(cr), rows_gp(-ci)], axis=0)
    n = jnp.arange(1, SSM_CHUNK + 1, dtype=F32)[:, None]
    pmag = jnp.exp(n * (lr * dt).reshape(1, gp))
    ang = n * (li * dt).reshape(1, gp)
    return b_op.astype(BF16), c_op.astype(BF16), pmag * jnp.cos(ang), pmag * jnp.sin(ang)


def _ssm_kernel(u_ref, perm_ref, permt_ref, b_ref, c_ref, pr_ref, pi_ref, d_ref, gw_ref, bgn_ref, o_ref,
                xr_scr, xi_scr, er_scr, ei_scr, cr_scr, ci_scr, *, tm):
    gp = SSM_GROUPS * SSM_STATE
    nchunk = tm // SSM_CHUNK

    @pl.when(pl.program_id(0) == 0)
    def _():
        cr_scr[...] = jnp.zeros_like(cr_scr)
        ci_scr[...] = jnp.zeros_like(ci_scr)

    def pos(s):
        return pl.ds(s * nchunk, nchunk)

    def inject(b):
        up = _dot(perm_ref[...], u_ref[b].astype(BF16)).astype(BF16)
        bu = _dot(up, b_ref[...])
        xr_scr[b] = bu[:, :gp]
        xi_scr[b] = bu[:, gp:]

    def scan(b):
        ar, ai = pr_ref[0:1, :], pi_ref[0:1, :]
        xr, xi = xr_scr[b, pos(0), :], xi_scr[b, pos(0), :]
        for s in range(1, SSM_CHUNK):
            xr, xi = (ar * xr - ai * xi + xr_scr[b, pos(s), :], ar * xi + ai * xr + xi_scr[b, pos(s), :])
            xr_scr[b, pos(s), :] = xr
            xi_scr[b, pos(s), :] = xi
        nr, ni = pr_ref[SSM_CHUNK - 1:SSM_CHUNK, :], pi_ref[SSM_CHUNK - 1:SSM_CHUNK, :]
        er, ei = cr_scr[b], ci_scr[b]
        for c in range(nchunk):
            er_scr[b, c:c + 1, :] = er
            ei_scr[b, c:c + 1, :] = ei
            er, ei = (nr * er - ni * ei + xr[c:c + 1, :], nr * ei + ni * er + xi[c:c + 1, :])
        cr_scr[b] = er
        ci_scr[b] = ei
        er, ei = er_scr[b], ei_scr[b]
        for s in range(SSM_CHUNK):
            sr, si = pr_ref[s:s + 1, :], pi_ref[s:s + 1, :]
            xr_scr[b, pos(s), :] += sr * er - si * ei
            xi_scr[b, pos(s), :] += sr * ei + si * er

    def readout(b):
        yp = _dot(xr_scr[b].astype(BF16), c_ref[0:gp, :]) + _dot(xi_scr[b].astype(BF16), c_ref[gp:, :])
        hi = yp.astype(BF16)
        lo = (yp - hi.astype(F32)).astype(BF16)
        y = _dot(permt_ref[...], hi) + _dot(permt_ref[...], lo) + d_ref[...] * u_ref[b]
        z = _wdot(y.astype(BF16), gw_ref[...])
        bg = bgn_ref[:, SB_WIDTH + CONV_CH:]
        o_ref[b] = _rms_rows(z[:, :SSM_CH] * _sigmoid(z[:, SSM_CH:]), bg).astype(BF16)

    nb = u_ref.shape[0]
    for b in range(nb):
        inject(b)
    for b in range(nb):
        scan(b)
        readout(b)


def _ssm_branch(u, layer, ops, d, gw_bf, bgn, tm):
    b_op, c_op, pow_r, pow_i = ops
    b, l, ch = u.shape
    gp = SSM_GROUPS * SSM_STATE
    nchunk = tm // SSM_CHUNK
    full = lambda a: pl.BlockSpec(a.shape, lambda i: (0,) * a.ndim)
    r = jnp.arange(tm)
    perm = (r[None, :] == ((r % nchunk) * SSM_CHUNK + r // nchunk)[:, None]).astype(BF16)
    perm_t = perm.T
    nbytes = (2 * (2 * b * tm * ch * 4 + 2 * tm * tm * 2 + 2 * ch * gp * 2 * 2 + 2 * SSM_CHUNK * gp * 4 + ch * 2 * ch * 2)
              + b * (2 * tm * gp * 4 + 2 * nchunk * gp * 4 + 2 * tm * 2 * gp * 4))
    return pl.pallas_call(
        functools.partial(_ssm_kernel, tm=tm),
        grid=(l // tm,),
        in_specs=[pl.BlockSpec((b, tm, ch), lambda i: (0, i, 0)), full(perm), full(perm_t)]
        + [_layer_block(a, layer) for a in (b_op, c_op, pow_r, pow_i, d, gw_bf, bgn)],
        out_specs=pl.BlockSpec((b, tm, ch), lambda i: (0, i, 0)),
        out_shape=jax.ShapeDtypeStruct((b, l, ch), BF16),
        scratch_shapes=[pltpu.VMEM((b, tm, gp), F32), pltpu.VMEM((b, tm, gp), F32),
                        pltpu.VMEM((b, nchunk, gp), F32), pltpu.VMEM((b, nchunk, gp), F32),
                        pltpu.VMEM((b, 1, gp), F32), pltpu.VMEM((b, 1, gp), F32)],
        compiler_params=_params(("arbitrary",), nbytes),
        name="ssm_branch",
    )(u, perm, perm_t, b_op, c_op, pow_r, pow_i, d, gw_bf, bgn)


def _mix_out(x, sb, cv, sm, bg, w_ref):
    s1 = SB_WIDTH
    s2 = SB_WIDTH + CONV_CH
    y = _wdot(_rms_rows(sb, bg).astype(BF16), w_ref[0:s1, :])
    y = y + _wdot(cv, w_ref[s1:s2, :])
    y = y + _wdot(sm, w_ref[s2:, :])
    return x + y


def _mem_kv_kernel(m_ref, g_ref, wk_ref, wv_ref, kg_ref, k_ref, v_ref):
    hm = _rms_rows(m_ref[0], g_ref[...]).astype(BF16)
    kk = _wdot(hm, wk_ref[...])
    for hh in range(XA_HEADS):
        hs = slice(hh * XA_HEAD_DIM, (hh + 1) * XA_HEAD_DIM)
        k_ref[0, :, hs] = _rms_rows(kk[:, hs], kg_ref[...]).astype(BF16)
    v_ref[0] = _wdot(hm, wv_ref[...]).astype(BF16)


def _mem_kv(mem, layer, g, wk_bf, wv_bf, kg):
    b, n, d = mem.shape
    blk = pl.BlockSpec((1, n, d), lambda bb: (bb, 0, 0))
    nbytes = 2 * (n * d * 4 + 2 * d * d * 2 + 2 * n * d * 2) + 4 * n * d * 4
    return pl.pallas_call(
        _mem_kv_kernel,
        grid=(b,),
        in_specs=[blk] + [_layer_block(a, layer) for a in (g, wk_bf, wv_bf, kg)],
        out_specs=[blk, blk],
        out_shape=[jax.ShapeDtypeStruct((b, n, d), BF16), jax.ShapeDtypeStruct((b, n, d), BF16)],
        compiler_params=_params(("parallel",), nbytes),
        name="mem_kv",
    )(mem, g, wk_bf, wv_bf, kg)


def _xattn_kernel(x_ref, sb_ref, cv_ref, sm_ref, bgn_ref, wm_ref, g_ref, wq_ref, qg_ref, wo_ref, k_ref, v_ref,
                  o_ref, ob_scr):
    x = _mix_out(x_ref[0], sb_ref[0], cv_ref[0], sm_ref[0], bgn_ref[:, :SB_WIDTH], wm_ref)
    hx = _rms_rows(x, g_ref[...]).astype(BF16)
    q = _wdot(hx, wq_ref[...])
    scale = XA_HEAD_DIM ** -0.5
    for hh in range(XA_HEADS):
        hs = slice(hh * XA_HEAD_DIM, (hh + 1) * XA_HEAD_DIM)
        qh = (_rms_rows(q[:, hs], qg_ref[...]) * scale).astype(BF16)
        s = _dot_nt(qh, k_ref[0, :, hs])
        s = s - jnp.max(s, axis=-1, keepdims=True)
        e = jnp.exp(s)
        p = e / jnp.sum(e, axis=-1, keepdims=True)
        ob_scr[:, hs] = _dot(p.astype(BF16), v_ref[0, :, hs]).astype(BF16)
    o_ref[0] = x + _wdot(ob_scr[...], wo_ref[...])


def _mix_xattn(x, sb, cv, sm, layer, bgn, wm_bf, g, wq_bf, qg, wo_bf, k_bf, v_bf, tm):
    b, l, d = x.shape
    n = k_bf.shape[1]
    rows = lambda w: pl.BlockSpec((1, tm, w), lambda bb, i: (bb, i, 0))
    kv = pl.BlockSpec((1, n, d), lambda bb, i: (bb, 0, 0))
    nbytes = (2 * (2 * tm * d * 4 + tm * SB_WIDTH * 4 + tm * (CONV_CH + SSM_CH) * 2 + 3 * d * d * 2 + 2 * n * d * 2)
              + 5 * tm * d * 4)
    return pl.pallas_call(
        _xattn_kernel,
        grid=(b, l // tm),
        in_specs=[rows(d), rows(SB_WIDTH), rows(CONV_CH), rows(SSM_CH)]
        + [_layer_block(a, layer) for a in (bgn, wm_bf, g, wq_bf, qg, wo_bf)] + [kv, kv],
        out_specs=rows(d),
        out_shape=jax.ShapeDtypeStruct((b, l, d), F32),
        scratch_shapes=[pltpu.VMEM((tm, d), BF16)],
        compiler_params=_params(("parallel", "parallel"), nbytes),
        name="mix_xattn",
    )(x, sb, cv, sm, bgn, wm_bf, g, wq_bf, qg, wo_bf, k_bf, v_bf)


def _ffn_kernel(x_ref, g_ref, wi_ref, wo_ref, o_ref, *, th):
    hidden = wo_ref.shape[0]
    x = x_ref[...]
    h = _rms_rows(x, g_ref[...]).astype(BF16)
    o_ref[...] = x
    for c0 in range(0, hidden, th):
        c1 = min(c0 + th, hidden)
        gate = _wdot(h, wi_ref[:, c0:c1])
        up = _wdot(h, wi_ref[:, hidden + c0:hidden + c1])
        act = (gate * _sigmoid(gate) * up).astype(BF16)
        o_ref[...] += _wdot(act, wo_ref[c0:c1, :])


def _ffn(x2, layer, g, w_in, w_out, tm, th):
    t, d = x2.shape
    hidden = w_out.shape[1]
    nbytes = 3 * d * hidden * 4 + 2 * 2 * tm * d * 4 + tm * d * 2 + 6 * tm * th * 4
    return pl.pallas_call(
        functools.partial(_ffn_kernel, th=th),
        grid=(t // tm,),
        in_specs=[pl.BlockSpec((tm, d), lambda i: (i, 0))] + [_layer_block(a, layer) for a in (g, w_in, w_out)],
        out_specs=pl.BlockSpec((tm, d), lambda i: (i, 0)),
        out_shape=jax.ShapeDtypeStruct((t, d), F32),
        compiler_params=_params(("parallel",), nbytes),
        name="ffn",
    )(x2, g, w_in, w_out)


def _tile(n, want):
    want = min(want, n)
    for cand in range(want, 0, -1):
        if n % cand == 0 and (cand % V7X_SUBLANES == 0 or cand == n):
            return cand
    return n


def kernel(x, mem, norm_mix_g, w_in, sb_q_norm_g, sb_k_norm_g, conv_dw_w, conv_dw_b, conv_ln_g, conv_ln_b, conv_pw2_w, ssm_lam_re, ssm_lam_im, ssm_log_dt, ssm_b_re, ssm_b_im, ssm_c_re, ssm_c_im, ssm_d, ssm_glu_w, branch_norm_g, w_out, norm_xa_g, norm_mem_g, xa_wq, xa_wk, xa_wv, xa_q_norm_g, xa_k_norm_g, xa_wo, norm_ffn_g, ffn_w_in, ffn_w_out):
    bsz, seq, d = x.shape
    depth = w_in.shape[0]
    t = bsz * seq
    tm = _tile(seq, 512)
    tx = _tile(seq, 1024)
    tq = _tile(seq, 256)
    ts = _tile(seq, 512)
    th = 512
    assert ts % (SSM_CHUNK * V7X_SUBLANES) == 0 and tm >= CONV_HALO
    vec = lambda a: a.astype(F32).reshape(depth, 1, -1)
    qg = vec(jnp.tile(sb_q_norm_g, (1, SB_HEADS))) * (SB_HEAD_DIM ** -0.5)
    kg = vec(jnp.tile(sb_k_norm_g, (1, SB_HEADS)))
    conv = (conv_dw_w.astype(F32), vec(conv_dw_b), vec(conv_ln_g), vec(conv_ln_b), conv_pw2_w.astype(F32))
    ops = jax.vmap(_ssm_operators)(ssm_lam_re, ssm_lam_im, ssm_log_dt, ssm_b_re, ssm_b_im, ssm_c_re, ssm_c_im)
    bgn = vec(branch_norm_g)

    seg = (jnp.arange(SEG_WIDTH)[:, None] // SB_HEAD_DIM == jnp.arange(SEG_WIDTH)[None, :] // SB_HEAD_DIM).astype(BF16)
    ntri = -(jnp.arange(tq)[:, None] > jnp.arange(tq)[None, :]).astype(BF16)

    x2 = x.reshape(t, d)
    for l in range(depth):
        q, k, v, o_conv, u = _mix_in(x2, seq, l, vec(norm_mix_g), w_in.astype(F32), qg, kg, seg, conv, bgn, tx)
        o_conv = o_conv.reshape(bsz, seq, CONV_CH)
        o_sb = _sb_attn(q.reshape(bsz, seq, SB_WIDTH), k.reshape(bsz, seq, SB_WIDTH),
                        v.reshape(bsz, seq, SB_WIDTH), ntri, tq)
        o_ssm = _ssm_branch(u.reshape(bsz, seq, SSM_CH), l, ops, vec(ssm_d), ssm_glu_w.astype(F32), bgn, ts)
        k_m, v_m = _mem_kv(mem, l, vec(norm_mem_g), xa_wk.astype(F32), xa_wv.astype(F32), vec(xa_k_norm_g))
        x2 = _mix_xattn(x2.reshape(bsz, seq, d), o_sb, o_conv, o_ssm, l, bgn, w_out.astype(F32), vec(norm_xa_g),
                        xa_wq.astype(F32), vec(xa_q_norm_g), xa_wo.astype(F32), k_m, v_m, tx).reshape(t, d)
        x2 = _ffn(x2, l, vec(norm_ffn_g), ffn_w_in.astype(F32), ffn_w_out.astype(F32), tm, th)
    return x2.reshape(bsz, seq, d)
```

```python
import functools

import jax
import jax.numpy as jnp
from jax import lax
from jax.experimental import pallas as pl
from jax.experimental.pallas import tpu as pltpu

F32 = jnp.float32
BF16 = jnp.bfloat16
EPS = 1e-6

V7X_LANES = 128
V7X_SUBLANES = 8
V7X_VMEM_BYTES = 64 * 1024 * 1024

SB_HEADS = 8
SB_HEAD_DIM = 64
SB_WIDTH = SB_HEADS * SB_HEAD_DIM
SEG_WIDTH = 256
CONV_CH = 256
CONV_WIDTH = 31
CONV_HALO = 32
SSM_CH = 256
SSM_GROUP = 16
SSM_GROUPS = SSM_CH // SSM_GROUP
SSM_STATE = 64
SSM_CHUNK = 32
XA_HEADS = 4
XA_HEAD_DIM = 256


def _vmem_limit(nbytes):
    return int(min(max(nbytes * 3 // 2, 16 * 1024 * 1024), V7X_VMEM_BYTES - 8 * 1024 * 1024))


def _params(semantics, nbytes):
    return pltpu.CompilerParams(dimension_semantics=semantics, vmem_limit_bytes=_vmem_limit(nbytes))


def _dot(a, b):
    return jnp.dot(a, b, preferred_element_type=F32)


def _wdot(a, w):
    return jnp.dot(a, w.astype(BF16), preferred_element_type=F32)


def _dot_nt(a, b):
    return lax.dot_general(a, b, (((1,), (1,)), ((), ())), preferred_element_type=F32)


def _rms_rows(xf, g):
    return xf * lax.rsqrt(jnp.mean(xf * xf, axis=-1, keepdims=True) + EPS) * g


def _sigmoid(x):
    return 1.0 / (1.0 + jnp.exp(-x))


def _segment_mean_sq(p, seg):
    sq = p * p
    hi = sq.astype(BF16)
    lo = (sq - hi.astype(F32)).astype(BF16)
    parts = []
    for c0 in range(0, p.shape[1], SEG_WIDTH):
        cols = slice(c0, c0 + SEG_WIDTH)
        parts.append(_dot(hi[:, cols], seg) + _dot(lo[:, cols], seg))
    return jnp.concatenate(parts, axis=1) * (1.0 / SB_HEAD_DIM)


def _mix_in_kernel(x_ref, g_ref, w_ref, qg_ref, kg_ref, seg_ref, dww_ref, dwb_ref, lng_ref, lnb_ref, pw_ref, bgn_ref,
                   q_ref, k_ref, v_ref, cv_ref, u_ref, hbuf, sh_scr, *, tm, tiles_per_seq):
    i = pl.program_id(0)

    @pl.when(i == 0)
    def _():
        hbuf[...] = jnp.zeros_like(hbuf)

    h = _rms_rows(x_ref[...], g_ref[...]).astype(BF16)
    s1, s2, s3 = SB_WIDTH, 2 * SB_WIDTH, 3 * SB_WIDTH
    s4 = s3 + 2 * CONV_CH
    seg = seg_ref[...]
    pc = _wdot(h, w_ref[:, s3:s4])
    history = hbuf[tm:tm + CONV_HALO, :]
    hbuf[:CONV_HALO, :] = jnp.where(i % tiles_per_seq == 0, 0.0, history)
    hbuf[CONV_HALO:, :] = pc[:, :CONV_CH] * _sigmoid(pc[:, CONV_CH:])
    quarter = -(-CONV_WIDTH // 4)
    taps = [range(n * quarter, min((n + 1) * quarter, CONV_WIDTH)) for n in range(4)]
    pq = _wdot(h, w_ref[:, 0:s1])
    _conv_shift(hbuf, sh_scr, tm)
    acc = _conv_taps(hbuf, sh_scr, dww_ref, jnp.zeros((tm, CONV_CH), F32) + dwb_ref[...], taps[0], tm)
    q_ref[...] = (pq * lax.rsqrt(_segment_mean_sq(pq, seg) + EPS) * qg_ref[...]).astype(BF16)
    pk = _wdot(h, w_ref[:, s1:s2])
    acc = _conv_taps(hbuf, sh_scr, dww_ref, acc, taps[1], tm)
    k_ref[...] = (pk * lax.rsqrt(_segment_mean_sq(pk, seg) + EPS) * kg_ref[...]).astype(BF16)
    pv = _wdot(h, w_ref[:, s2:s3])
    acc = _conv_taps(hbuf, sh_scr, dww_ref, acc, taps[2], tm)
    v_ref[...] = pv.astype(BF16)
    pu = _wdot(h, w_ref[:, s4:])
    acc = _conv_taps(hbuf, sh_scr, dww_ref, acc, taps[3], tm)
    u_ref[...] = pu
    cv_ref[...] = _conv_tail(acc, lng_ref, lnb_ref, pw_ref, bgn_ref[:, SB_WIDTH:SB_WIDTH + CONV_CH])


def _layer_block(a, layer):
    tail = a.shape[1:]
    return pl.BlockSpec((None,) + tail, lambda *_: (layer,) + (0,) * len(tail), pipeline_mode=pl.Buffered(1))


def _mix_in(x2, seq, layer, g, w_bf, qg, kg, seg, conv, bgn, tm):
    t, d = x2.shape
    n_in = w_bf.shape[2]
    row = lambda i: (i, 0)
    hist = tm + CONV_HALO
    nbytes = (2 * (tm * d * 4 + d * n_in * 2 + tm * (3 * SB_WIDTH * 2 + CONV_CH * 2 + SSM_CH * 4)) + tm * n_in * 4
              + hist * CONV_CH * 4 * 14)
    return pl.pallas_call(
        functools.partial(_mix_in_kernel, tm=tm, tiles_per_seq=seq // tm),
        grid=(t // tm,),
        in_specs=[pl.BlockSpec((tm, d), row)]
        + [_layer_block(a, layer) for a in (g, w_bf, qg, kg)]
        + [pl.BlockSpec((SEG_WIDTH, SEG_WIDTH), lambda i: (0, 0))]
        + [_layer_block(a, layer) for a in (*conv, bgn)],
        out_specs=[pl.BlockSpec((tm, SB_WIDTH), row), pl.BlockSpec((tm, SB_WIDTH), row),
                   pl.BlockSpec((tm, SB_WIDTH), row), pl.BlockSpec((tm, CONV_CH), row),
                   pl.BlockSpec((tm, SSM_CH), row)],
        out_shape=[jax.ShapeDtypeStruct((t, SB_WIDTH), BF16), jax.ShapeDtypeStruct((t, SB_WIDTH), BF16),
                   jax.ShapeDtypeStruct((t, SB_WIDTH), BF16), jax.ShapeDtypeStruct((t, CONV_CH), BF16),
                   jax.ShapeDtypeStruct((t, SSM_CH), F32)],
        scratch_shapes=[pltpu.VMEM((hist, CONV_CH), F32),
                        pltpu.VMEM((V7X_SUBLANES - 1, hist - V7X_SUBLANES, CONV_CH), F32)],
        compiler_params=_params(("arbitrary",), nbytes),
        name="mix_in",
    )(x2, g, w_bf, qg, kg, seg, *conv, bgn)


SB_SLOTS = 2
SB_GROUPS_PER_STEP = 2
SB_QBLOCKS_PER_STEP = 4
SB_SOFTPLUS_LINEAR = 40.0
SB_EXP_UNDERFLOW = -104.0
SB_NEVER = -1e30


def _sb_decay(z, run, ntri, mask):
    sp = jnp.maximum(jnp.log(1.0 + jnp.exp(jnp.minimum(z, SB_SOFTPLUS_LINEAR))), z)
    if mask is not None:
        sp = jnp.where(mask, sp, 0.0)
    later = _dot(sp.astype(BF16), ntri)
    return z - sp + run, later, run + later[:, 0:1] - sp[:, 0:1]


def _sb_weights(t0, later, mask):
    w = jnp.exp(t0 + later)
    if mask is not None:
        w = jnp.where(mask, w, 0.0)
    return w.astype(BF16)


def _sb_attn_kernel(q_ref, k_ref, v_ref, ntri_ref, o_ref, q2_scr, z_scr, lat_scr, run_scr, acc_scr, *, tq):
    step = pl.program_id(2)
    nh = V7X_LANES // SB_HEAD_DIM
    m = nh * tq
    lanes = [slice(g * V7X_LANES, (g + 1) * V7X_LANES) for g in range(SB_GROUPS_PER_STEP)]
    qblocks = [step * SB_QBLOCKS_PER_STEP + s for s in range(SB_QBLOCKS_PER_STEP)]
    units = [(s, g) for s in range(SB_QBLOCKS_PER_STEP) for g in range(SB_GROUPS_PER_STEP)]

    def rows(s, j):
        i = qblocks[s]
        return pl.ds(pl.multiple_of((i - jnp.minimum(j, i)) * tq, tq), tq)

    def scores(us, j, slot):
        for s, g in us:
            z_scr[s, g, slot] = _dot_nt(q2_scr[s, g], k_ref[0, rows(s, j), lanes[g]])

    def decay(us, slot, mask=None, bias=None):
        for s, g in us:
            run = run_scr[s, g] if bias is None else run_scr[s, g] + bias[s]
            t0, later, run = _sb_decay(z_scr[s, g, slot], run, ntri_ref[...], mask)
            z_scr[s, g, slot] = t0
            lat_scr[s, g, slot] = later
            run_scr[s, g] = run

    def output(us, j, slot, mask=None):
        for s, g in us:
            w = _sb_weights(z_scr[s, g, slot], lat_scr[s, g, slot], mask)
            acc_scr[s, g] += _dot(w, v_ref[0, rows(s, j), lanes[g]])

    def live(us):
        run = run_scr[us[0]]
        for u in us[1:]:
            run = jnp.maximum(run, run_scr[u])
        return (jnp.max(run) >= SB_EXP_UNDERFLOW).astype(jnp.int32)

    lane_head = lax.broadcasted_iota(jnp.int32, (tq, V7X_LANES), 1) // SB_HEAD_DIM
    for s, g in units:
        q = q_ref[0, s * tq:(s + 1) * tq, lanes[g]]
        for h in range(nh):
            q2_scr[s, g, h * tq:(h + 1) * tq, :] = jnp.where(lane_head == h, q, jnp.zeros_like(q))
    row = lax.broadcasted_iota(jnp.int32, (m, tq), 0) % tq
    col = lax.broadcasted_iota(jnp.int32, (m, tq), 1)
    diag_mask = col < row
    run_scr[...] = jnp.zeros_like(run_scr)
    acc_scr[...] = jnp.zeros_like(acc_scr)

    scores(units, 0, 0)
    scores(units, 1, 1)
    decay(units, 0, mask=diag_mask)
    decay(units, 1, bias=[jnp.where(i >= 1, 0.0, SB_NEVER) for i in qblocks])
    output(units, 0, 0, diag_mask)
    output(units, 1, 1)

    for s, i in enumerate(qblocks):
        us = [u for u in units if u[0] == s]

        @pl.when(jnp.logical_and(i >= 2, live(us) > 0))
        def _(us=us, i=i):
            scores(us, 2, 0)
            decay(us, 0)
            scores(us, 3, 1)
            more = live(us)
            output(us, 2, 0)
            scores(us, 4, 0)
            decay(us, 1)
            n_pairs = (i - 2) // 2

            def cond(carry):
                p, more = carry
                return jnp.logical_and(p < n_pairs, more > 0)

            def body(carry):
                p, _ = carry
                t = 2 * p + 3
                output(us, t, 1)
                scores(us, t + 2, 1)
                decay(us, 0)
                more = live(us)
                output(us, t + 1, 0)
                scores(us, t + 3, 0)
                decay(us, 1)
                return p + 1, more

            _, more = lax.while_loop(cond, body, (0, more))

            @pl.when(jnp.logical_and(more > 0, (i - 2) % 2 == 1))
            def _():
                output(us, i, 1)

    for s, g in units:
        out = acc_scr[s, g, 0:tq, :]
        for h in range(1, nh):
            out = jnp.where(lane_head == h, acc_scr[s, g, h * tq:(h + 1) * tq, :], out)
        o_ref[0, s * tq:(s + 1) * tq, lanes[g]] = out


def _sb_attn(q, k, v, ntri, tq):
    b, l, w = q.shape
    ns, ng = SB_QBLOCKS_PER_STEP, SB_GROUPS_PER_STEP
    gw = ng * V7X_LANES
    assert l % (ns * tq) == 0 and w % gw == 0
    m = (V7X_LANES // SB_HEAD_DIM) * tq
    blk = pltpu.VMEM((ns, ng, SB_SLOTS, m, tq), F32)
    nbytes = (2 * (2 * l * gw * 2 + ns * tq * gw * 2 + ns * tq * gw * 4 + tq * tq * 2)
              + ns * ng * (2 * SB_SLOTS * m * tq * 4 + 3 * m * V7X_LANES * 4) + 6 * m * tq * 4)
    return pl.pallas_call(
        functools.partial(_sb_attn_kernel, tq=tq),
        grid=(b, w // gw, l // (ns * tq)),
        in_specs=[pl.BlockSpec((1, ns * tq, gw), lambda bb, hp, i: (bb, i, hp)),
                  pl.BlockSpec((1, l, gw), lambda bb, hp, i: (bb, 0, hp)),
                  pl.BlockSpec((1, l, gw), lambda bb, hp, i: (bb, 0, hp)),
                  pl.BlockSpec((tq, tq), lambda bb, hp, i: (0, 0))],
        out_specs=pl.BlockSpec((1, ns * tq, gw), lambda bb, hp, i: (bb, i, hp)),
        out_shape=jax.ShapeDtypeStruct((b, l, w), F32),
        scratch_shapes=[pltpu.VMEM((ns, ng, m, V7X_LANES), BF16), blk, blk,
                        pltpu.VMEM((ns, ng, m, 1), F32), pltpu.VMEM((ns, ng, m, V7X_LANES), F32)],
        compiler_params=_params(("parallel", "parallel", "arbitrary"), nbytes),
        name="sb_attn",
    )(q, k, v, ntri)


def _conv_shift(hbuf, sh_scr, tc):
    span = tc + CONV_HALO - V7X_SUBLANES
    for ph in range(1, V7X_SUBLANES):
        sh_scr[ph - 1] = hbuf[ph:ph + span, :]


def _conv_taps(hbuf, sh_scr, dww_ref, acc, taps, tc):
    off = CONV_HALO - (CONV_WIDTH - 1)
    for j in taps:
        ph = (off + j) % V7X_SUBLANES
        base = off + j - ph
        tap = hbuf[base:base + tc, :] if ph == 0 else sh_scr[ph - 1, base:base + tc, :]
        acc = acc + dww_ref[j:j + 1, :] * tap
    return acc


def _conv_tail(acc, lng_ref, lnb_ref, pw_ref, bg):
    mu = jnp.mean(acc, axis=-1, keepdims=True)
    cen = acc - mu
    var = jnp.mean(cen * cen, axis=-1, keepdims=True)
    y = cen * lax.rsqrt(var + EPS) * lng_ref[...] + lnb_ref[...]
    y = y * _sigmoid(y)
    o = _wdot(y.astype(BF16), pw_ref[...])
    return _rms_rows(o, bg).astype(BF16)


def _ssm_operators(lam_re, lam_im, log_dt, b_re, b_im, c_re, c_im):
    lr, li = lam_re.astype(F32), lam_im.astype(F32)
    dt = jnp.exp(log_dt.astype(F32))[:, None]
    mag = jnp.exp(lr * dt)
    ar, ai = mag * jnp.cos(li * dt), mag * jnp.sin(li * dt)
    den = lr * lr + li * li
    fr = ((ar - 1.0) * lr + ai * li) / den
    fi = (ai * lr - (ar - 1.0) * li) / den
    br, bi = b_re.astype(F32), b_im.astype(F32)
    bbr = fr[..., None] * br - fi[..., None] * bi
    bbi = fr[..., None] * bi + fi[..., None] * br
    cr, ci = c_re.astype(F32), c_im.astype(F32)
    eye = jnp.eye(SSM_GROUPS, dtype=F32)
    gp = SSM_GROUPS * SSM_STATE

    def rows_gh(w):
        return jnp.einsum('gph,gk->ghkp', w, eye).reshape(SSM_CH, gp)

    def rows_gp(w):
        return jnp.einsum('ghp,gk->gpkh', w, eye).reshape(gp, SSM_CH)

    b_op = jnp.concatenate([rows_gh(bbr), rows_gh(bbi)], axis=1)
    c_op = jnp.concatenate([rows_gp(cr), rows_gp(-ci)], axis=0)
    n = jnp.arange(1, SSM_CHUNK + 1, dtype=F32)[:, None]
    pmag = jnp.exp(n * (lr * dt).reshape(1, gp))
    ang = n * (li * dt).reshape(1, gp)
    return b_op.astype(BF16), c_op.astype(BF16), pmag * jnp.cos(ang), pmag * jnp.sin(ang)


def _ssm_kernel(u_ref, perm_ref, permt_ref, b_ref, c_ref, pr_ref, pi_ref, d_ref, gw_ref, bgn_ref, o_ref,
                xr_scr, xi_scr, er_scr, ei_scr, cr_scr, ci_scr, *, tm):
    gp = SSM_GROUPS * SSM_STATE
    nchunk = tm // SSM_CHUNK

    @pl.when(pl.program_id(0) == 0)
    def _():
        cr_scr[...] = jnp.zeros_like(cr_scr)
        ci_scr[...] = jnp.zeros_like(ci_scr)

    def pos(s):
        return pl.ds(s * nchunk, nchunk)

    def inject(b):
        up = _dot(perm_ref[...], u_ref[b].astype(BF16)).astype(BF16)
        bu = _dot(up, b_ref[...])
        xr_scr[b] = bu[:, :gp]
        xi_scr[b] = bu[:, gp:]

    def scan(b):
        ar, ai = pr_ref[0:1, :], pi_ref[0:1, :]
        xr, xi = xr_scr[b, pos(0), :], xi_scr[b, pos(0), :]
        for s in range(1, SSM_CHUNK):
            xr, xi = (ar * xr - ai * xi + xr_scr[b, pos(s), :], ar * xi + ai * xr + xi_scr[b, pos(s), :])
            xr_scr[b, pos(s), :] = xr
            xi_scr[b, pos(s), :] = xi
        nr, ni = pr_ref[SSM_CHUNK - 1:SSM_CHUNK, :], pi_ref[SSM_CHUNK - 1:SSM_CHUNK, :]
        er, ei = cr_scr[b], ci_scr[b]
        for c in range(nchunk):
            er_scr[b, c:c + 1, :] = er
            ei_scr[b, c:c + 1, :] = ei
            er, ei = (nr * er - ni * ei + xr[c:c + 1, :], nr * ei + ni * er + xi[c:c + 1, :])
        cr_scr[b] = er
        ci_scr[b] = ei
        er, ei = er_scr[b], ei_scr[b]
        for s in range(SSM_CHUNK):
            sr, si = pr_ref[s:s + 1, :], pi_ref[s:s + 1, :]
            xr_scr[b, pos(s), :] += sr * er - si * ei
            xi_scr[b, pos(s), :] += sr * ei + si * er

    def readout(b):
        yp = _dot(xr_scr[b].astype(BF16), c_ref[0:gp, :]) + _dot(xi_scr[b].astype(BF16), c_ref[gp:, :])
        hi = yp.astype(BF16)
        lo = (yp - hi.astype(F32)).astype(BF16)
        y = _dot(permt_ref[...], hi) + _dot(permt_ref[...], lo) + d_ref[...] * u_ref[b]
        z = _wdot(y.astype(BF16), gw_ref[...])
        bg = bgn_ref[:, SB_WIDTH + CONV_CH:]
        o_ref[b] = _rms_rows(z[:, :SSM_CH] * _sigmoid(z[:, SSM_CH:]), bg).astype(BF16)

    nb = u_ref.shape[0]
    for b in range(nb):
        inject(b)
    for b in range(nb):
        scan(b)
        readout(b)


def _ssm_branch(u, layer, ops, d, gw_bf, bgn, tm):
    b_op, c_op, pow_r, pow_i = ops
    b, l, ch = u.shape
    gp = SSM_GROUPS * SSM_STATE
    nchunk = tm // SSM_CHUNK
    full = lambda a: pl.BlockSpec(a.shape, lambda i: (0,) * a.ndim)
    r = jnp.arange(tm)
    perm = (r[None, :] == ((r % nchunk) * SSM_CHUNK + r // nchunk)[:, None]).astype(BF16)
    perm_t = perm.T
    nbytes = (2 * (2 * b * tm * ch * 4 + 2 * tm * tm * 2 + 2 * ch * gp * 2 * 2 + 2 * SSM_CHUNK * gp * 4 + ch * 2 * ch * 2)
              + b * (2 * tm * gp * 4 + 2 * nchunk * gp * 4 + 2 * tm * 2 * gp * 4))
    return pl.pallas_call(
        functools.partial(_ssm_kernel, tm=tm),
        grid=(l // tm,),
        in_specs=[pl.BlockSpec((b, tm, ch), lambda i: (0, i, 0)), full(perm), full(perm_t)]
        + [_layer_block(a, layer) for a in (b_op, c_op, pow_r, pow_i, d, gw_bf, bgn)],
        out_specs=pl.BlockSpec((b, tm, ch), lambda i: (0, i, 0)),
        out_shape=jax.ShapeDtypeStruct((b, l, ch), BF16),
        scratch_shapes=[pltpu.VMEM((b, tm, gp), F32), pltpu.VMEM((b, tm, gp), F32),
                        pltpu.VMEM((b, nchunk, gp), F32), pltpu.VMEM((b, nchunk, gp), F32),
                        pltpu.VMEM((b, 1, gp), F32), pltpu.VMEM((b, 1, gp), F32)],
        compiler_params=_params(("arbitrary",), nbytes),
        name="ssm_branch",
    )(u, perm, perm_t, b_op, c_op, pow_r, pow_i, d, gw_bf, bgn)


def _mix_out(x, sb, cv, sm, bg, w_ref):
    s1 = SB_WIDTH
    s2 = SB_WIDTH + CONV_CH
    y = _wdot(_rms_rows(sb, bg).astype(BF16), w_ref[0:s1, :])
    y = y + _wdot(cv, w_ref[s1:s2, :])
    y = y + _wdot(sm, w_ref[s2:, :])
    return x + y


def _mem_kv_kernel(m_ref, g_ref, wk_ref, wv_ref, kg_ref, k_ref, v_ref):
    hm = _rms_rows(m_ref[0], g_ref[...]).astype(BF16)
    kk = _wdot(hm, wk_ref[...])
    for hh in range(XA_HEADS):
        hs = slice(hh * XA_HEAD_DIM, (hh + 1) * XA_HEAD_DIM)
        k_ref[0, :, hs] = _rms_rows(kk[:, hs], kg_ref[...]).astype(BF16)
    v_ref[0] = _wdot(hm, wv_ref[...]).astype(BF16)


def _mem_kv(mem, layer, g, wk_bf, wv_bf, kg):
    b, n, d = mem.shape
    blk = pl.BlockSpec((1, n, d), lambda bb: (bb, 0, 0))
    nbytes = 2 * (n * d * 4 + 2 * d * d * 2 + 2 * n * d * 2) + 4 * n * d * 4
    return pl.pallas_call(
        _mem_kv_kernel,
        grid=(b,),
        in_specs=[blk] + [_layer_block(a, layer) for a in (g, wk_bf, wv_bf, kg)],
        out_specs=[blk, blk],
        out_shape=[jax.ShapeDtypeStruct((b, n, d), BF16), jax.ShapeDtypeStruct((b, n, d), BF16)],
        compiler_params=_params(("parallel",), nbytes),
        name="mem_kv",
    )(mem, g, wk_bf, wv_bf, kg)


def _xattn_kernel(x_ref, sb_ref, cv_ref, sm_ref, bgn_ref, wm_ref, g_ref, wq_ref, qg_ref, wo_ref, k_ref, v_ref,
                  o_ref, ob_scr):
    x = _mix_out(x_ref[0], sb_ref[0], cv_ref[0], sm_ref[0], bgn_ref[:, :SB_WIDTH], wm_ref)
    hx = _rms_rows(x, g_ref[...]).astype(BF16)
    q = _wdot(hx, wq_ref[...])
    scale = XA_HEAD_DIM ** -0.5
    for hh in range(XA_HEADS):
        hs = slice(hh * XA_HEAD_DIM, (hh + 1) * XA_HEAD_DIM)
        qh = (_rms_rows(q[:, hs], qg_ref[...]) * scale).astype(BF16)
        s = _dot_nt(qh, k_ref[0, :, hs])
        s = s - jnp.max(s, axis=-1, keepdims=True)
        e = jnp.exp(s)
        p = e / jnp.sum(e, axis=-1, keepdims=True)
        ob_scr[:, hs] = _dot(p.astype(BF16), v_ref[0, :, hs]).astype(BF16)
    o_ref[0] = x + _wdot(ob_scr[...], wo_ref[...])


def _mix_xattn(x, sb, cv, sm, layer, bgn, wm_bf, g, wq_bf, qg, wo_bf, k_bf, v_bf, tm):
    b, l, d = x.shape
    n = k_bf.shape[1]
    rows = lambda w: pl.BlockSpec((1, tm, w), lambda bb, i: (bb, i, 0))
    kv = pl.BlockSpec((1, n, d), lambda bb, i: (bb, 0, 0))
    nbytes = (2 * (2 * tm * d * 4 + tm * SB_WIDTH * 4 + tm * (CONV_CH + SSM_CH) * 2 + 3 * d * d * 2 + 2 * n * d * 2)
              + 5 * tm * d * 4)
    return pl.pallas_call(
        _xattn_kernel,
        grid=(b, l // tm),
        in_specs=[rows(d), rows(SB_WIDTH), rows(CONV_CH), rows(SSM_CH)]
        + [_layer_block(a, layer) for a in (bgn, wm_bf, g, wq_bf, qg, wo_bf)] + [kv, kv],
        out_specs=rows(d),
        out_shape=jax.ShapeDtypeStruct((b, l, d), F32),
        scratch_shapes=[pltpu.VMEM((tm, d), BF16)],
        compiler_params=_params(("parallel", "parallel"), nbytes),
        name="mix_xattn",
    )(x, sb, cv, sm, bgn, wm_bf, g, wq_bf, qg, wo_bf, k_bf, v_bf)


def _ffn_kernel(x_ref, g_ref, wi_ref, wo_ref, o_ref, *, th):
    hidden = wo_ref.shape[0]
    x = x_ref[...]
    h = _rms_rows(x, g_ref[...]).astype(BF16)
    o_ref[...] = x
    for c0 in range(0, hidden, th):
        c1 = min(c0 + th, hidden)
        gate = _wdot(h, wi_ref[:, c0:c1])
        up = _wdot(h, wi_ref[:, hidden + c0:hidden + c1])
        act = (gate * _sigmoid(gate) * up).astype(BF16)
        o_ref[...] += _wdot(act, wo_ref[c0:c1, :])


def _ffn(x2, layer, g, w_in, w_out, tm, th):
    t, d = x2.shape
    hidden = w_out.shape[1]
    nbytes = 3 * d * hidden * 4 + 2 * 2 * tm * d * 4 + tm * d * 2 + 6 * tm * th * 4
    return pl.pallas_call(
        functools.partial(_ffn_kernel, th=th),
        grid=(t // tm,),
        in_specs=[pl.BlockSpec((tm, d), lambda i: (i, 0))] + [_layer_block(a, layer) for a in (g, w_in, w_out)],
        out_specs=pl.BlockSpec((tm, d), lambda i: (i, 0)),
        out_shape=jax.ShapeDtypeStruct((t, d), F32),
        compiler_params=_params(("parallel",), nbytes),
        name="ffn",
    )(x2, g, w_in, w_out)


def _tile(n, want):
    want = min(want, n)
    for cand in range(want, 0, -1):
        if n % cand == 0 and (cand % V7X_SUBLANES == 0 or cand == n):
            return cand
    return n


def kernel(x, mem, norm_mix_g, w_in, sb_q_norm_g, sb_k_norm_g, conv_dw_w, conv_dw_b, conv_ln_g, conv_ln_b, conv_pw2_w, ssm_lam_re, ssm_lam_im, ssm_log_dt, ssm_b_re, ssm_b_im, ssm_c_re, ssm_c_im, ssm_d, ssm_glu_w, branch_norm_g, w_out, norm_xa_g, norm_mem_g, xa_wq, xa_wk, xa_wv, xa_q_norm_g, xa_k_norm_g, xa_wo, norm_ffn_g, ffn_w_in, ffn_w_out):
    bsz, seq, d = x.shape
    depth = w_in.shape[0]
    t = bsz * seq
    tm = _tile(seq, 512)
    tx = _tile(seq, 1024)
    tq = _tile(seq, 256)
    ts = _tile(seq, 512)
    th = 512
    assert ts % (SSM_CHUNK * V7X_SUBLANES) == 0 and tm >= CONV_HALO
    vec = lambda a: a.astype(F32).reshape(depth, 1, -1)
    qg = vec(jnp.tile(sb_q_norm_g, (1, SB_HEADS))) * (SB_HEAD_DIM ** -0.5)
    kg = vec(jnp.tile(sb_k_norm_g, (1, SB_HEADS)))
    conv = (conv_dw_w.astype(F32), vec(conv_dw_b), vec(conv_ln_g), vec(conv_ln_b), conv_pw2_w.astype(F32))
    ops = jax.vmap(_ssm_operators)(ssm_lam_re, ssm_lam_im, ssm_log_dt, ssm_b_re, ssm_b_im, ssm_c_re, ssm_c_im)
    bgn = vec(branch_norm_g)

    seg = (jnp.arange(SEG_WIDTH)[:, None] // SB_HEAD_DIM == jnp.arange(SEG_WIDTH)[None, :] // SB_HEAD_DIM).astype(BF16)
    ntri = -(jnp.arange(tq)[:, None] > jnp.arange(tq)[None, :]).astype(BF16)

    x2 = x.reshape(t, d)
    for l in range(depth):
        q, k, v, o_conv, u = _mix_in(x2, seq, l, vec(norm_mix_g), w_in.astype(F32), qg, kg, seg, conv, bgn, tx)
        o_conv = o_conv.reshape(bsz, seq, CONV_CH)
        o_sb = _sb_attn(q.reshape(bsz, seq, SB_WIDTH), k.reshape(bsz, seq, SB_WIDTH),
                        v.reshape(bsz, seq, SB_WIDTH), ntri, tq)
        o_ssm = _ssm_branch(u.reshape(bsz, seq, SSM_CH), l, ops, vec(ssm_d), ssm_glu_w.astype(F32), bgn, ts)
        k_m, v_m = _mem_kv(mem, l, vec(norm_mem_g), xa_wk.astype(F32), xa_wv.astype(F32), vec(xa_k_norm_g))
        x2 = _mix_xattn(x2.reshape(bsz, seq, d), o_sb, o_conv, o_ssm, l, bgn, w_out.astype(F32), vec(norm_xa_g),
                        xa_wq.astype(F32), vec(xa_q_norm_g), xa_wo.astype(F32), k_m, v_m, tx).reshape(t, d)
        x2 = _ffn(x2, l, vec(norm_ffn_g), ffn_w_in.astype(F32), ffn_w_out.astype(F32), tm, th)
    return x2.reshape(bsz, seq, d)
```

```python
import functools

import jax
import jax.numpy as jnp
from jax import lax
from jax.experimental import pallas as pl
from jax.experimental.pallas import tpu as pltpu

F32 = jnp.float32
BF16 = jnp.bfloat16
EPS = 1e-6

V7X_LANES = 128
V7X_SUBLANES = 8
V7X_VMEM_BYTES = 64 * 1024 * 1024

SB_HEADS = 8
SB_HEAD_DIM = 64
SB_WIDTH = SB_HEADS * SB_HEAD_DIM
SEG_WIDTH = 256
CONV_CH = 256
CONV_WIDTH = 31
CONV_HALO = 32
SSM_CH = 256
SSM_GROUP = 16
SSM_GROUPS = SSM_CH // SSM_GROUP
SSM_STATE = 64
SSM_CHUNK = 32
XA_HEADS = 4
XA_HEAD_DIM = 256


def _vmem_limit(nbytes):
    return int(min(max(nbytes * 3 // 2, 16 * 1024 * 1024), V7X_VMEM_BYTES - 8 * 1024 * 1024))


def _params(semantics, nbytes):
    return pltpu.CompilerParams(dimension_semantics=semantics, vmem_limit_bytes=_vmem_limit(nbytes))


def _dot(a, b):
    return jnp.dot(a, b, preferred_element_type=F32)


def _wdot(a, w):
    return jnp.dot(a, w.astype(BF16), preferred_element_type=F32)


def _dot_nt(a, b):
    return lax.dot_general(a, b, (((1,), (1,)), ((), ())), preferred_element_type=F32)


def _rms_rows(xf, g):
    return xf * lax.rsqrt(jnp.mean(xf * xf, axis=-1, keepdims=True) + EPS) * g


def _sigmoid(x):
    return 1.0 / (1.0 + jnp.exp(-x))


def _segment_mean_sq(p, seg):
    sq = p * p
    hi = sq.astype(BF16)
    lo = (sq - hi.astype(F32)).astype(BF16)
    parts = []
    for c0 in range(0, p.shape[1], SEG_WIDTH):
        cols = slice(c0, c0 + SEG_WIDTH)
        parts.append(_dot(hi[:, cols], seg) + _dot(lo[:, cols], seg))
    return jnp.concatenate(parts, axis=1) * (1.0 / SB_HEAD_DIM)


def _mix_in_kernel(x_ref, g_ref, w_ref, qg_ref, kg_ref, seg_ref, dww_ref, dwb_ref, lng_ref, lnb_ref, pw_ref, bgn_ref,
                   q_ref, k_ref, v_ref, cv_ref, u_ref, hbuf, sh_scr, *, tm, tiles_per_seq):
    i = pl.program_id(0)

    @pl.when(i == 0)
    def _():
        hbuf[...] = jnp.zeros_like(hbuf)

    h = _rms_rows(x_ref[...], g_ref[...]).astype(BF16)
    s1, s2, s3 = SB_WIDTH, 2 * SB_WIDTH, 3 * SB_WIDTH
    s4 = s3 + 2 * CONV_CH
    seg = seg_ref[...]
    pc = _wdot(h, w_ref[:, s3:s4])
    history = hbuf[tm:tm + CONV_HALO, :]
    hbuf[:CONV_HALO, :] = jnp.where(i % tiles_per_seq == 0, 0.0, history)
    hbuf[CONV_HALO:, :] = pc[:, :CONV_CH] * _sigmoid(pc[:, CONV_CH:])
    quarter = -(-CONV_WIDTH // 4)
    taps = [range(n * quarter, min((n + 1) * quarter, CONV_WIDTH)) for n in range(4)]
    pq = _wdot(h, w_ref[:, 0:s1])
    _conv_shift(hbuf, sh_scr, tm)
    acc = _conv_taps(hbuf, sh_scr, dww_ref, jnp.zeros((tm, CONV_CH), F32) + dwb_ref[...], taps[0], tm)
    q_ref[...] = (pq * lax.rsqrt(_segment_mean_sq(pq, seg) + EPS) * qg_ref[...]).astype(BF16)
    pk = _wdot(h, w_ref[:, s1:s2])
    acc = _conv_taps(hbuf, sh_scr, dww_ref, acc, taps[1], tm)
    k_ref[...] = (pk * lax.rsqrt(_segment_mean_sq(pk, seg) + EPS) * kg_ref[...]).astype(BF16)
    pv = _wdot(h, w_ref[:, s2:s3])
    acc = _conv_taps(hbuf, sh_scr, dww_ref, acc, taps[2], tm)
    v_ref[...] = pv.astype(BF16)
    pu = _wdot(h, w_ref[:, s4:])
    acc = _conv_taps(hbuf, sh_scr, dww_ref, acc, taps[3], tm)
    u_ref[...] = pu
    cv_ref[...] = _conv_tail(acc, lng_ref, lnb_ref, pw_ref, bgn_ref[:, SB_WIDTH:SB_WIDTH + CONV_CH])


def _layer_block(a, layer):
    tail = a.shape[1:]
    return pl.BlockSpec((None,) + tail, lambda *_: (layer,) + (0,) * len(tail), pipeline_mode=pl.Buffered(1))


def _mix_in(x2, seq, layer, g, w_bf, qg, kg, seg, conv, bgn, tm):
    t, d = x2.shape
    n_in = w_bf.shape[2]
    row = lambda i: (i, 0)
    hist = tm + CONV_HALO
    nbytes = (2 * (tm * d * 4 + d * n_in * 2 + tm * (3 * SB_WIDTH * 2 + CONV_CH * 2 + SSM_CH * 4)) + tm * n_in * 4
              + hist * CONV_CH * 4 * 14)
    return pl.pallas_call(
        functools.partial(_mix_in_kernel, tm=tm, tiles_per_seq=seq // tm),
        grid=(t // tm,),
        in_specs=[pl.BlockSpec((tm, d), row)]
        + [_layer_block(a, layer) for a in (g, w_bf, qg, kg)]
        + [pl.BlockSpec((SEG_WIDTH, SEG_WIDTH), lambda i: (0, 0))]
        + [_layer_block(a, layer) for a in (*conv, bgn)],
        out_specs=[pl.BlockSpec((tm, SB_WIDTH), row), pl.BlockSpec((tm, SB_WIDTH), row),
                   pl.BlockSpec((tm, SB_WIDTH), row), pl.BlockSpec((tm, CONV_CH), row),
                   pl.BlockSpec((tm, SSM_CH), row)],
        out_shape=[jax.ShapeDtypeStruct((t, SB_WIDTH), BF16), jax.ShapeDtypeStruct((t, SB_WIDTH), BF16),
                   jax.ShapeDtypeStruct((t, SB_WIDTH), BF16), jax.ShapeDtypeStruct((t, CONV_CH), BF16),
                   jax.ShapeDtypeStruct((t, SSM_CH), F32)],
        scratch_shapes=[pltpu.VMEM((hist, CONV_CH), F32),
                        pltpu.VMEM((V7X_SUBLANES - 1, hist - V7X_SUBLANES, CONV_CH), F32)],
        compiler_params=_params(("arbitrary",), nbytes),
        name="mix_in",
    )(x2, g, w_bf, qg, kg, seg, *conv, bgn)


SB_SLOTS = 2
SB_GROUPS_PER_STEP = 2
SB_QBLOCKS_PER_STEP = 2
SB_SOFTPLUS_LINEAR = 40.0
SB_EXP_UNDERFLOW = -104.0
SB_NEVER = -1e30


def _sb_decay(z, run, ntri, mask):
    sp = jnp.maximum(jnp.log(1.0 + jnp.exp(jnp.minimum(z, SB_SOFTPLUS_LINEAR))), z)
    if mask is not None:
        sp = jnp.where(mask, sp, 0.0)
    later = _dot(sp.astype(BF16), ntri)
    return z - sp + run, later, run + later[:, 0:1] - sp[:, 0:1]


def _sb_weights(t0, later, mask):
    w = jnp.exp(t0 + later)
    if mask is not None:
        w = jnp.where(mask, w, 0.0)
    return w.astype(BF16)


def _sb_attn_kernel(q_ref, k_ref, v_ref, ntri_ref, o_ref, q2_scr, z_scr, t0_scr, lat_scr, run_scr, acc_scr, *, tq):
    step = pl.program_id(2)
    nh = V7X_LANES // SB_HEAD_DIM
    m = nh * tq
    lanes = [slice(g * V7X_LANES, (g + 1) * V7X_LANES) for g in range(SB_GROUPS_PER_STEP)]
    qblocks = [step * SB_QBLOCKS_PER_STEP + s for s in range(SB_QBLOCKS_PER_STEP)]
    units = [(s, g) for s in range(SB_QBLOCKS_PER_STEP) for g in range(SB_GROUPS_PER_STEP)]

    def rows(s, j):
        i = qblocks[s]
        return pl.ds(pl.multiple_of((i - jnp.minimum(j, i)) * tq, tq), tq)

    def scores(us, j, slot):
        for s, g in us:
            z_scr[s, g, slot] = _dot_nt(q2_scr[s, g], k_ref[0, rows(s, j), lanes[g]])

    def decay(us, slot, mask=None, bias=None):
        for s, g in us:
            run = run_scr[s, g] if bias is None else run_scr[s, g] + bias[s]
            t0, later, run = _sb_decay(z_scr[s, g, slot], run, ntri_ref[...], mask)
            t0_scr[s, g, slot] = t0
            lat_scr[s, g, slot] = later
            run_scr[s, g] = run

    def output(us, j, slot, mask=None):
        for s, g in us:
            w = _sb_weights(t0_scr[s, g, slot], lat_scr[s, g, slot], mask)
            acc_scr[s, g] += _dot(w, v_ref[0, rows(s, j), lanes[g]])

    def live(us):
        run = run_scr[us[0]]
        for u in us[1:]:
            run = jnp.maximum(run, run_scr[u])
        return (jnp.max(run) >= SB_EXP_UNDERFLOW).astype(jnp.int32)

    lane_head = lax.broadcasted_iota(jnp.int32, (tq, V7X_LANES), 1) // SB_HEAD_DIM
    for s, g in units:
        q = q_ref[0, s * tq:(s + 1) * tq, lanes[g]]
        for h in range(nh):
            q2_scr[s, g, h * tq:(h + 1) * tq, :] = jnp.where(lane_head == h, q, jnp.zeros_like(q))
    row = lax.broadcasted_iota(jnp.int32, (m, tq), 0) % tq
    col = lax.broadcasted_iota(jnp.int32, (m, tq), 1)
    diag_mask = col < row
    run_scr[...] = jnp.zeros_like(run_scr)
    acc_scr[...] = jnp.zeros_like(acc_scr)

    scores(units, 0, 0)
    scores(units, 1, 1)
    decay(units, 0, mask=diag_mask)
    decay(units, 1, bias=[jnp.where(i >= 1, 0.0, SB_NEVER) for i in qblocks])
    output(units, 0, 0, diag_mask)
    output(units, 1, 1)

    for s, i in enumerate(qblocks):
        us = [u for u in units if u[0] == s]

        @pl.when(jnp.logical_and(i >= 2, live(us) > 0))
        def _(us=us, i=i):
            scores(us, 2, 0)
            decay(us, 0)
            scores(us, 3, 1)
            more = live(us)
            output(us, 2, 0)
            scores(us, 4, 0)
            decay(us, 1)
            n_pairs = (i - 2) // 2

            def cond(carry):
                p, more = carry
                return jnp.logical_and(p < n_pairs, more > 0)

            def body(carry):
                p, _ = carry
                t = 2 * p + 3
                scores(us, t + 2, 1)
                output(us, t, 1)
                decay(us, 0)
                more = live(us)
                scores(us, t + 3, 0)
                output(us, t + 1, 0)
                decay(us, 1)
                return p + 1, more

            _, more = lax.while_loop(cond, body, (0, more))

            @pl.when(jnp.logical_and(more > 0, (i - 2) % 2 == 1))
            def _():
                output(us, i, 1)

    for s, g in units:
        out = acc_scr[s, g, 0:tq, :]
        for h in range(1, nh):
            out = jnp.where(lane_head == h, acc_scr[s, g, h * tq:(h + 1) * tq, :], out)
        o_ref[0, s * tq:(s + 1) * tq, lanes[g]] = out


def _sb_attn(q, k, v, ntri, tq):
    b, l, w = q.shape
    ns, ng = SB_QBLOCKS_PER_STEP, SB_GROUPS_PER_STEP
    gw = ng * V7X_LANES
    assert l % (ns * tq) == 0 and w % gw == 0
    m = (V7X_LANES // SB_HEAD_DIM) * tq
    blk = pltpu.VMEM((ns, ng, SB_SLOTS, m, tq), F32)
    nbytes = (2 * (2 * l * gw * 2 + ns * tq * gw * 2 + ns * tq * gw * 4 + tq * tq * 2)
              + ns * ng * (3 * SB_SLOTS * m * tq * 4 + 3 * m * V7X_LANES * 4) + 6 * m * tq * 4)
    return pl.pallas_call(
        functools.partial(_sb_attn_kernel, tq=tq),
        grid=(b, w // gw, l // (ns * tq)),
        in_specs=[pl.BlockSpec((1, ns * tq, gw), lambda bb, hp, i: (bb, i, hp)),
                  pl.BlockSpec((1, l, gw), lambda bb, hp, i: (bb, 0, hp)),
                  pl.BlockSpec((1, l, gw), lambda bb, hp, i: (bb, 0, hp)),
                  pl.BlockSpec((tq, tq), lambda bb, hp, i: (0, 0))],
        out_specs=pl.BlockSpec((1, ns * tq, gw), lambda bb, hp, i: (bb, i, hp)),
        out_shape=jax.ShapeDtypeStruct((b, l, w), F32),
        scratch_shapes=[pltpu.VMEM((ns, ng, m, V7X_LANES), BF16), blk, blk, blk,
                        pltpu.VMEM((ns, ng, m, 1), F32), pltpu.VMEM((ns, ng, m, V7X_LANES), F32)],
        compiler_params=_params(("parallel", "parallel", "arbitrary"), nbytes),
        name="sb_attn",
    )(q, k, v, ntri)


def _conv_shift(hbuf, sh_scr, tc):
    span = tc + CONV_HALO - V7X_SUBLANES
    for ph in range(1, V7X_SUBLANES):
        sh_scr[ph - 1] = hbuf[ph:ph + span, :]


def _conv_taps(hbuf, sh_scr, dww_ref, acc, taps, tc):
    off = CONV_HALO - (CONV_WIDTH - 1)
    for j in taps:
        ph = (off + j) % V7X_SUBLANES
        base = off + j - ph
        tap = hbuf[base:base + tc, :] if ph == 0 else sh_scr[ph - 1, base:base + tc, :]
        acc = acc + dww_ref[j:j + 1, :] * tap
    return acc


def _conv_tail(acc, lng_ref, lnb_ref, pw_ref, bg):
    mu = jnp.mean(acc, axis=-1, keepdims=True)
    cen = acc - mu
    var = jnp.mean(cen * cen, axis=-1, keepdims=True)
    y = cen * lax.rsqrt(var + EPS) * lng_ref[...] + lnb_ref[...]
    y = y * _sigmoid(y)
    o = _wdot(y.astype(BF16), pw_ref[...])
    return _rms_rows(o, bg).astype(BF16)


def _ssm_operators(lam_re, lam_im, log_dt, b_re, b_im, c_re, c_im):
    lr, li = lam_re.astype(F32), lam_im.astype(F32)
    dt = jnp.exp(log_dt.astype(F32))[:, None]
    mag = jnp.exp(lr * dt)
    ar, ai = mag * jnp.cos(li * dt), mag * jnp.sin(li * dt)
    den = lr * lr + li * li
    fr = ((ar - 1.0) * lr + ai * li) / den
    fi = (ai * lr - (ar - 1.0) * li) / den
    br, bi = b_re.astype(F32), b_im.astype(F32)
    bbr = fr[..., None] * br - fi[..., None] * bi
    bbi = fr[..., None] * bi + fi[..., None] * br
    cr, ci = c_re.astype(F32), c_im.astype(F32)
    eye = jnp.eye(SSM_GROUPS, dtype=F32)
    gp = SSM_GROUPS * SSM_STATE

    def rows_gh(w):
        return jnp.einsum('gph,gk->ghkp', w, eye).reshape(SSM_CH, gp)

    def rows_gp(w):
        return jnp.einsum('ghp,gk->gpkh', w, eye).reshape(gp, SSM_CH)

    b_op = jnp.concatenate([rows_gh(bbr), rows_gh(bbi)], axis=1)
    c_op = jnp.concatenate([rows_gp(cr), rows_gp(-ci)], axis=0)
    n = jnp.arange(1, SSM_CHUNK + 1, dtype=F32)[:, None]
    pmag = jnp.exp(n * (lr * dt).reshape(1, gp))
    ang = n * (li * dt).reshape(1, gp)
    return b_op.astype(BF16), c_op.astype(BF16), pmag * jnp.cos(ang), pmag * jnp.sin(ang)


def _ssm_kernel(u_ref, perm_ref, permt_ref, b_ref, c_ref, pr_ref, pi_ref, d_ref, gw_ref, bgn_ref, o_ref,
                xr_scr, xi_scr, er_scr, ei_scr, cr_scr, ci_scr, *, tm):
    gp = SSM_GROUPS * SSM_STATE
    nchunk = tm // SSM_CHUNK

    @pl.when(pl.program_id(0) == 0)
    def _():
        cr_scr[...] = jnp.zeros_like(cr_scr)
        ci_scr[...] = jnp.zeros_like(ci_scr)

    def pos(s):
        return pl.ds(s * nchunk, nchunk)

    def inject(b):
        up = _dot(perm_ref[...], u_ref[b].astype(BF16)).astype(BF16)
        bu = _dot(up, b_ref[...])
        xr_scr[b] = bu[:, :gp]
        xi_scr[b] = bu[:, gp:]

    def scan(b):
        ar, ai = pr_ref[0:1, :], pi_ref[0:1, :]
        xr, xi = xr_scr[b, pos(0), :], xi_scr[b, pos(0), :]
        for s in range(1, SSM_CHUNK):
            xr, xi = (ar * xr - ai * xi + xr_scr[b, pos(s), :], ar * xi + ai * xr + xi_scr[b, pos(s), :])
            xr_scr[b, pos(s), :] = xr
            xi_scr[b, pos(s), :] = xi
        nr, ni = pr_ref[SSM_CHUNK - 1:SSM_CHUNK, :], pi_ref[SSM_CHUNK - 1:SSM_CHUNK, :]
        er, ei = cr_scr[b], ci_scr[b]
        for c in range(nchunk):
            er_scr[b, c:c + 1, :] = er
            ei_scr[b, c:c + 1, :] = ei
            er, ei = (nr * er - ni * ei + xr[c:c + 1, :], nr * ei + ni * er + xi[c:c + 1, :])
        cr_scr[b] = er
        ci_scr[b] = ei
        er, ei = er_scr[b], ei_scr[b]
        for s in range(SSM_CHUNK):
            sr, si = pr_ref[s:s + 1, :], pi_ref[s:s + 1, :]
            xr_scr[b, pos(s), :] += sr * er - si * ei
            xi_scr[b, pos(s), :] += sr * ei + si * er

    def readout(b):
        yp = _dot(xr_scr[b].astype(BF16), c_ref[0:gp, :]) + _dot(xi_scr[b].astype(BF16), c_ref[gp:, :])
        hi = yp.astype(BF16)
        lo = (yp - hi.astype(F32)).astype(BF16)
        y = _dot(permt_ref[...], hi) + _dot(permt_ref[...], lo) + d_ref[...] * u_ref[b]
        z = _wdot(y.astype(BF16), gw_ref[...])
        bg = bgn_ref[:, SB_WIDTH + CONV_CH:]
        o_ref[b] = _rms_rows(z[:, :SSM_CH] * _sigmoid(z[:, SSM_CH:]), bg).astype(BF16)

    nb = u_ref.shape[0]
    for b in range(nb):
        inject(b)
    for b in range(nb):
        scan(b)
        readout(b)


def _ssm_branch(u, layer, ops, d, gw_bf, bgn, tm):
    b_op, c_op, pow_r, pow_i = ops
    b, l, ch = u.shape
    gp = SSM_GROUPS * SSM_STATE
    nchunk = tm // SSM_CHUNK
    full = lambda a: pl.BlockSpec(a.shape, lambda i: (0,) * a.ndim)
    r = jnp.arange(tm)
    perm = (r[None, :] == ((r % nchunk) * SSM_CHUNK + r // nchunk)[:, None]).astype(BF16)
    perm_t = perm.T
    nbytes = (2 * (2 * b * tm * ch * 4 + 2 * tm * tm * 2 + 2 * ch * gp * 2 * 2 + 2 * SSM_CHUNK * gp * 4 + ch * 2 * ch * 2)
              + b * (2 * tm * gp * 4 + 2 * nchunk * gp * 4 + 2 * tm * 2 * gp * 4))
    return pl.pallas_call(
        functools.partial(_ssm_kernel, tm=tm),
        grid=(l // tm,),
        in_specs=[pl.BlockSpec((b, tm, ch), lambda i: (0, i, 0)), full(perm), full(perm_t)]
        + [_layer_block(a, layer) for a in (b_op, c_op, pow_r, pow_i, d, gw_bf, bgn)],
        out_specs=pl.BlockSpec((b, tm, ch), lambda i: (0, i, 0)),
        out_shape=jax.ShapeDtypeStruct((b, l, ch), BF16),
        scratch_shapes=[pltpu.VMEM((b, tm, gp), F32), pltpu.VMEM((b, tm, gp), F32),
                        pltpu.VMEM((b, nchunk, gp), F32), pltpu.VMEM((b, nchunk, gp), F32),
                        pltpu.VMEM((b, 1, gp), F32), pltpu.VMEM((b, 1, gp), F32)],
        compiler_params=_params(("arbitrary",), nbytes),
        name="ssm_branch",
    )(u, perm, perm_t, b_op, c_op, pow_r, pow_i, d, gw_bf, bgn)


def _mix_out(x, sb, cv, sm, bg, w_ref):
    s1 = SB_WIDTH
    s2 = SB_WIDTH + CONV_CH
    y = _wdot(_rms_rows(sb, bg).astype(BF16), w_ref[0:s1, :])
    y = y + _wdot(cv, w_ref[s1:s2, :])
    y = y + _wdot(sm, w_ref[s2:, :])
    return x + y


def _mem_kv_kernel(m_ref, g_ref, wk_ref, wv_ref, kg_ref, k_ref, v_ref):
    hm = _rms_rows(m_ref[0], g_ref[...]).astype(BF16)
    kk = _wdot(hm, wk_ref[...])
    for hh in range(XA_HEADS):
        hs = slice(hh * XA_HEAD_DIM, (hh + 1) * XA_HEAD_DIM)
        k_ref[0, :, hs] = _rms_rows(kk[:, hs], kg_ref[...]).astype(BF16)
    v_ref[0] = _wdot(hm, wv_ref[...]).astype(BF16)


def _mem_kv(mem, layer, g, wk_bf, wv_bf, kg):
    b, n, d = mem.shape
    blk = pl.BlockSpec((1, n, d), lambda bb: (bb, 0, 0))
    nbytes = 2 * (n * d * 4 + 2 * d * d * 2 + 2 * n * d * 2) + 4 * n * d * 4
    return pl.pallas_call(
        _mem_kv_kernel,
        grid=(b,),
        in_specs=[blk] + [_layer_block(a, layer) for a in (g, wk_bf, wv_bf, kg)],
        out_specs=[blk, blk],
        out_shape=[jax.ShapeDtypeStruct((b, n, d), BF16), jax.ShapeDtypeStruct((b, n, d), BF16)],
        compiler_params=_params(("parallel",), nbytes),
        name="mem_kv",
    )(mem, g, wk_bf, wv_bf, kg)


def _xattn_kernel(x_ref, sb_ref, cv_ref, sm_ref, bgn_ref, wm_ref, g_ref, wq_ref, qg_ref, wo_ref, k_ref, v_ref,
                  o_ref, ob_scr):
    x = _mix_out(x_ref[0], sb_ref[0], cv_ref[0], sm_ref[0], bgn_ref[:, :SB_WIDTH], wm_ref)
    hx = _rms_rows(x, g_ref[...]).astype(BF16)
    q = _wdot(hx, wq_ref[...])
    scale = XA_HEAD_DIM ** -0.5
    for hh in range(XA_HEADS):
        hs = slice(hh * XA_HEAD_DIM, (hh + 1) * XA_HEAD_DIM)
        qh = (_rms_rows(q[:, hs], qg_ref[...]) * scale).astype(BF16)
        s = _dot_nt(qh, k_ref[0, :, hs])
        s = s - jnp.max(s, axis=-1, keepdims=True)
        e = jnp.exp(s)
        p = e / jnp.sum(e, axis=-1, keepdims=True)
        ob_scr[:, hs] = _dot(p.astype(BF16), v_ref[0, :, hs]).astype(BF16)
    o_ref[0] = x + _wdot(ob_scr[...], wo_ref[...])


def _mix_xattn(x, sb, cv, sm, layer, bgn, wm_bf, g, wq_bf, qg, wo_bf, k_bf, v_bf, tm):
    b, l, d = x.shape
    n = k_bf.shape[1]
    rows = lambda w: pl.BlockSpec((1, tm, w), lambda bb, i: (bb, i, 0))
    kv = pl.BlockSpec((1, n, d), lambda bb, i: (bb, 0, 0))
    nbytes = (2 * (2 * tm * d * 4 + tm * SB_WIDTH * 4 + tm * (CONV_CH + SSM_CH) * 2 + 3 * d * d * 2 + 2 * n * d * 2)
              + 5 * tm * d * 4)
    return pl.pallas_call(
        _xattn_kernel,
        grid=(b, l // tm),
        in_specs=[rows(d), rows(SB_WIDTH), rows(CONV_CH), rows(SSM_CH)]
        + [_layer_block(a, layer) for a in (bgn, wm_bf, g, wq_bf, qg, wo_bf)] + [kv, kv],
        out_specs=rows(d),
        out_shape=jax.ShapeDtypeStruct((b, l, d), F32),
        scratch_shapes=[pltpu.VMEM((tm, d), BF16)],
        compiler_params=_params(("parallel", "parallel"), nbytes),
        name="mix_xattn",
    )(x, sb, cv, sm, bgn, wm_bf, g, wq_bf, qg, wo_bf, k_bf, v_bf)


def _ffn_kernel(x_ref, g_ref, wi_ref, wo_ref, o_ref, *, th):
    hidden = wo_ref.shape[0]
    x = x_ref[...]
    h = _rms_rows(x, g_ref[...]).astype(BF16)
    o_ref[...] = x
    for c0 in range(0, hidden, th):
        c1 = min(c0 + th, hidden)
        gate = _wdot(h, wi_ref[:, c0:c1])
        up = _wdot(h, wi_ref[:, hidden + c0:hidden + c1])
        act = (gate * _sigmoid(gate) * up).astype(BF16)
        o_ref[...] += _wdot(act, wo_ref[c0:c1, :])


def _ffn(x2, layer, g, w_in, w_out, tm, th):
    t, d = x2.shape
    hidden = w_out.shape[1]
    nbytes = 3 * d * hidden * 4 + 2 * 2 * tm * d * 4 + tm * d * 2 + 6 * tm * th * 4
    return pl.pallas_call(
        functools.partial(_ffn_kernel, th=th),
        grid=(t // tm,),
        in_specs=[pl.BlockSpec((tm, d), lambda i: (i, 0))] + [_layer_block(a, layer) for a in (g, w_in, w_out)],
        out_specs=pl.BlockSpec((tm, d), lambda i: (i, 0)),
        out_shape=jax.ShapeDtypeStruct((t, d), F32),
        compiler_params=_params(("parallel",), nbytes),
        name="ffn",
    )(x2, g, w_in, w_out)


def _tile(n, want):
    want = min(want, n)
    for cand in range(want, 0, -1):
        if n % cand == 0 and (cand % V7X_SUBLANES == 0 or cand == n):
            return cand
    return n


def kernel(x, mem, norm_mix_g, w_in, sb_q_norm_g, sb_k_norm_g, conv_dw_w, conv_dw_b, conv_ln_g, conv_ln_b, conv_pw2_w, ssm_lam_re, ssm_lam_im, ssm_log_dt, ssm_b_re, ssm_b_im, ssm_c_re, ssm_c_im, ssm_d, ssm_glu_w, branch_norm_g, w_out, norm_xa_g, norm_mem_g, xa_wq, xa_wk, xa_wv, xa_q_norm_g, xa_k_norm_g, xa_wo, norm_ffn_g, ffn_w_in, ffn_w_out):
    bsz, seq, d = x.shape
    depth = w_in.shape[0]
    t = bsz * seq
    tm = _tile(seq, 512)
    tx = _tile(seq, 1024)
    tq = _tile(seq, 256)
    ts = _tile(seq, 512)
    th = 256
    assert ts % (SSM_CHUNK * V7X_SUBLANES) == 0 and tm >= CONV_HALO
    vec = lambda a: a.astype(F32).reshape(depth, 1, -1)
    qg = vec(jnp.tile(sb_q_norm_g, (1, SB_HEADS))) * (SB_HEAD_DIM ** -0.5)
    kg = vec(jnp.tile(sb_k_norm_g, (1, SB_HEADS)))
    conv = (conv_dw_w.astype(F32), vec(conv_dw_b), vec(conv_ln_g), vec(conv_ln_b), conv_pw2_w.astype(F32))
    ops = jax.vmap(_ssm_operators)(ssm_lam_re, ssm_lam_im, ssm_log_dt, ssm_b_re, ssm_b_im, ssm_c_re, ssm_c_im)
    bgn = vec(branch_norm_g)

    seg = (jnp.arange(SEG_WIDTH)[:, None] // SB_HEAD_DIM == jnp.arange(SEG_WIDTH)[None, :] // SB_HEAD_DIM).astype(BF16)
    ntri = -(jnp.arange(tq)[:, None] > jnp.arange(tq)[None, :]).astype(BF16)

    x2 = x.reshape(t, d)
    for l in range(depth):
        q, k, v, o_conv, u = _mix_in(x2, seq, l, vec(norm_mix_g), w_in.astype(F32), qg, kg, seg, conv, bgn, tx)
        o_conv = o_conv.reshape(bsz, seq, CONV_CH)
        o_sb = _sb_attn(q.reshape(bsz, seq, SB_WIDTH), k.reshape(bsz, seq, SB_WIDTH),
                        v.reshape(bsz, seq, SB_WIDTH), ntri, tq)
        o_ssm = _ssm_branch(u.reshape(bsz, seq, SSM_CH), l, ops, vec(ssm_d), ssm_glu_w.astype(F32), bgn, ts)
        k_m, v_m = _mem_kv(mem, l, vec(norm_mem_g), xa_wk.astype(F32), xa_wv.astype(F32), vec(xa_k_norm_g))
        x2 = _mix_xattn(x2.reshape(bsz, seq, d), o_sb, o_conv, o_ssm, l, bgn, w_out.astype(F32), vec(norm_xa_g),
                        xa_wq.astype(F32), vec(xa_q_norm_g), xa_wo.astype(F32), k_m, v_m, tx).reshape(t, d)
        x2 = _ffn(x2, l, vec(norm_ffn_g), ffn_w_in.astype(F32), ffn_w_out.astype(F32), tm, th)
    return x2.reshape(bsz, seq, d)
```

```python
import functools

import jax
import jax.numpy as jnp
from jax import lax
from jax.experimental import pallas as pl
from jax.experimental.pallas import tpu as pltpu

F32 = jnp.float32
BF16 = jnp.bfloat16
EPS = 1e-6

V7X_LANES = 128
V7X_SUBLANES = 8
V7X_VMEM_BYTES = 64 * 1024 * 1024

SB_HEADS = 8
SB_HEAD_DIM = 64
SB_WIDTH = SB_HEADS * SB_HEAD_DIM
SEG_WIDTH = 256
CONV_CH = 256
CONV_WIDTH = 31
CONV_HALO = 32
SSM_CH = 256
SSM_GROUP = 16
SSM_GROUPS = SSM_CH // SSM_GROUP
SSM_STATE = 64
SSM_CHUNK = 32
XA_HEADS = 4
XA_HEAD_DIM = 256


def _vmem_limit(nbytes):
    return int(min(max(nbytes * 3 // 2, 16 * 1024 * 1024), V7X_VMEM_BYTES - 8 * 1024 * 1024))


def _params(semantics, nbytes):
    return pltpu.CompilerParams(dimension_semantics=semantics, vmem_limit_bytes=_vmem_limit(nbytes))


def _dot(a, b):
    return jnp.dot(a, b, preferred_element_type=F32)


def _wdot(a, w):
    return jnp.dot(a, w.astype(BF16), preferred_element_type=F32)


def _dot_nt(a, b):
    return lax.dot_general(a, b, (((1,), (1,)), ((), ())), preferred_element_type=F32)


def _rms_rows(xf, g):
    return xf * lax.rsqrt(jnp.mean(xf * xf, axis=-1, keepdims=True) + EPS) * g


def _sigmoid(x):
    return 1.0 / (1.0 + jnp.exp(-x))


def _segment_mean_sq(p, seg):
    sq = p * p
    hi = sq.astype(BF16)
    lo = (sq - hi.astype(F32)).astype(BF16)
    parts = []
    for c0 in range(0, p.shape[1], SEG_WIDTH):
        cols = slice(c0, c0 + SEG_WIDTH)
        parts.append(_dot(hi[:, cols], seg) + _dot(lo[:, cols], seg))
    return jnp.concatenate(parts, axis=1) * (1.0 / SB_HEAD_DIM)


def _mix_in_kernel(x_ref, g_ref, w_ref, qg_ref, kg_ref, seg_ref, dww_ref, dwb_ref, lng_ref, lnb_ref, pw_ref, bgn_ref,
                   q_ref, k_ref, v_ref, cv_ref, u_ref, hbuf, sh_scr, *, tm, tiles_per_seq):
    i = pl.program_id(0)

    @pl.when(i == 0)
    def _():
        hbuf[...] = jnp.zeros_like(hbuf)

    h = _rms_rows(x_ref[...], g_ref[...]).astype(BF16)
    s1, s2, s3 = SB_WIDTH, 2 * SB_WIDTH, 3 * SB_WIDTH
    s4 = s3 + 2 * CONV_CH
    seg = seg_ref[...]
    pc = _wdot(h, w_ref[:, s3:s4])
    history = hbuf[tm:tm + CONV_HALO, :]
    hbuf[:CONV_HALO, :] = jnp.where(i % tiles_per_seq == 0, 0.0, history)
    hbuf[CONV_HALO:, :] = pc[:, :CONV_CH] * _sigmoid(pc[:, CONV_CH:])
    quarter = -(-CONV_WIDTH // 4)
    taps = [range(n * quarter, min((n + 1) * quarter, CONV_WIDTH)) for n in range(4)]
    pq = _wdot(h, w_ref[:, 0:s1])
    _conv_shift(hbuf, sh_scr, tm)
    acc = _conv_taps(hbuf, sh_scr, dww_ref, jnp.zeros((tm, CONV_CH), F32) + dwb_ref[...], taps[0], tm)
    q_ref[...] = (pq * lax.rsqrt(_segment_mean_sq(pq, seg) + EPS) * qg_ref[...]).astype(BF16)
    pk = _wdot(h, w_ref[:, s1:s2])
    acc = _conv_taps(hbuf, sh_scr, dww_ref, acc, taps[1], tm)
    k_ref[...] = (pk * lax.rsqrt(_segment_mean_sq(pk, seg) + EPS) * kg_ref[...]).astype(BF16)
    pv = _wdot(h, w_ref[:, s2:s3])
    acc = _conv_taps(hbuf, sh_scr, dww_ref, acc, taps[2], tm)
    v_ref[...] = pv.astype(BF16)
    pu = _wdot(h, w_ref[:, s4:])
    acc = _conv_taps(hbuf, sh_scr, dww_ref, acc, taps[3], tm)
    u_ref[...] = pu
    cv_ref[...] = _conv_tail(acc, lng_ref, lnb_ref, pw_ref, bgn_ref[:, SB_WIDTH:SB_WIDTH + CONV_CH])


def _layer_block(a, layer):
    tail = a.shape[1:]
    return pl.BlockSpec((None,) + tail, lambda *_: (layer,) + (0,) * len(tail), pipeline_mode=pl.Buffered(1))


def _mix_in(x2, seq, layer, g, w_bf, qg, kg, seg, conv, bgn, tm):
    t, d = x2.shape
    n_in = w_bf.shape[2]
    row = lambda i: (i, 0)
    hist = tm + CONV_HALO
    nbytes = (2 * (tm * d * 4 + d * n_in * 2 + tm * (3 * SB_WIDTH * 2 + CONV_CH * 2 + SSM_CH * 4)) + tm * n_in * 4
              + hist * CONV_CH * 4 * 14)
    return pl.pallas_call(
        functools.partial(_mix_in_kernel, tm=tm, tiles_per_seq=seq // tm),
        grid=(t // tm,),
        in_specs=[pl.BlockSpec((tm, d), row)]
        + [_layer_block(a, layer) for a in (g, w_bf, qg, kg)]
        + [pl.BlockSpec((SEG_WIDTH, SEG_WIDTH), lambda i: (0, 0))]
        + [_layer_block(a, layer) for a in (*conv, bgn)],
        out_specs=[pl.BlockSpec((tm, SB_WIDTH), row), pl.BlockSpec((tm, SB_WIDTH), row),
                   pl.BlockSpec((tm, SB_WIDTH), row), pl.BlockSpec((tm, CONV_CH), row),
                   pl.BlockSpec((tm, SSM_CH), row)],
        out_shape=[jax.ShapeDtypeStruct((t, SB_WIDTH), BF16), jax.ShapeDtypeStruct((t, SB_WIDTH), BF16),
                   jax.ShapeDtypeStruct((t, SB_WIDTH), BF16), jax.ShapeDtypeStruct((t, CONV_CH), BF16),
                   jax.ShapeDtypeStruct((t, SSM_CH), F32)],
        scratch_shapes=[pltpu.VMEM((hist, CONV_CH), F32),
                        pltpu.VMEM((V7X_SUBLANES - 1, hist - V7X_SUBLANES, CONV_CH), F32)],
        compiler_params=_params(("arbitrary",), nbytes),
        name="mix_in",
    )(x2, g, w_bf, qg, kg, seg, *conv, bgn)


SB_SLOTS = 2
SB_GROUPS_PER_STEP = 2
SB_QBLOCKS_PER_STEP = 2
SB_SOFTPLUS_LINEAR = 40.0
SB_EXP_UNDERFLOW = -104.0
SB_NEVER = -1e30


def _sb_decay(z, run, ntri, mask):
    sp = jnp.maximum(jnp.log(1.0 + jnp.exp(jnp.minimum(z, SB_SOFTPLUS_LINEAR))), z)
    if mask is not None:
        sp = jnp.where(mask, sp, 0.0)
    later = _dot(sp.astype(BF16), ntri)
    return z - sp + run, later, run + later[:, 0:1] - sp[:, 0:1]


def _sb_weights(t0, later, mask):
    w = jnp.exp(t0 + later)
    if mask is not None:
        w = jnp.where(mask, w, 0.0)
    return w.astype(BF16)


def _sb_attn_kernel(q_ref, k_ref, v_ref, ntri_ref, o_ref, q2_scr, z_scr, t0_scr, lat_scr, run_scr, acc_scr, *, tq):
    step = pl.program_id(2)
    nh = V7X_LANES // SB_HEAD_DIM
    m = nh * tq
    lanes = [slice(g * V7X_LANES, (g + 1) * V7X_LANES) for g in range(SB_GROUPS_PER_STEP)]
    qblocks = [step * SB_QBLOCKS_PER_STEP + s for s in range(SB_QBLOCKS_PER_STEP)]
    units = [(s, g) for s in range(SB_QBLOCKS_PER_STEP) for g in range(SB_GROUPS_PER_STEP)]

    def rows(s, j):
        i = qblocks[s]
        return pl.ds(pl.multiple_of((i - jnp.minimum(j, i)) * tq, tq), tq)

    def scores(us, j, slot):
        for s, g in us:
            z_scr[s, g, slot] = _dot_nt(q2_scr[s, g], k_ref[0, rows(s, j), lanes[g]])

    def decay(us, slot, mask=None, bias=None):
        for s, g in us:
            run = run_scr[s, g] if bias is None else run_scr[s, g] + bias[s]
            t0, later, run = _sb_decay(z_scr[s, g, slot], run, ntri_ref[...], mask)
            t0_scr[s, g, slot] = t0
            lat_scr[s, g, slot] = later
            run_scr[s, g] = run

    def output(us, j, slot, mask=None):
        for s, g in us:
            w = _sb_weights(t0_scr[s, g, slot], lat_scr[s, g, slot], mask)
            acc_scr[s, g] += _dot(w, v_ref[0, rows(s, j), lanes[g]])

    def live(us):
        run = run_scr[us[0]]
        for u in us[1:]:
            run = jnp.maximum(run, run_scr[u])
        return (jnp.max(run) >= SB_EXP_UNDERFLOW).astype(jnp.int32)

    lane_head = lax.broadcasted_iota(jnp.int32, (tq, V7X_LANES), 1) // SB_HEAD_DIM
    for s, g in units:
        q = q_ref[0, s * tq:(s + 1) * tq, lanes[g]]
        for h in range(nh):
            q2_scr[s, g, h * tq:(h + 1) * tq, :] = jnp.where(lane_head == h, q, jnp.zeros_like(q))
    row = lax.broadcasted_iota(jnp.int32, (m, tq), 0) % tq
    col = lax.broadcasted_iota(jnp.int32, (m, tq), 1)
    diag_mask = col < row
    run_scr[...] = jnp.zeros_like(run_scr)
    acc_scr[...] = jnp.zeros_like(acc_scr)

    scores(units, 0, 0)
    scores(units, 1, 1)
    decay(units, 0, mask=diag_mask)
    decay(units, 1, bias=[jnp.where(i >= 1, 0.0, SB_NEVER) for i in qblocks])
    output(units, 0, 0, diag_mask)
    output(units, 1, 1)

    for s, i in enumerate(qblocks):
        us = [u for u in units if u[0] == s]

        @pl.when(jnp.logical_and(i >= 2, live(us) > 0))
        def _(us=us, i=i):
            scores(us, 2, 0)
            decay(us, 0)
            scores(us, 3, 1)
            more = live(us)
            output(us, 2, 0)
            scores(us, 4, 0)
            decay(us, 1)
            n_pairs = (i - 2) // 2

            def cond(carry):
                p, more = carry
                return jnp.logical_and(p < n_pairs, more > 0)

            def body(carry):
                p, _ = carry
                t = 2 * p + 3
                scores(us, t + 2, 1)
                output(us, t, 1)
                decay(us, 0)
                more = live(us)
                scores(us, t + 3, 0)
                output(us, t + 1, 0)
                decay(us, 1)
                return p + 1, more

            _, more = lax.while_loop(cond, body, (0, more))

            @pl.when(jnp.logical_and(more > 0, (i - 2) % 2 == 1))
            def _():
                output(us, i, 1)

    for s, g in units:
        out = acc_scr[s, g, 0:tq, :]
        for h in range(1, nh):
            out = jnp.where(lane_head == h, acc_scr[s, g, h * tq:(h + 1) * tq, :], out)
        o_ref[0, s * tq:(s + 1) * tq, lanes[g]] = out


def _sb_attn(q, k, v, ntri, tq):
    b, l, w = q.shape
    ns, ng = SB_QBLOCKS_PER_STEP, SB_GROUPS_PER_STEP
    gw = ng * V7X_LANES
    assert l % (ns * tq) == 0 and w % gw == 0
    m = (V7X_LANES // SB_HEAD_DIM) * tq
    blk = pltpu.VMEM((ns, ng, SB_SLOTS, m, tq), F32)
    nbytes = (2 * (2 * l * gw * 2 + ns * tq * gw * 2 + ns * tq * gw * 4 + tq * tq * 2)
              + ns * ng * (3 * SB_SLOTS * m * tq * 4 + 3 * m * V7X_LANES * 4) + 6 * m * tq * 4)
    return pl.pallas_call(
        functools.partial(_sb_attn_kernel, tq=tq),
        grid=(b, w // gw, l // (ns * tq)),
        in_specs=[pl.BlockSpec((1, ns * tq, gw), lambda bb, hp, i: (bb, i, hp)),
                  pl.BlockSpec((1, l, gw), lambda bb, hp, i: (bb, 0, hp)),
                  pl.BlockSpec((1, l, gw), lambda bb, hp, i: (bb, 0, hp)),
                  pl.BlockSpec((tq, tq), lambda bb, hp, i: (0, 0))],
        out_specs=pl.BlockSpec((1, ns * tq, gw), lambda bb, hp, i: (bb, i, hp)),
        out_shape=jax.ShapeDtypeStruct((b, l, w), F32),
        scratch_shapes=[pltpu.VMEM((ns, ng, m, V7X_LANES), BF16), blk, blk, blk,
                        pltpu.VMEM((ns, ng, m, 1), F32), pltpu.VMEM((ns, ng, m, V7X_LANES), F32)],
        compiler_params=_params(("parallel", "parallel", "arbitrary"), nbytes),
        name="sb_attn",
    )(q, k, v, ntri)


def _conv_shift(hbuf, sh_scr, tc):
    span = tc + CONV_HALO - V7X_SUBLANES
    for ph in range(1, V7X_SUBLANES):
        sh_scr[ph - 1] = hbuf[ph:ph + span, :]


def _conv_taps(hbuf, sh_scr, dww_ref, acc, taps, tc):
    off = CONV_HALO - (CONV_WIDTH - 1)
    for j in taps:
        ph = (off + j) % V7X_SUBLANES
        base = off + j - ph
        tap = hbuf[base:base + tc, :] if ph == 0 else sh_scr[ph - 1, base:base + tc, :]
        acc = acc + dww_ref[j:j + 1, :] * tap
    return acc


def _conv_tail(acc, lng_ref, lnb_ref, pw_ref, bg):
    mu = jnp.mean(acc, axis=-1, keepdims=True)
    cen = acc - mu
    var = jnp.mean(cen * cen, axis=-1, keepdims=True)
    y = cen * lax.rsqrt(var + EPS) * lng_ref[...] + lnb_ref[...]
    y = y * _sigmoid(y)
    o = _wdot(y.astype(BF16), pw_ref[...])
    return _rms_rows(o, bg).astype(BF16)


def _ssm_operators(lam_re, lam_im, log_dt, b_re, b_im, c_re, c_im):
    lr, li = lam_re.astype(F32), lam_im.astype(F32)
    dt = jnp.exp(log_dt.astype(F32))[:, None]
    mag = jnp.exp(lr * dt)
    ar, ai = mag * jnp.cos(li * dt), mag * jnp.sin(li * dt)
    den = lr * lr + li * li
    fr = ((ar - 1.0) * lr + ai * li) / den
    fi = (ai * lr - (ar - 1.0) * li) / den
    br, bi = b_re.astype(F32), b_im.astype(F32)
    bbr = fr[..., None] * br - fi[..., None] * bi
    bbi = fr[..., None] * bi + fi[..., None] * br
    cr, ci = c_re.astype(F32), c_im.astype(F32)
    eye = jnp.eye(SSM_GROUPS, dtype=F32)
    gp = SSM_GROUPS * SSM_STATE

    def rows_gh(w):
        return jnp.einsum('gph,gk->ghkp', w, eye).reshape(SSM_CH, gp)

    def rows_gp(w):
        return jnp.einsum('ghp,gk->gpkh', w, eye).reshape(gp, SSM_CH)

    b_op = jnp.concatenate([rows_gh(bbr), rows_gh(bbi)], axis=1)
    c_op = jnp.concatenate([rows_gp(cr), rows_gp(-ci)], axis=0)
    n = jnp.arange(1, SSM_CHUNK + 1, dtype=F32)[:, None]
    pmag = jnp.exp(n * (lr * dt).reshape(1, gp))
    ang = n * (li * dt).reshape(1, gp)
    return b_op.astype(BF16), c_op.astype(BF16), pmag * jnp.cos(ang), pmag * jnp.sin(ang)


def _ssm_kernel(u_ref, perm_ref, permt_ref, b_ref, c_ref, pr_ref, pi_ref, d_ref, gw_ref, bgn_ref, o_ref,
                xr_scr, xi_scr, er_scr, ei_scr, cr_scr, ci_scr, *, tm):
    gp = SSM_GROUPS * SSM_STATE
    nchunk = tm // SSM_CHUNK

    @pl.when(pl.program_id(0) == 0)
    def _():
        cr_scr[...] = jnp.zeros_like(cr_scr)
        ci_scr[...] = jnp.zeros_like(ci_scr)

    def pos(s):
        return pl.ds(s * nchunk, nchunk)

    def inject(b):
        up = _dot(perm_ref[...], u_ref[b].astype(BF16)).astype(BF16)
        bu = _dot(up, b_ref[...])
        xr_scr[b] = bu[:, :gp]
        xi_scr[b] = bu[:, gp:]

    def scan(b):
        ar, ai = pr_ref[0:1, :], pi_ref[0:1, :]
        xr, xi = xr_scr[b, pos(0), :], xi_scr[b, pos(0), :]
        for s in range(1, SSM_CHUNK):
            xr, xi = (ar * xr - ai * xi + xr_scr[b, pos(s), :], ar * xi + ai * xr + xi_scr[b, pos(s), :])
            xr_scr[b, pos(s), :] = xr
            xi_scr[b, pos(s), :] = xi
        nr, ni = pr_ref[SSM_CHUNK - 1:SSM_CHUNK, :], pi_ref[SSM_CHUNK - 1:SSM_CHUNK, :]
        er, ei = cr_scr[b], ci_scr[b]
        for c in range(nchunk):
            er_scr[b, c:c + 1, :] = er
            ei_scr[b, c:c + 1, :] = ei
            er, ei = (nr * er - ni * ei + xr[c:c + 1, :], nr * ei + ni * er + xi[c:c + 1, :])
        cr_scr[b] = er
        ci_scr[b] = ei
        er, ei = er_scr[b], ei_scr[b]
        for s in range(SSM_CHUNK):
            sr, si = pr_ref[s:s + 1, :], pi_ref[s:s + 1, :]
            xr_scr[b, pos(s), :] += sr * er - si * ei
            xi_scr[b, pos(s), :] += sr * ei + si * er

    def readout(b):
        yp = _dot(xr_scr[b].astype(BF16), c_ref[0:gp, :]) + _dot(xi_scr[b].astype(BF16), c_ref[gp:, :])
        hi = yp.astype(BF16)
        lo = (yp - hi.astype(F32)).astype(BF16)
        y = _dot(permt_ref[...], hi) + _dot(permt_ref[...], lo) + d_ref[...] * u_ref[b]
        z = _wdot(y.astype(BF16), gw_ref[...])
        bg = bgn_ref[:, SB_WIDTH + CONV_CH:]
        o_ref[b] = _rms_rows(z[:, :SSM_CH] * _sigmoid(z[:, SSM_CH:]), bg).astype(BF16)

    nb = u_ref.shape[0]
    for b in range(nb):
        inject(b)
    for b in range(nb):
        scan(b)
        readout(b)


def _ssm_branch(u, layer, ops, d, gw_bf, bgn, tm):
    b_op, c_op, pow_r, pow_i = ops
    b, l, ch = u.shape
    gp = SSM_GROUPS * SSM_STATE
    nchunk = tm // SSM_CHUNK
    full = lambda a: pl.BlockSpec(a.shape, lambda i: (0,) * a.ndim)
    r = jnp.arange(tm)
    perm = (r[None, :] == ((r % nchunk) * SSM_CHUNK + r // nchunk)[:, None]).astype(BF16)
    perm_t = perm.T
    nbytes = (2 * (2 * b * tm * ch * 4 + 2 * tm * tm * 2 + 2 * ch * gp * 2 * 2 + 2 * SSM_CHUNK * gp * 4 + ch * 2 * ch * 2)
              + b * (2 * tm * gp * 4 + 2 * nchunk * gp * 4 + 2 * tm * 2 * gp * 4))
    return pl.pallas_call(
        functools.partial(_ssm_kernel, tm=tm),
        grid=(l // tm,),
        in_specs=[pl.BlockSpec((b, tm, ch), lambda i: (0, i, 0)), full(perm), full(perm_t)]
        + [_layer_block(a, layer) for a in (b_op, c_op, pow_r, pow_i, d, gw_bf, bgn)],
        out_specs=pl.BlockSpec((b, tm, ch), lambda i: (0, i, 0)),
        out_shape=jax.ShapeDtypeStruct((b, l, ch), BF16),
        scratch_shapes=[pltpu.VMEM((b, tm, gp), F32), pltpu.VMEM((b, tm, gp), F32),
                        pltpu.VMEM((b, nchunk, gp), F32), pltpu.VMEM((b, nchunk, gp), F32),
                        pltpu.VMEM((b, 1, gp), F32), pltpu.VMEM((b, 1, gp), F32)],
        compiler_params=_params(("arbitrary",), nbytes),
        name="ssm_branch",
    )(u, perm, perm_t, b_op, c_op, pow_r, pow_i, d, gw_bf, bgn)


def _mix_out(x, sb, cv, sm, bg, w_ref):
    s1 = SB_WIDTH
    s2 = SB_WIDTH + CONV_CH
    y = _wdot(_rms_rows(sb, bg).astype(BF16), w_ref[0:s1, :])
    y = y + _wdot(cv, w_ref[s1:s2, :])
    y = y + _wdot(sm, w_ref[s2:, :])
    return x + y


def _mem_kv_kernel(m_ref, g_ref, wk_ref, wv_ref, kg_ref, k_ref, v_ref):
    hm = _rms_rows(m_ref[0], g_ref[...]).astype(BF16)
    kk = _wdot(hm, wk_ref[...])
    for hh in range(XA_HEADS):
        hs = slice(hh * XA_HEAD_DIM, (hh + 1) * XA_HEAD_DIM)
        k_ref[0, :, hs] = _rms_rows(kk[:, hs], kg_ref[...]).astype(BF16)
    v_ref[0] = _wdot(hm, wv_ref[...]).astype(BF16)


def _mem_kv(mem, layer, g, wk_bf, wv_bf, kg):
    b, n, d = mem.shape
    blk = pl.BlockSpec((1, n, d), lambda bb: (bb, 0, 0))
    nbytes = 2 * (n * d * 4 + 2 * d * d * 2 + 2 * n * d * 2) + 4 * n * d * 4
    return pl.pallas_call(
        _mem_kv_kernel,
        grid=(b,),
        in_specs=[blk] + [_layer_block(a, layer) for a in (g, wk_bf, wv_bf, kg)],
        out_specs=[blk, blk],
        out_shape=[jax.ShapeDtypeStruct((b, n, d), BF16), jax.ShapeDtypeStruct((b, n, d), BF16)],
        compiler_params=_params(("parallel",), nbytes),
        name="mem_kv",
    )(mem, g, wk_bf, wv_bf, kg)


def _xattn_kernel(x_ref, sb_ref, cv_ref, sm_ref, bgn_ref, wm_ref, g_ref, wq_ref, qg_ref, wo_ref, k_ref, v_ref,
                  o_ref, ob_scr):
    x = _mix_out(x_ref[0], sb_ref[0], cv_ref[0], sm_ref[0], bgn_ref[:, :SB_WIDTH], wm_ref)
    hx = _rms_rows(x, g_ref[...]).astype(BF16)
    q = _wdot(hx, wq_ref[...])
    scale = XA_HEAD_DIM ** -0.5
    for hh in range(XA_HEADS):
        hs = slice(hh * XA_HEAD_DIM, (hh + 1) * XA_HEAD_DIM)
        qh = (_rms_rows(q[:, hs], qg_ref[...]) * scale).astype(BF16)
        s = _dot_nt(qh, k_ref[0, :, hs])
        s = s - jnp.max(s, axis=-1, keepdims=True)
        e = jnp.exp(s)
        p = e / jnp.sum(e, axis=-1, keepdims=True)
        ob_scr[:, hs] = _dot(p.astype(BF16), v_ref[0, :, hs]).astype(BF16)
    o_ref[0] = x + _wdot(ob_scr[...], wo_ref[...])


def _mix_xattn(x, sb, cv, sm, layer, bgn, wm_bf, g, wq_bf, qg, wo_bf, k_bf, v_bf, tm):
    b, l, d = x.shape
    n = k_bf.shape[1]
    rows = lambda w: pl.BlockSpec((1, tm, w), lambda bb, i: (bb, i, 0))
    kv = pl.BlockSpec((1, n, d), lambda bb, i: (bb, 0, 0))
    nbytes = (2 * (2 * tm * d * 4 + tm * SB_WIDTH * 4 + tm * (CONV_CH + SSM_CH) * 2 + 3 * d * d * 2 + 2 * n * d * 2)
              + 5 * tm * d * 4)
    return pl.pallas_call(
        _xattn_kernel,
        grid=(b, l // tm),
        in_specs=[rows(d), rows(SB_WIDTH), rows(CONV_CH), rows(SSM_CH)]
        + [_layer_block(a, layer) for a in (bgn, wm_bf, g, wq_bf, qg, wo_bf)] + [kv, kv],
        out_specs=rows(d),
        out_shape=jax.ShapeDtypeStruct((b, l, d), F32),
        scratch_shapes=[pltpu.VMEM((tm, d), BF16)],
        compiler_params=_params(("parallel", "parallel"), nbytes),
        name="mix_xattn",
    )(x, sb, cv, sm, bgn, wm_bf, g, wq_bf, qg, wo_bf, k_bf, v_bf)


def _ffn_kernel(x_ref, g_ref, wi_hbm, wo_hbm, o_ref, wi_scr, wo_scr, sem, *, th, layer):
    hidden = wo_scr.shape[0]
    slices = [(c0, min(c0 + th, hidden)) for c0 in range(0, hidden, th)]

    def copies(n):
        c0, c1 = slices[n]
        return (pltpu.make_async_copy(wi_hbm.at[layer, :, c0:c1], wi_scr.at[:, c0:c1], sem.at[0, n]),
                pltpu.make_async_copy(wi_hbm.at[layer, :, hidden + c0:hidden + c1],
                                      wi_scr.at[:, hidden + c0:hidden + c1], sem.at[1, n]),
                pltpu.make_async_copy(wo_hbm.at[layer, c0:c1, :], wo_scr.at[c0:c1, :], sem.at[2, n]))

    def body(first_step):
        x = x_ref[...]
        h = _rms_rows(x, g_ref[...]).astype(BF16)
        o_ref[...] = x
        for n, (c0, c1) in enumerate(slices):
            if first_step:
                for cp in copies(n):
                    cp.wait()
            gate = _wdot(h, wi_scr[:, c0:c1])
            up = _wdot(h, wi_scr[:, hidden + c0:hidden + c1])
            act = (gate * _sigmoid(gate) * up).astype(BF16)
            o_ref[...] += _wdot(act, wo_scr[c0:c1, :])

    @pl.when(pl.program_id(0) == 0)
    def _():
        for n in range(len(slices)):
            for cp in copies(n):
                cp.start()
        body(True)

    @pl.when(pl.program_id(0) > 0)
    def _():
        body(False)


def _ffn(x2, layer, g, w_in, w_out, tm, th):
    t, d = x2.shape
    hidden = w_out.shape[1]
    n_slices = -(-hidden // th)
    nbytes = 3 * d * hidden * 4 + 2 * 2 * tm * d * 4 + tm * d * 2 + 6 * tm * th * 4
    return pl.pallas_call(
        functools.partial(_ffn_kernel, th=th, layer=layer),
        grid=(t // tm,),
        in_specs=[pl.BlockSpec((tm, d), lambda i: (i, 0)), _layer_block(g, layer),
                  pl.BlockSpec(memory_space=pl.ANY), pl.BlockSpec(memory_space=pl.ANY)],
        out_specs=pl.BlockSpec((tm, d), lambda i: (i, 0)),
        out_shape=jax.ShapeDtypeStruct((t, d), F32),
        scratch_shapes=[pltpu.VMEM((d, 2 * hidden), F32), pltpu.VMEM((hidden, d), F32),
                        pltpu.SemaphoreType.DMA((3, n_slices))],
        compiler_params=_params(("arbitrary",), nbytes),
        name="ffn",
    )(x2, g, w_in, w_out)


def _tile(n, want):
    want = min(want, n)
    for cand in range(want, 0, -1):
        if n % cand == 0 and (cand % V7X_SUBLANES == 0 or cand == n):
            return cand
    return n


def kernel(x, mem, norm_mix_g, w_in, sb_q_norm_g, sb_k_norm_g, conv_dw_w, conv_dw_b, conv_ln_g, conv_ln_b, conv_pw2_w, ssm_lam_re, ssm_lam_im, ssm_log_dt, ssm_b_re, ssm_b_im, ssm_c_re, ssm_c_im, ssm_d, ssm_glu_w, branch_norm_g, w_out, norm_xa_g, norm_mem_g, xa_wq, xa_wk, xa_wv, xa_q_norm_g, xa_k_norm_g, xa_wo, norm_ffn_g, ffn_w_in, ffn_w_out):
    bsz, seq, d = x.shape
    depth = w_in.shape[0]
    t = bsz * seq
    tm = _tile(seq, 512)
    tx = _tile(seq, 1024)
    tq = _tile(seq, 256)
    ts = _tile(seq, 512)
    th = 256
    assert ts % (SSM_CHUNK * V7X_SUBLANES) == 0 and tm >= CONV_HALO
    vec = lambda a: a.astype(F32).reshape(depth, 1, -1)
    qg = vec(jnp.tile(sb_q_norm_g, (1, SB_HEADS))) * (SB_HEAD_DIM ** -0.5)
    kg = vec(jnp.tile(sb_k_norm_g, (1, SB_HEADS)))
    conv = (conv_dw_w.astype(F32), vec(conv_dw_b), vec(conv_ln_g), vec(conv_ln_b), conv_pw2_w.astype(F32))
    ops = jax.vmap(_ssm_operators)(ssm_lam_re, ssm_lam_im, ssm_log_dt, ssm_b_re, ssm_b_im, ssm_c_re, ssm_c_im)
    bgn = vec(branch_norm_g)

    seg = (jnp.arange(SEG_WIDTH)[:, None] // SB_HEAD_DIM == jnp.arange(SEG_WIDTH)[None, :] // SB_HEAD_DIM).astype(BF16)
    ntri = -(jnp.arange(tq)[:, None] > jnp.arange(tq)[None, :]).astype(BF16)

    x2 = x.reshape(t, d)
    for l in range(depth):
        q, k, v, o_conv, u = _mix_in(x2, seq, l, vec(norm_mix_g), w_in.astype(F32), qg, kg, seg, conv, bgn, tx)
        o_conv = o_conv.reshape(bsz, seq, CONV_CH)
        o_sb = _sb_attn(q.reshape(bsz, seq, SB_WIDTH), k.reshape(bsz, seq, SB_WIDTH),
                        v.reshape(bsz, seq, SB_WIDTH), ntri, tq)
        o_ssm = _ssm_branch(u.reshape(bsz, seq, SSM_CH), l, ops, vec(ssm_d), ssm_glu_w.astype(F32), bgn, ts)
        k_m, v_m = _mem_kv(mem, l, vec(norm_mem_g), xa_wk.astype(F32), xa_wv.astype(F32), vec(xa_k_norm_g))
        x2 = _mix_xattn(x2.reshape(bsz, seq, d), o_sb, o_conv, o_ssm, l, bgn, w_out.astype(F32), vec(norm_xa_g),
                        xa_wq.astype(F32), vec(xa_q_norm_g), xa_wo.astype(F32), k_m, v_m, tx).reshape(t, d)
        x2 = _ffn(x2, l, vec(norm_ffn_g), ffn_w_in.astype(F32), ffn_w_out.astype(F32), tm, th)
    return x2.reshape(bsz, seq, d)
```

```python
import functools

import jax
import jax.numpy as jnp
from jax import lax
from jax.experimental import pallas as pl
from jax.experimental.pallas import tpu as pltpu

F32 = jnp.float32
BF16 = jnp.bfloat16
EPS = 1e-6

V7X_LANES = 128
V7X_SUBLANES = 8
V7X_VMEM_BYTES = 64 * 1024 * 1024

SB_HEADS = 8
SB_HEAD_DIM = 64
SB_WIDTH = SB_HEADS * SB_HEAD_DIM
SEG_WIDTH = 256
CONV_CH = 256
CONV_WIDTH = 31
CONV_HALO = 32
SSM_CH = 256
SSM_GROUP = 16
SSM_GROUPS = SSM_CH // SSM_GROUP
SSM_STATE = 64
SSM_CHUNK = 32
XA_HEADS = 4
XA_HEAD_DIM = 256


def _vmem_limit(nbytes):
    return int(min(max(nbytes * 3 // 2, 16 * 1024 * 1024), V7X_VMEM_BYTES - 8 * 1024 * 1024))


def _params(semantics, nbytes):
    return pltpu.CompilerParams(dimension_semantics=semantics, vmem_limit_bytes=_vmem_limit(nbytes))


def _dot(a, b):
    return jnp.dot(a, b, preferred_element_type=F32)


def _wdot(a, w):
    return jnp.dot(a, w.astype(BF16), preferred_element_type=F32)


def _dot_nt(a, b):
    return lax.dot_general(a, b, (((1,), (1,)), ((), ())), preferred_element_type=F32)


def _rms_rows(xf, g):
    return xf * lax.rsqrt(jnp.mean(xf * xf, axis=-1, keepdims=True) + EPS) * g


def _sigmoid(x):
    return 1.0 / (1.0 + jnp.exp(-x))


def _segment_mean_sq(p, seg):
    sq = p * p
    hi = sq.astype(BF16)
    lo = (sq - hi.astype(F32)).astype(BF16)
    parts = []
    for c0 in range(0, p.shape[1], SEG_WIDTH):
        cols = slice(c0, c0 + SEG_WIDTH)
        parts.append(_dot(hi[:, cols], seg) + _dot(lo[:, cols], seg))
    return jnp.concatenate(parts, axis=1) * (1.0 / SB_HEAD_DIM)


def _mix_in_kernel(x_ref, g_ref, w_ref, qg_ref, kg_ref, seg_ref, dww_ref, dwb_ref, lng_ref, lnb_ref, pw_ref, bgn_ref,
                   q_ref, k_ref, v_ref, cv_ref, u_ref, hbuf, sh_scr, *, tm, tiles_per_seq):
    i = pl.program_id(0)

    @pl.when(i == 0)
    def _():
        hbuf[...] = jnp.zeros_like(hbuf)

    h = _rms_rows(x_ref[...], g_ref[...]).astype(BF16)
    s1, s2, s3 = SB_WIDTH, 2 * SB_WIDTH, 3 * SB_WIDTH
    s4 = s3 + 2 * CONV_CH
    seg = seg_ref[...]
    pc = _wdot(h, w_ref[:, s3:s4])
    history = hbuf[tm:tm + CONV_HALO, :]
    hbuf[:CONV_HALO, :] = jnp.where(i % tiles_per_seq == 0, 0.0, history)
    hbuf[CONV_HALO:, :] = pc[:, :CONV_CH] * _sigmoid(pc[:, CONV_CH:])
    quarter = -(-CONV_WIDTH // 4)
    taps = [range(n * quarter, min((n + 1) * quarter, CONV_WIDTH)) for n in range(4)]
    pq = _wdot(h, w_ref[:, 0:s1])
    _conv_shift(hbuf, sh_scr, tm)
    acc = _conv_taps(hbuf, sh_scr, dww_ref, jnp.zeros((tm, CONV_CH), F32) + dwb_ref[...], taps[0], tm)
    q_ref[...] = (pq * lax.rsqrt(_segment_mean_sq(pq, seg) + EPS) * qg_ref[...]).astype(BF16)
    pk = _wdot(h, w_ref[:, s1:s2])
    acc = _conv_taps(hbuf, sh_scr, dww_ref, acc, taps[1], tm)
    k_ref[...] = (pk * lax.rsqrt(_segment_mean_sq(pk, seg) + EPS) * kg_ref[...]).astype(BF16)
    pv = _wdot(h, w_ref[:, s2:s3])
    acc = _conv_taps(hbuf, sh_scr, dww_ref, acc, taps[2], tm)
    v_ref[...] = pv.astype(BF16)
    pu = _wdot(h, w_ref[:, s4:])
    acc = _conv_taps(hbuf, sh_scr, dww_ref, acc, taps[3], tm)
    u_ref[...] = pu
    cv_ref[...] = _conv_tail(acc, lng_ref, lnb_ref, pw_ref, bgn_ref[:, SB_WIDTH:SB_WIDTH + CONV_CH])


def _layer_block(a, layer):
    tail = a.shape[1:]
    return pl.BlockSpec((None,) + tail, lambda *_: (layer,) + (0,) * len(tail), pipeline_mode=pl.Buffered(1))


def _mix_in(x2, seq, layer, g, w_bf, qg, kg, seg, conv, bgn, tm):
    t, d = x2.shape
    n_in = w_bf.shape[2]
    row = lambda i: (i, 0)
    hist = tm + CONV_HALO
    nbytes = (2 * (tm * d * 4 + d * n_in * 2 + tm * (3 * SB_WIDTH * 2 + CONV_CH * 2 + SSM_CH * 4)) + tm * n_in * 4
              + hist * CONV_CH * 4 * 14)
    return pl.pallas_call(
        functools.partial(_mix_in_kernel, tm=tm, tiles_per_seq=seq // tm),
        grid=(t // tm,),
        in_specs=[pl.BlockSpec((tm, d), row)]
        + [_layer_block(a, layer) for a in (g, w_bf, qg, kg)]
        + [pl.BlockSpec((SEG_WIDTH, SEG_WIDTH), lambda i: (0, 0))]
        + [_layer_block(a, layer) for a in (*conv, bgn)],
        out_specs=[pl.BlockSpec((tm, SB_WIDTH), row), pl.BlockSpec((tm, SB_WIDTH), row),
                   pl.BlockSpec((tm, SB_WIDTH), row), pl.BlockSpec((tm, CONV_CH), row),
                   pl.BlockSpec((tm, SSM_CH), row)],
        out_shape=[jax.ShapeDtypeStruct((t, SB_WIDTH), BF16), jax.ShapeDtypeStruct((t, SB_WIDTH), BF16),
                   jax.ShapeDtypeStruct((t, SB_WIDTH), BF16), jax.ShapeDtypeStruct((t, CONV_CH), BF16),
                   jax.ShapeDtypeStruct((t, SSM_CH), F32)],
        scratch_shapes=[pltpu.VMEM((hist, CONV_CH), F32),
                        pltpu.VMEM((V7X_SUBLANES - 1, hist - V7X_SUBLANES, CONV_CH), F32)],
        compiler_params=_params(("arbitrary",), nbytes),
        name="mix_in",
    )(x2, g, w_bf, qg, kg, seg, *conv, bgn)


SB_SLOTS = 2
SB_GROUPS_PER_STEP = 2
SB_QBLOCKS_PER_STEP = 2
SB_SOFTPLUS_LINEAR = 40.0
SB_EXP_UNDERFLOW = -104.0
SB_NEVER = -1e30


def _sb_decay(z, run, ntri, mask):
    sp = jnp.maximum(jnp.log(1.0 + jnp.exp(jnp.minimum(z, SB_SOFTPLUS_LINEAR))), z)
    if mask is not None:
        sp = jnp.where(mask, sp, 0.0)
    later = _dot(sp.astype(BF16), ntri)
    return z - sp + run, later, run + later[:, 0:1] - sp[:, 0:1]


def _sb_weights(t0, later, mask):
    w = jnp.exp(t0 + later)
    if mask is not None:
        w = jnp.where(mask, w, 0.0)
    return w.astype(BF16)


def _sb_attn_kernel(q_ref, k_ref, v_ref, ntri_ref, o_ref, q2_scr, z_scr, t0_scr, lat_scr, run_scr, acc_scr, *, tq):
    step = pl.program_id(2)
    nh = V7X_LANES // SB_HEAD_DIM
    m = nh * tq
    lanes = [slice(g * V7X_LANES, (g + 1) * V7X_LANES) for g in range(SB_GROUPS_PER_STEP)]
    qblocks = [step * SB_QBLOCKS_PER_STEP + s for s in range(SB_QBLOCKS_PER_STEP)]
    units = [(s, g) for s in range(SB_QBLOCKS_PER_STEP) for g in range(SB_GROUPS_PER_STEP)]

    def rows(s, j):
        i = qblocks[s]
        return pl.ds(pl.multiple_of((i - jnp.minimum(j, i)) * tq, tq), tq)

    def scores(us, j, slot):
        for s, g in us:
            z_scr[s, g, slot] = _dot_nt(q2_scr[s, g], k_ref[0, rows(s, j), lanes[g]])

    def decay(us, slot, mask=None, bias=None):
        for s, g in us:
            run = run_scr[s, g] if bias is None else run_scr[s, g] + bias[s]
            t0, later, run = _sb_decay(z_scr[s, g, slot], run, ntri_ref[...], mask)
            t0_scr[s, g, slot] = t0
            lat_scr[s, g, slot] = later
            run_scr[s, g] = run

    def output(us, j, slot, mask=None):
        for s, g in us:
            w = _sb_weights(t0_scr[s, g, slot], lat_scr[s, g, slot], mask)
            acc_scr[s, g] += _dot(w, v_ref[0, rows(s, j), lanes[g]])

    def live(us):
        run = run_scr[us[0]]
        for u in us[1:]:
            run = jnp.maximum(run, run_scr[u])
        return (jnp.max(run) >= SB_EXP_UNDERFLOW).astype(jnp.int32)

    lane_head = lax.broadcasted_iota(jnp.int32, (tq, V7X_LANES), 1) // SB_HEAD_DIM
    for s, g in units:
        q = q_ref[0, s * tq:(s + 1) * tq, lanes[g]]
        for h in range(nh):
            q2_scr[s, g, h * tq:(h + 1) * tq, :] = jnp.where(lane_head == h, q, jnp.zeros_like(q))
    row = lax.broadcasted_iota(jnp.int32, (m, tq), 0) % tq
    col = lax.broadcasted_iota(jnp.int32, (m, tq), 1)
    diag_mask = col < row
    run_scr[...] = jnp.zeros_like(run_scr)
    acc_scr[...] = jnp.zeros_like(acc_scr)

    scores(units, 0, 0)
    scores(units, 1, 1)
    decay(units, 0, mask=diag_mask)
    decay(units, 1, bias=[jnp.where(i >= 1, 0.0, SB_NEVER) for i in qblocks])
    output(units, 0, 0, diag_mask)
    output(units, 1, 1)

    for s, i in enumerate(qblocks):
        us = [u for u in units if u[0] == s]

        @pl.when(jnp.logical_and(i >= 2, live(us) > 0))
        def _(us=us, i=i):
            scores(us, 2, 0)
            decay(us, 0)
            scores(us, 3, 1)
            more = live(us)
            output(us, 2, 0)
            scores(us, 4, 0)
            decay(us, 1)
            n_pairs = (i - 2) // 2

            def cond(carry):
                p, more = carry
                return jnp.logical_and(p < n_pairs, more > 0)

            def body(carry):
                p, _ = carry
                t = 2 * p + 3
                scores(us, t + 2, 1)
                output(us, t, 1)
                decay(us, 0)
                more = live(us)
                scores(us, t + 3, 0)
                output(us, t + 1, 0)
                decay(us, 1)
                return p + 1, more

            _, more = lax.while_loop(cond, body, (0, more))

            @pl.when(jnp.logical_and(more > 0, (i - 2) % 2 == 1))
            def _():
                output(us, i, 1)

    for s, g in units:
        out = acc_scr[s, g, 0:tq, :]
        for h in range(1, nh):
            out = jnp.where(lane_head == h, acc_scr[s, g, h * tq:(h + 1) * tq, :], out)
        o_ref[0, s * tq:(s + 1) * tq, lanes[g]] = out


def _sb_attn(q, k, v, ntri, tq):
    b, l, w = q.shape
    ns, ng = SB_QBLOCKS_PER_STEP, SB_GROUPS_PER_STEP
    gw = ng * V7X_LANES
    assert l % (ns * tq) == 0 and w % gw == 0
    m = (V7X_LANES // SB_HEAD_DIM) * tq
    blk = pltpu.VMEM((ns, ng, SB_SLOTS, m, tq), F32)
    nbytes = (2 * (2 * l * gw * 2 + ns * tq * gw * 2 + ns * tq * gw * 4 + tq * tq * 2)
              + ns * ng * (3 * SB_SLOTS * m * tq * 4 + 3 * m * V7X_LANES * 4) + 6 * m * tq * 4)
    return pl.pallas_call(
        functools.partial(_sb_attn_kernel, tq=tq),
        grid=(b, w // gw, l // (ns * tq)),
        in_specs=[pl.BlockSpec((1, ns * tq, gw), lambda bb, hp, i: (bb, i, hp)),
                  pl.BlockSpec((1, l, gw), lambda bb, hp, i: (bb, 0, hp)),
                  pl.BlockSpec((1, l, gw), lambda bb, hp, i: (bb, 0, hp)),
                  pl.BlockSpec((tq, tq), lambda bb, hp, i: (0, 0))],
        out_specs=pl.BlockSpec((1, ns * tq, gw), lambda bb, hp, i: (bb, i, hp)),
        out_shape=jax.ShapeDtypeStruct((b, l, w), F32),
        scratch_shapes=[pltpu.VMEM((ns, ng, m, V7X_LANES), BF16), blk, blk, blk,
                        pltpu.VMEM((ns, ng, m, 1), F32), pltpu.VMEM((ns, ng, m, V7X_LANES), F32)],
        compiler_params=_params(("parallel", "parallel", "arbitrary"), nbytes),
        name="sb_attn",
    )(q, k, v, ntri)


def _conv_shift(hbuf, sh_scr, tc):
    span = tc + CONV_HALO - V7X_SUBLANES
    for ph in range(1, V7X_SUBLANES):
        sh_scr[ph - 1] = hbuf[ph:ph + span, :]


def _conv_taps(hbuf, sh_scr, dww_ref, acc, taps, tc):
    off = CONV_HALO - (CONV_WIDTH - 1)
    for j in taps:
        ph = (off + j) % V7X_SUBLANES
        base = off + j - ph
        tap = hbuf[base:base + tc, :] if ph == 0 else sh_scr[ph - 1, base:base + tc, :]
        acc = acc + dww_ref[j:j + 1, :] * tap
    return acc


def _conv_tail(acc, lng_ref, lnb_ref, pw_ref, bg):
    mu = jnp.mean(acc, axis=-1, keepdims=True)
    cen = acc - mu
    var = jnp.mean(cen * cen, axis=-1, keepdims=True)
    y = cen * lax.rsqrt(var + EPS) * lng_ref[...] + lnb_ref[...]
    y = y * _sigmoid(y)
    o = _wdot(y.astype(BF16), pw_ref[...])
    return _rms_rows(o, bg).astype(BF16)


def _ssm_operators(lam_re, lam_im, log_dt, b_re, b_im, c_re, c_im):
    lr, li = lam_re.astype(F32), lam_im.astype(F32)
    dt = jnp.exp(log_dt.astype(F32))[:, None]
    mag = jnp.exp(lr * dt)
    ar, ai = mag * jnp.cos(li * dt), mag * jnp.sin(li * dt)
    den = lr * lr + li * li
    fr = ((ar - 1.0) * lr + ai * li) / den
    fi = (ai * lr - (ar - 1.0) * li) / den
    br, bi = b_re.astype(F32), b_im.astype(F32)
    bbr = fr[..., None] * br - fi[..., None] * bi
    bbi = fr[..., None] * bi + fi[..., None] * br
    cr, ci = c_re.astype(F32), c_im.astype(F32)
    eye = jnp.eye(SSM_GROUPS, dtype=F32)
    gp = SSM_GROUPS * SSM_STATE

    def rows_gh(w):
        return jnp.einsum('gph,gk->ghkp', w, eye).reshape(SSM_CH, gp)

    def rows_gp(w):
        return jnp.einsum('ghp,gk->gpkh', w, eye).reshape(gp, SSM_CH)

    b_op = jnp.concatenate([rows_gh(bbr), rows_gh(bbi)], axis=1)
    c_op = jnp.concatenate([rows_gp(cr), rows_gp(-ci)], axis=0)
    n = jnp.arange(1, SSM_CHUNK + 1, dtype=F32)[:, None]
    pmag = jnp.exp(n * (lr * dt).reshape(1, gp))
    ang = n * (li * dt).reshape(1, gp)
    return b_op.astype(BF16), c_op.astype(BF16), pmag * jnp.cos(ang), pmag * jnp.sin(ang)


def _ssm_kernel(u_ref, perm_ref, permt_ref, b_ref, c_ref, pr_ref, pi_ref, d_ref, gw_ref, bgn_ref, o_ref,
                xr_scr, xi_scr, er_scr, ei_scr, cr_scr, ci_scr, *, tm):
    gp = SSM_GROUPS * SSM_STATE
    nchunk = tm // SSM_CHUNK

    @pl.when(pl.program_id(0) == 0)
    def _():
        cr_scr[...] = jnp.zeros_like(cr_scr)
        ci_scr[...] = jnp.zeros_like(ci_scr)

    def pos(s):
        return pl.ds(s * nchunk, nchunk)

    def inject(b):
        up = _dot(perm_ref[...], u_ref[b].astype(BF16)).astype(BF16)
        bu = _dot(up, b_ref[...])
        xr_scr[b] = bu[:, :gp]
        xi_scr[b] = bu[:, gp:]

    def scan(b):
        ar, ai = pr_ref[0:1, :], pi_ref[0:1, :]
        xr, xi = xr_scr[b, pos(0), :], xi_scr[b, pos(0), :]
        for s in range(1, SSM_CHUNK):
            xr, xi = (ar * xr - ai * xi + xr_scr[b, pos(s), :], ar * xi + ai * xr + xi_scr[b, pos(s), :])
            xr_scr[b, pos(s), :] = xr
            xi_scr[b, pos(s), :] = xi
        nr, ni = pr_ref[SSM_CHUNK - 1:SSM_CHUNK, :], pi_ref[SSM_CHUNK - 1:SSM_CHUNK, :]
        er, ei = cr_scr[b], ci_scr[b]
        for c in range(nchunk):
            er_scr[b, c:c + 1, :] = er
            ei_scr[b, c:c + 1, :] = ei
            er, ei = (nr * er - ni * ei + xr[c:c + 1, :], nr * ei + ni * er + xi[c:c + 1, :])
        cr_scr[b] = er
        ci_scr[b] = ei
        er, ei = er_scr[b], ei_scr[b]
        for s in range(SSM_CHUNK):
            sr, si = pr_ref[s:s + 1, :], pi_ref[s:s + 1, :]
            xr_scr[b, pos(s), :] += sr * er - si * ei
            xi_scr[b, pos(s), :] += sr * ei + si * er

    def readout(b):
        yp = _dot(xr_scr[b].astype(BF16), c_ref[0:gp, :]) + _dot(xi_scr[b].astype(BF16), c_ref[gp:, :])
        hi = yp.astype(BF16)
        lo = (yp - hi.astype(F32)).astype(BF16)
        y = _dot(permt_ref[...], hi) + _dot(permt_ref[...], lo) + d_ref[...] * u_ref[b]
        z = _wdot(y.astype(BF16), gw_ref[...])
        bg = bgn_ref[:, SB_WIDTH + CONV_CH:]
        o_ref[b] = _rms_rows(z[:, :SSM_CH] * _sigmoid(z[:, SSM_CH:]), bg).astype(BF16)

    nb = u_ref.shape[0]
    for b in range(nb):
        inject(b)
    for b in range(nb):
        scan(b)
        readout(b)


def _ssm_branch(u, layer, ops, d, gw_bf, bgn, tm):
    b_op, c_op, pow_r, pow_i = ops
    b, l, ch = u.shape
    gp = SSM_GROUPS * SSM_STATE
    nchunk = tm // SSM_CHUNK
    full = lambda a: pl.BlockSpec(a.shape, lambda i: (0,) * a.ndim)
    r = jnp.arange(tm)
    perm = (r[None, :] == ((r % nchunk) * SSM_CHUNK + r // nchunk)[:, None]).astype(BF16)
    perm_t = perm.T
    nbytes = (2 * (2 * b * tm * ch * 4 + 2 * tm * tm * 2 + 2 * ch * gp * 2 * 2 + 2 * SSM_CHUNK * gp * 4 + ch * 2 * ch * 2)
              + b * (2 * tm * gp * 4 + 2 * nchunk * gp * 4 + 2 * tm * 2 * gp * 4))
    return pl.pallas_call(
        functools.partial(_ssm_kernel, tm=tm),
        grid=(l // tm,),
        in_specs=[pl.BlockSpec((b, tm, ch), lambda i: (0, i, 0)), full(perm), full(perm_t)]
        + [_layer_block(a, layer) for a in (b_op, c_op, pow_r, pow_i, d, gw_bf, bgn)],
        out_specs=pl.BlockSpec((b, tm, ch), lambda i: (0, i, 0)),
        out_shape=jax.ShapeDtypeStruct((b, l, ch), BF16),
        scratch_shapes=[pltpu.VMEM((b, tm, gp), F32), pltpu.VMEM((b, tm, gp), F32),
                        pltpu.VMEM((b, nchunk, gp), F32), pltpu.VMEM((b, nchunk, gp), F32),
                        pltpu.VMEM((b, 1, gp), F32), pltpu.VMEM((b, 1, gp), F32)],
        compiler_params=_params(("arbitrary",), nbytes),
        name="ssm_branch",
    )(u, perm, perm_t, b_op, c_op, pow_r, pow_i, d, gw_bf, bgn)


def _mix_out(x, sb, cv, sm, bg, w_ref):
    s1 = SB_WIDTH
    s2 = SB_WIDTH + CONV_CH
    y = _wdot(_rms_rows(sb, bg).astype(BF16), w_ref[0:s1, :])
    y = y + _wdot(cv, w_ref[s1:s2, :])
    y = y + _wdot(sm, w_ref[s2:, :])
    return x + y


def _mem_kv_kernel(m_ref, g_ref, wk_ref, wv_ref, kg_ref, k_ref, v_ref):
    hm = _rms_rows(m_ref[0], g_ref[...]).astype(BF16)
    kk = _wdot(hm, wk_ref[...])
    for hh in range(XA_HEADS):
        hs = slice(hh * XA_HEAD_DIM, (hh + 1) * XA_HEAD_DIM)
        k_ref[0, :, hs] = _rms_rows(kk[:, hs], kg_ref[...]).astype(BF16)
    v_ref[0] = _wdot(hm, wv_ref[...]).astype(BF16)


def _mem_kv(mem, layer, g, wk_bf, wv_bf, kg):
    b, n, d = mem.shape
    blk = pl.BlockSpec((1, n, d), lambda bb: (bb, 0, 0))
    nbytes = 2 * (n * d * 4 + 2 * d * d * 2 + 2 * n * d * 2) + 4 * n * d * 4
    return pl.pallas_call(
        _mem_kv_kernel,
        grid=(b,),
        in_specs=[blk] + [_layer_block(a, layer) for a in (g, wk_bf, wv_bf, kg)],
        out_specs=[blk, blk],
        out_shape=[jax.ShapeDtypeStruct((b, n, d), BF16), jax.ShapeDtypeStruct((b, n, d), BF16)],
        compiler_params=_params(("parallel",), nbytes),
        name="mem_kv",
    )(mem, g, wk_bf, wv_bf, kg)


def _xattn_kernel(x_ref, sb_ref, cv_ref, sm_ref, bgn_ref, wm_ref, g_ref, wq_ref, qg_ref, wo_ref, k_ref, v_ref,
                  o_ref, ob_scr):
    x = _mix_out(x_ref[0], sb_ref[0], cv_ref[0], sm_ref[0], bgn_ref[:, :SB_WIDTH], wm_ref)
    hx = _rms_rows(x, g_ref[...]).astype(BF16)
    q = _wdot(hx, wq_ref[...])
    scale = XA_HEAD_DIM ** -0.5
    for hh in range(XA_HEADS):
        hs = slice(hh * XA_HEAD_DIM, (hh + 1) * XA_HEAD_DIM)
        qh = (_rms_rows(q[:, hs], qg_ref[...]) * scale).astype(BF16)
        s = _dot_nt(qh, k_ref[0, :, hs])
        s = s - jnp.max(s, axis=-1, keepdims=True)
        e = jnp.exp(s)
        p = e / jnp.sum(e, axis=-1, keepdims=True)
        ob_scr[:, hs] = _dot(p.astype(BF16), v_ref[0, :, hs]).astype(BF16)
    o_ref[0] = x + _wdot(ob_scr[...], wo_ref[...])


def _mix_xattn(x, sb, cv, sm, layer, bgn, wm_bf, g, wq_bf, qg, wo_bf, k_bf, v_bf, tm):
    b, l, d = x.shape
    n = k_bf.shape[1]
    rows = lambda w: pl.BlockSpec((1, tm, w), lambda bb, i: (bb, i, 0))
    kv = pl.BlockSpec((1, n, d), lambda bb, i: (bb, 0, 0))
    nbytes = (2 * (2 * tm * d * 4 + tm * SB_WIDTH * 4 + tm * (CONV_CH + SSM_CH) * 2 + 3 * d * d * 2 + 2 * n * d * 2)
              + 5 * tm * d * 4)
    return pl.pallas_call(
        _xattn_kernel,
        grid=(b, l // tm),
        in_specs=[rows(d), rows(SB_WIDTH), rows(CONV_CH), rows(SSM_CH)]
        + [_layer_block(a, layer) for a in (bgn, wm_bf, g, wq_bf, qg, wo_bf)] + [kv, kv],
        out_specs=rows(d),
        out_shape=jax.ShapeDtypeStruct((b, l, d), F32),
        scratch_shapes=[pltpu.VMEM((tm, d), BF16)],
        compiler_params=_params(("parallel", "parallel"), nbytes),
        name="mix_xattn",
    )(x, sb, cv, sm, bgn, wm_bf, g, wq_bf, qg, wo_bf, k_bf, v_bf)


def _ffn_kernel(x_ref, g_ref, wi_hbm, wo_hbm, o_ref, wi_scr, wo_scr, sem, *, th, layer):
    hidden = wo_scr.shape[0]
    slices = [(c0, min(c0 + th, hidden)) for c0 in range(0, hidden, th)]

    def copies(n):
        c0, c1 = slices[n]
        return (pltpu.make_async_copy(wi_hbm.at[layer, :, c0:c1], wi_scr.at[:, c0:c1], sem.at[0, n]),
                pltpu.make_async_copy(wi_hbm.at[layer, :, hidden + c0:hidden + c1],
                                      wi_scr.at[:, hidden + c0:hidden + c1], sem.at[1, n]),
                pltpu.make_async_copy(wo_hbm.at[layer, c0:c1, :], wo_scr.at[c0:c1, :], sem.at[2, n]))

    def body(first_step):
        x = x_ref[...]
        h = _rms_rows(x, g_ref[...]).astype(BF16)
        o_ref[...] = x
        for n, (c0, c1) in enumerate(slices):
            if first_step:
                for cp in copies(n):
                    cp.wait()
            gate = _wdot(h, wi_scr[:, c0:c1])
            up = _wdot(h, wi_scr[:, hidden + c0:hidden + c1])
            act = (gate * _sigmoid(gate) * up).astype(BF16)
            o_ref[...] += _wdot(act, wo_scr[c0:c1, :])

    @pl.when(pl.program_id(0) == 0)
    def _():
        for n in range(len(slices)):
            for cp in copies(n):
                cp.start()
        body(True)

    @pl.when(pl.program_id(0) > 0)
    def _():
        body(False)


def _ffn(x2, layer, g, w_in, w_out, tm, th):
    t, d = x2.shape
    hidden = w_out.shape[1]
    n_slices = -(-hidden // th)
    nbytes = 3 * d * hidden * 4 + 2 * 2 * tm * d * 4 + tm * d * 2 + 6 * tm * th * 4
    return pl.pallas_call(
        functools.partial(_ffn_kernel, th=th, layer=layer),
        grid=(t // tm,),
        in_specs=[pl.BlockSpec((tm, d), lambda i: (i, 0)), _layer_block(g, layer),
                  pl.BlockSpec(memory_space=pl.ANY), pl.BlockSpec(memory_space=pl.ANY)],
        out_specs=pl.BlockSpec((tm, d), lambda i: (i, 0)),
        out_shape=jax.ShapeDtypeStruct((t, d), F32),
        scratch_shapes=[pltpu.VMEM((d, 2 * hidden), F32), pltpu.VMEM((hidden, d), F32),
                        pltpu.SemaphoreType.DMA((3, n_slices))],
        compiler_params=_params(("arbitrary",), nbytes),
        name="ffn",
    )(x2, g, w_in, w_out)


def _tile(n, want):
    want = min(want, n)
    for cand in range(want, 0, -1):
        if n % cand == 0 and (cand % V7X_SUBLANES == 0 or cand == n):
            return cand
    return n


def kernel(x, mem, norm_mix_g, w_in, sb_q_norm_g, sb_k_norm_g, conv_dw_w, conv_dw_b, conv_ln_g, conv_ln_b, conv_pw2_w, ssm_lam_re, ssm_lam_im, ssm_log_dt, ssm_b_re, ssm_b_im, ssm_c_re, ssm_c_im, ssm_d, ssm_glu_w, branch_norm_g, w_out, norm_xa_g, norm_mem_g, xa_wq, xa_wk, xa_wv, xa_q_norm_g, xa_k_norm_g, xa_wo, norm_ffn_g, ffn_w_in, ffn_w_out):
    bsz, seq, d = x.shape
    depth = w_in.shape[0]
    t = bsz * seq
    tm = _tile(seq, 512)
    tx = _tile(seq, 1024)
    tq = _tile(seq, 256)
    ts = _tile(seq, 512)
    th = 256
    assert ts % (SSM_CHUNK * V7X_SUBLANES) == 0 and tm >= CONV_HALO
    vec = lambda a: a.astype(F32).reshape(depth, 1, -1)
    qg = vec(jnp.tile(sb_q_norm_g, (1, SB_HEADS))) * (SB_HEAD_DIM ** -0.5)
    kg = vec(jnp.tile(sb_k_norm_g, (1, SB_HEADS)))
    conv = (conv_dw_w.astype(F32), vec(conv_dw_b), vec(conv_ln_g), vec(conv_ln_b), conv_pw2_w.astype(F32))
    ops = jax.vmap(_ssm_operators)(ssm_lam_re, ssm_lam_im, ssm_log_dt, ssm_b_re, ssm_b_im, ssm_c_re, ssm_c_im)
    bgn = vec(branch_norm_g)

    seg = (jnp.arange(SEG_WIDTH)[:, None] // SB_HEAD_DIM == jnp.arange(SEG_WIDTH)[None, :] // SB_HEAD_DIM).astype(BF16)
    ntri = -(jnp.arange(tq)[:, None] > jnp.arange(tq)[None, :]).astype(BF16)

    x2 = x.reshape(t, d)
    for l in range(depth):
        q, k, v, o_conv, u = _mix_in(x2, seq, l, vec(norm_mix_g), w_in.astype(F32), qg, kg, seg, conv, bgn, tx)
        o_conv = o_conv.reshape(bsz, seq, CONV_CH)
        o_sb = _sb_attn(q.reshape(bsz, seq, SB_WIDTH), k.reshape(bsz, seq, SB_WIDTH),
                        v.reshape(bsz, seq, SB_WIDTH), ntri, tq)
        o_ssm = _ssm_branch(u.reshape(bsz, seq, SSM_CH), l, ops, vec(ssm_d), ssm_glu_w.astype(F32), bgn, ts)
        k_m, v_m = _mem_kv(mem, l, vec(norm_mem_g), xa_wk.astype(F32), xa_wv.astype(F32), vec(xa_k_norm_g))
        x2 = _mix_xattn(x2.reshape(bsz, seq, d), o_sb, o_conv, o_ssm, l, bgn, w_out.astype(F32), vec(norm_xa_g),
                        xa_wq.astype(F32), vec(xa_q_norm_g), xa_wo.astype(F32), k_m, v_m, tx).reshape(t, d)
        x2 = _ffn(x2, l, vec(norm_ffn_g), ffn_w_in.astype(F32), ffn_w_out.astype(F32), tx, th)
    return x2.reshape(bsz, seq, d)
```

```python
import functools

import jax
import jax.numpy as jnp
from jax import lax
from jax.experimental import pallas as pl
from jax.experimental.pallas import tpu as pltpu

F32 = jnp.float32
BF16 = jnp.bfloat16
EPS = 1e-6

V7X_LANES = 128
V7X_SUBLANES = 8
V7X_VMEM_BYTES = 64 * 1024 * 1024

SB_HEADS = 8
SB_HEAD_DIM = 64
SB_WIDTH = SB_HEADS * SB_HEAD_DIM
SEG_WIDTH = 256
CONV_CH = 256
CONV_WIDTH = 31
CONV_HALO = 32
SSM_CH = 256
SSM_GROUP = 16
SSM_GROUPS = SSM_CH // SSM_GROUP
SSM_STATE = 64
SSM_CHUNK = 32
XA_HEADS = 4
XA_HEAD_DIM = 256


def _vmem_limit(nbytes):
    return int(min(max(nbytes * 3 // 2, 16 * 1024 * 1024), V7X_VMEM_BYTES - 8 * 1024 * 1024))


def _params(semantics, nbytes):
    return pltpu.CompilerParams(dimension_semantics=semantics, vmem_limit_bytes=_vmem_limit(nbytes))


def _dot(a, b):
    return jnp.dot(a, b, preferred_element_type=F32)


def _wdot(a, w):
    return jnp.dot(a, w.astype(BF16), preferred_element_type=F32)


def _dot_nt(a, b):
    return lax.dot_general(a, b, (((1,), (1,)), ((), ())), preferred_element_type=F32)


def _rms_rows(xf, g):
    return xf * lax.rsqrt(jnp.mean(xf * xf, axis=-1, keepdims=True) + EPS) * g


def _sigmoid(x):
    return 1.0 / (1.0 + jnp.exp(-x))


def _segment_mean_sq(p, seg):
    sq = p * p
    hi = sq.astype(BF16)
    lo = (sq - hi.astype(F32)).astype(BF16)
    parts = []
    for c0 in range(0, p.shape[1], SEG_WIDTH):
        cols = slice(c0, c0 + SEG_WIDTH)
        parts.append(_dot(hi[:, cols], seg) + _dot(lo[:, cols], seg))
    return jnp.concatenate(parts, axis=1) * (1.0 / SB_HEAD_DIM)


def _mix_in_kernel(x_ref, g_ref, w_ref, qg_ref, kg_ref, seg_ref, dww_ref, dwb_ref, lng_ref, lnb_ref, pw_ref, bgn_ref,
                   q_ref, k_ref, v_ref, cv_ref, u_ref, hbuf, sh_scr, *, tm, tiles_per_seq):
    i = pl.program_id(0)

    @pl.when(i == 0)
    def _():
        hbuf[...] = jnp.zeros_like(hbuf)

    h = _rms_rows(x_ref[...], g_ref[...]).astype(BF16)
    s1, s2, s3 = SB_WIDTH, 2 * SB_WIDTH, 3 * SB_WIDTH
    s4 = s3 + 2 * CONV_CH
    seg = seg_ref[...]
    pc = _wdot(h, w_ref[:, s3:s4])
    history = hbuf[tm:tm + CONV_HALO, :]
    hbuf[:CONV_HALO, :] = jnp.where(i % tiles_per_seq == 0, 0.0, history)
    hbuf[CONV_HALO:, :] = pc[:, :CONV_CH] * _sigmoid(pc[:, CONV_CH:])
    quarter = -(-CONV_WIDTH // 4)
    taps = [range(n * quarter, min((n + 1) * quarter, CONV_WIDTH)) for n in range(4)]
    pq = _wdot(h, w_ref[:, 0:s1])
    _conv_shift(hbuf, sh_scr, tm)
    acc = _conv_taps(hbuf, sh_scr, dww_ref, jnp.zeros((tm, CONV_CH), F32) + dwb_ref[...], taps[0], tm)
    q_ref[...] = (pq * lax.rsqrt(_segment_mean_sq(pq, seg) + EPS) * qg_ref[...]).astype(BF16)
    pk = _wdot(h, w_ref[:, s1:s2])
    acc = _conv_taps(hbuf, sh_scr, dww_ref, acc, taps[1], tm)
    k_ref[...] = (pk * lax.rsqrt(_segment_mean_sq(pk, seg) + EPS) * kg_ref[...]).astype(BF16)
    pv = _wdot(h, w_ref[:, s2:s3])
    acc = _conv_taps(hbuf, sh_scr, dww_ref, acc, taps[2], tm)
    v_ref[...] = pv.astype(BF16)
    pu = _wdot(h, w_ref[:, s4:])
    acc = _conv_taps(hbuf, sh_scr, dww_ref, acc, taps[3], tm)
    u_ref[...] = pu
    cv_ref[...] = _conv_tail(acc, lng_ref, lnb_ref, pw_ref, bgn_ref[:, SB_WIDTH:SB_WIDTH + CONV_CH])


def _layer_block(a, layer):
    tail = a.shape[1:]
    return pl.BlockSpec((None,) + tail, lambda *_: (layer,) + (0,) * len(tail), pipeline_mode=pl.Buffered(1))


def _mix_in(x2, seq, layer, g, w_bf, qg, kg, seg, conv, bgn, tm):
    t, d = x2.shape
    n_in = w_bf.shape[2]
    row = lambda i: (i, 0)
    hist = tm + CONV_HALO
    nbytes = (2 * (tm * d * 4 + d * n_in * 2 + tm * (3 * SB_WIDTH * 2 + CONV_CH * 2 + SSM_CH * 4)) + tm * n_in * 4
              + hist * CONV_CH * 4 * 14)
    return pl.pallas_call(
        functools.partial(_mix_in_kernel, tm=tm, tiles_per_seq=seq // tm),
        grid=(t // tm,),
        in_specs=[pl.BlockSpec((tm, d), row)]
        + [_layer_block(a, layer) for a in (g, w_bf, qg, kg)]
        + [pl.BlockSpec((SEG_WIDTH, SEG_WIDTH), lambda i: (0, 0))]
        + [_layer_block(a, layer) for a in (*conv, bgn)],
        out_specs=[pl.BlockSpec((tm, SB_WIDTH), row), pl.BlockSpec((tm, SB_WIDTH), row),
                   pl.BlockSpec((tm, SB_WIDTH), row), pl.BlockSpec((tm, CONV_CH), row),
                   pl.BlockSpec((tm, SSM_CH), row)],
        out_shape=[jax.ShapeDtypeStruct((t, SB_WIDTH), BF16), jax.ShapeDtypeStruct((t, SB_WIDTH), BF16),
                   jax.ShapeDtypeStruct((t, SB_WIDTH), BF16), jax.ShapeDtypeStruct((t, CONV_CH), BF16),
                   jax.ShapeDtypeStruct((t, SSM_CH), F32)],
        scratch_shapes=[pltpu.VMEM((hist, CONV_CH), F32),
                        pltpu.VMEM((V7X_SUBLANES - 1, hist - V7X_SUBLANES, CONV_CH), F32)],
        compiler_params=_params(("arbitrary",), nbytes),
        name="mix_in",
    )(x2, g, w_bf, qg, kg, seg, *conv, bgn)


SB_SLOTS = 2
SB_GROUPS_PER_STEP = 2
SB_QBLOCKS_PER_STEP = 2
SB_SOFTPLUS_LINEAR = 40.0
SB_EXP_UNDERFLOW = -104.0
SB_NEVER = -1e30


def _sb_decay(z, run, ntri, mask):
    sp = jnp.maximum(jnp.log(1.0 + jnp.exp(jnp.minimum(z, SB_SOFTPLUS_LINEAR))), z)
    if mask is not None:
        sp = jnp.where(mask, sp, 0.0)
    later = _dot(sp.astype(BF16), ntri)
    return z - sp + run, later, run + later[:, 0:1] - sp[:, 0:1]


def _sb_weights(t0, later, mask):
    w = jnp.exp(t0 + later)
    if mask is not None:
        w = jnp.where(mask, w, 0.0)
    return w.astype(BF16)


def _sb_attn_kernel(q_ref, k_ref, v_ref, ntri_ref, o_ref, q2_scr, z_scr, t0_scr, lat_scr, run_scr, acc_scr, *, tq):
    step = pl.program_id(2)
    nh = V7X_LANES // SB_HEAD_DIM
    m = nh * tq
    lanes = [slice(g * V7X_LANES, (g + 1) * V7X_LANES) for g in range(SB_GROUPS_PER_STEP)]
    qblocks = [step * SB_QBLOCKS_PER_STEP + s for s in range(SB_QBLOCKS_PER_STEP)]
    units = [(s, g) for s in range(SB_QBLOCKS_PER_STEP) for g in range(SB_GROUPS_PER_STEP)]

    def rows(s, j):
        i = qblocks[s]
        return pl.ds(pl.multiple_of((i - jnp.minimum(j, i)) * tq, tq), tq)

    def scores(us, j, slot):
        for s, g in us:
            z_scr[s, g, slot] = _dot_nt(q2_scr[s, g], k_ref[0, rows(s, j), lanes[g]])

    def decay(us, slot, mask=None, bias=None):
        for s, g in us:
            run = run_scr[s, g] if bias is None else run_scr[s, g] + bias[s]
            t0, later, run = _sb_decay(z_scr[s, g, slot], run, ntri_ref[...], mask)
            t0_scr[s, g, slot] = t0
            lat_scr[s, g, slot] = later
            run_scr[s, g] = run

    def output(us, j, slot, mask=None):
        for s, g in us:
            w = _sb_weights(t0_scr[s, g, slot], lat_scr[s, g, slot], mask)
            acc_scr[s, g] += _dot(w, v_ref[0, rows(s, j), lanes[g]])

    def live(us):
        run = run_scr[us[0]]
        for u in us[1:]:
            run = jnp.maximum(run, run_scr[u])
        return (jnp.max(run) >= SB_EXP_UNDERFLOW).astype(jnp.int32)

    lane_head = lax.broadcasted_iota(jnp.int32, (tq, V7X_LANES), 1) // SB_HEAD_DIM
    for s, g in units:
        q = q_ref[0, s * tq:(s + 1) * tq, lanes[g]]
        for h in range(nh):
            q2_scr[s, g, h * tq:(h + 1) * tq, :] = jnp.where(lane_head == h, q, jnp.zeros_like(q))
    row = lax.broadcasted_iota(jnp.int32, (m, tq), 0) % tq
    col = lax.broadcasted_iota(jnp.int32, (m, tq), 1)
    diag_mask = col < row
    run_scr[...] = jnp.zeros_like(run_scr)
    acc_scr[...] = jnp.zeros_like(acc_scr)

    scores(units, 0, 0)
    scores(units, 1, 1)
    decay(units, 0, mask=diag_mask)
    decay(units, 1, bias=[jnp.where(i >= 1, 0.0, SB_NEVER) for i in qblocks])
    output(units, 0, 0, diag_mask)
    output(units, 1, 1)

    for s, i in enumerate(qblocks):
        us = [u for u in units if u[0] == s]

        @pl.when(jnp.logical_and(i >= 2, live(us) > 0))
        def _(us=us, i=i):
            scores(us, 2, 0)
            decay(us, 0)
            scores(us, 3, 1)
            more = live(us)
            output(us, 2, 0)
            scores(us, 4, 0)
            decay(us, 1)
            n_pairs = (i - 2) // 2

            def cond(carry):
                p, more = carry
                return jnp.logical_and(p < n_pairs, more > 0)

            def body(carry):
                p, _ = carry
                t = 2 * p + 3
                scores(us, t + 2, 1)
                output(us, t, 1)
                decay(us, 0)
                more = live(us)
                scores(us, t + 3, 0)
                output(us, t + 1, 0)
                decay(us, 1)
                return p + 1, more

            _, more = lax.while_loop(cond, body, (0, more))

            @pl.when(jnp.logical_and(more > 0, (i - 2) % 2 == 1))
            def _():
                output(us, i, 1)

    for s, g in units:
        out = acc_scr[s, g, 0:tq, :]
        for h in range(1, nh):
            out = jnp.where(lane_head == h, acc_scr[s, g, h * tq:(h + 1) * tq, :], out)
        o_ref[0, s * tq:(s + 1) * tq, lanes[g]] = out


def _sb_attn(q, k, v, ntri, tq):
    b, l, w = q.shape
    ns, ng = SB_QBLOCKS_PER_STEP, SB_GROUPS_PER_STEP
    gw = ng * V7X_LANES
    assert l % (ns * tq) == 0 and w % gw == 0
    m = (V7X_LANES // SB_HEAD_DIM) * tq
    blk = pltpu.VMEM((ns, ng, SB_SLOTS, m, tq), F32)
    nbytes = (2 * (2 * l * gw * 2 + ns * tq * gw * 2 + ns * tq * gw * 4 + tq * tq * 2)
              + ns * ng * (3 * SB_SLOTS * m * tq * 4 + 3 * m * V7X_LANES * 4) + 6 * m * tq * 4)
    return pl.pallas_call(
        functools.partial(_sb_attn_kernel, tq=tq),
        grid=(b, w // gw, l // (ns * tq)),
        in_specs=[pl.BlockSpec((1, ns * tq, gw), lambda bb, hp, i: (bb, i, hp)),
                  pl.BlockSpec((1, l, gw), lambda bb, hp, i: (bb, 0, hp)),
                  pl.BlockSpec((1, l, gw), lambda bb, hp, i: (bb, 0, hp)),
                  pl.BlockSpec((tq, tq), lambda bb, hp, i: (0, 0))],
        out_specs=pl.BlockSpec((1, ns * tq, gw), lambda bb, hp, i: (bb, i, hp)),
        out_shape=jax.ShapeDtypeStruct((b, l, w), F32),
        scratch_shapes=[pltpu.VMEM((ns, ng, m, V7X_LANES), BF16), blk, blk, blk,
                        pltpu.VMEM((ns, ng, m, 1), F32), pltpu.VMEM((ns, ng, m, V7X_LANES), F32)],
        compiler_params=_params(("parallel", "parallel", "arbitrary"), nbytes),
        name="sb_attn",
    )(q, k, v, ntri)


def _conv_shift(hbuf, sh_scr, tc):
    span = tc + CONV_HALO - V7X_SUBLANES
    for ph in range(1, V7X_SUBLANES):
        sh_scr[ph - 1] = hbuf[ph:ph + span, :]


def _conv_taps(hbuf, sh_scr, dww_ref, acc, taps, tc):
    off = CONV_HALO - (CONV_WIDTH - 1)
    for j in taps:
        ph = (off + j) % V7X_SUBLANES
        base = off + j - ph
        tap = hbuf[base:base + tc, :] if ph == 0 else sh_scr[ph - 1, base:base + tc, :]
        acc = acc + dww_ref[j:j + 1, :] * tap
    return acc


def _conv_tail(acc, lng_ref, lnb_ref, pw_ref, bg):
    mu = jnp.mean(acc, axis=-1, keepdims=True)
    cen = acc - mu
    var = jnp.mean(cen * cen, axis=-1, keepdims=True)
    y = cen * lax.rsqrt(var + EPS) * lng_ref[...] + lnb_ref[...]
    y = y * _sigmoid(y)
    o = _wdot(y.astype(BF16), pw_ref[...])
    return _rms_rows(o, bg).astype(BF16)


def _ssm_operators(lam_re, lam_im, log_dt, b_re, b_im, c_re, c_im):
    lr, li = lam_re.astype(F32), lam_im.astype(F32)
    dt = jnp.exp(log_dt.astype(F32))[:, None]
    mag = jnp.exp(lr * dt)
    ar, ai = mag * jnp.cos(li * dt), mag * jnp.sin(li * dt)
    den = lr * lr + li * li
    fr = ((ar - 1.0) * lr + ai * li) / den
    fi = (ai * lr - (ar - 1.0) * li) / den
    br, bi = b_re.astype(F32), b_im.astype(F32)
    bbr = fr[..., None] * br - fi[..., None] * bi
    bbi = fr[..., None] * bi + fi[..., None] * br
    cr, ci = c_re.astype(F32), c_im.astype(F32)
    eye = jnp.eye(SSM_GROUPS, dtype=F32)
    gp = SSM_GROUPS * SSM_STATE

    def rows_gh(w):
        return jnp.einsum('gph,gk->ghkp', w, eye).reshape(SSM_CH, gp)

    def rows_gp(w):
        return jnp.einsum('ghp,gk->gpkh', w, eye).reshape(gp, SSM_CH)

    b_op = jnp.concatenate([rows_gh(bbr), rows_gh(bbi)], axis=1)
    c_op = jnp.concatenate([rows_gp(cr), rows_gp(-ci)], axis=0)
    n = jnp.arange(1, SSM_CHUNK + 1, dtype=F32)[:, None]
    pmag = jnp.exp(n * (lr * dt).reshape(1, gp))
    ang = n * (li * dt).reshape(1, gp)
    return b_op.astype(BF16), c_op.astype(BF16), pmag * jnp.cos(ang), pmag * jnp.sin(ang)


def _ssm_kernel(u_ref, perm_ref, permt_ref, b_ref, c_ref, pr_ref, pi_ref, d_ref, gw_ref, bgn_ref, o_ref,
                xr_scr, xi_scr, er_scr, ei_scr, cr_scr, ci_scr, *, tm):
    gp = SSM_GROUPS * SSM_STATE
    nchunk = tm // SSM_CHUNK

    @pl.when(pl.program_id(0) == 0)
    def _():
        cr_scr[...] = jnp.zeros_like(cr_scr)
        ci_scr[...] = jnp.zeros_like(ci_scr)

    def pos(s):
        return pl.ds(s * nchunk, nchunk)

    def inject(b):
        up = _dot(perm_ref[...], u_ref[b].astype(BF16)).astype(BF16)
        bu = _dot(up, b_ref[...])
        xr_scr[b] = bu[:, :gp]
        xi_scr[b] = bu[:, gp:]

    def scan(b):
        ar, ai = pr_ref[0:1, :], pi_ref[0:1, :]
        xr, xi = xr_scr[b, pos(0), :], xi_scr[b, pos(0), :]
        for s in range(1, SSM_CHUNK):
            xr, xi = (ar * xr - ai * xi + xr_scr[b, pos(s), :], ar * xi + ai * xr + xi_scr[b, pos(s), :])
            xr_scr[b, pos(s), :] = xr
            xi_scr[b, pos(s), :] = xi
        nr, ni = pr_ref[SSM_CHUNK - 1:SSM_CHUNK, :], pi_ref[SSM_CHUNK - 1:SSM_CHUNK, :]
        er, ei = cr_scr[b], ci_scr[b]
        for c in range(nchunk):
            er_scr[b, c:c + 1, :] = er
            ei_scr[b, c:c + 1, :] = ei
            er, ei = (nr * er - ni * ei + xr[c:c + 1, :], nr * ei + ni * er + xi[c:c + 1, :])
        cr_scr[b] = er
        ci_scr[b] = ei
        er, ei = er_scr[b], ei_scr[b]
        for s in range(SSM_CHUNK):
            sr, si = pr_ref[s:s + 1, :], pi_ref[s:s + 1, :]
            xr_scr[b, pos(s), :] += sr * er - si * ei
            xi_scr[b, pos(s), :] += sr * ei + si * er

    def readout(b):
        yp = _dot(xr_scr[b].astype(BF16), c_ref[0:gp, :]) + _dot(xi_scr[b].astype(BF16), c_ref[gp:, :])
        hi = yp.astype(BF16)
        lo = (yp - hi.astype(F32)).astype(BF16)
        y = _dot(permt_ref[...], hi) + _dot(permt_ref[...], lo) + d_ref[...] * u_ref[b]
        z = _wdot(y.astype(BF16), gw_ref[...])
        bg = bgn_ref[:, SB_WIDTH + CONV_CH:]
        o_ref[b] = _rms_rows(z[:, :SSM_CH] * _sigmoid(z[:, SSM_CH:]), bg).astype(BF16)

    nb = u_ref.shape[0]
    for b in range(nb):
        inject(b)
    for b in range(nb):
        scan(b)
        readout(b)


def _ssm_branch(u, layer, ops, d, gw_bf, bgn, tm):
    b_op, c_op, pow_r, pow_i = ops
    b, l, ch = u.shape
    gp = SSM_GROUPS * SSM_STATE
    nchunk = tm // SSM_CHUNK
    full = lambda a: pl.BlockSpec(a.shape, lambda i: (0,) * a.ndim)
    r = jnp.arange(tm)
    perm = (r[None, :] == ((r % nchunk) * SSM_CHUNK + r // nchunk)[:, None]).astype(BF16)
    perm_t = perm.T
    nbytes = (2 * (2 * b * tm * ch * 4 + 2 * tm * tm * 2 + 2 * ch * gp * 2 * 2 + 2 * SSM_CHUNK * gp * 4 + ch * 2 * ch * 2)
              + b * (2 * tm * gp * 4 + 2 * nchunk * gp * 4 + 2 * tm * 2 * gp * 4))
    return pl.pallas_call(
        functools.partial(_ssm_kernel, tm=tm),
        grid=(l // tm,),
        in_specs=[pl.BlockSpec((b, tm, ch), lambda i: (0, i, 0)), full(perm), full(perm_t)]
        + [_layer_block(a, layer) for a in (b_op, c_op, pow_r, pow_i, d, gw_bf, bgn)],
        out_specs=pl.BlockSpec((b, tm, ch), lambda i: (0, i, 0)),
        out_shape=jax.ShapeDtypeStruct((b, l, ch), BF16),
        scratch_shapes=[pltpu.VMEM((b, tm, gp), F32), pltpu.VMEM((b, tm, gp), F32),
                        pltpu.VMEM((b, nchunk, gp), F32), pltpu.VMEM((b, nchunk, gp), F32),
                        pltpu.VMEM((b, 1, gp), F32), pltpu.VMEM((b, 1, gp), F32)],
        compiler_params=_params(("arbitrary",), nbytes),
        name="ssm_branch",
    )(u, perm, perm_t, b_op, c_op, pow_r, pow_i, d, gw_bf, bgn)


def _mix_out(x, sb, cv, sm, bg, w_ref):
    s1 = SB_WIDTH
    s2 = SB_WIDTH + CONV_CH
    y = _wdot(_rms_rows(sb, bg).astype(BF16), w_ref[0:s1, :])
    y = y + _wdot(cv, w_ref[s1:s2, :])
    y = y + _wdot(sm, w_ref[s2:, :])
    return x + y


def _mem_kv_kernel(m_ref, g_ref, wk_ref, wv_ref, kg_ref, k_ref, v_ref):
    hm = _rms_rows(m_ref[0], g_ref[...]).astype(BF16)
    kk = _wdot(hm, wk_ref[...])
    for hh in range(XA_HEADS):
        hs = slice(hh * XA_HEAD_DIM, (hh + 1) * XA_HEAD_DIM)
        k_ref[0, :, hs] = _rms_rows(kk[:, hs], kg_ref[...]).astype(BF16)
    v_ref[0] = _wdot(hm, wv_ref[...]).astype(BF16)


def _mem_kv(mem, layer, g, wk_bf, wv_bf, kg):
    b, n, d = mem.shape
    blk = pl.BlockSpec((1, n, d), lambda bb: (bb, 0, 0))
    nbytes = 2 * (n * d * 4 + 2 * d * d * 2 + 2 * n * d * 2) + 4 * n * d * 4
    return pl.pallas_call(
        _mem_kv_kernel,
        grid=(b,),
        in_specs=[blk] + [_layer_block(a, layer) for a in (g, wk_bf, wv_bf, kg)],
        out_specs=[blk, blk],
        out_shape=[jax.ShapeDtypeStruct((b, n, d), BF16), jax.ShapeDtypeStruct((b, n, d), BF16)],
        compiler_params=_params(("parallel",), nbytes),
        name="mem_kv",
    )(mem, g, wk_bf, wv_bf, kg)


def _xattn_kernel(x_ref, sb_ref, cv_ref, sm_ref, bgn_ref, wm_ref, g_ref, wq_ref, qg_ref, wo_ref, k_ref, v_ref,
                  o_ref, ob_scr):
    x = _mix_out(x_ref[0], sb_ref[0], cv_ref[0], sm_ref[0], bgn_ref[:, :SB_WIDTH], wm_ref)
    hx = _rms_rows(x, g_ref[...]).astype(BF16)
    q = _wdot(hx, wq_ref[...])
    scale = XA_HEAD_DIM ** -0.5
    for hh in range(XA_HEADS):
        hs = slice(hh * XA_HEAD_DIM, (hh + 1) * XA_HEAD_DIM)
        qh = (_rms_rows(q[:, hs], qg_ref[...]) * scale).astype(BF16)
        s = _dot_nt(qh, k_ref[0, :, hs])
        s = s - jnp.max(s, axis=-1, keepdims=True)
        e = jnp.exp(s)
        p = e / jnp.sum(e, axis=-1, keepdims=True)
        ob_scr[:, hs] = _dot(p.astype(BF16), v_ref[0, :, hs]).astype(BF16)
    o_ref[0] = x + _wdot(ob_scr[...], wo_ref[...])


def _mix_xattn(x, sb, cv, sm, layer, bgn, wm_bf, g, wq_bf, qg, wo_bf, k_bf, v_bf, tm):
    b, l, d = x.shape
    n = k_bf.shape[1]
    rows = lambda w: pl.BlockSpec((1, tm, w), lambda bb, i: (bb, i, 0))
    kv = pl.BlockSpec((1, n, d), lambda bb, i: (bb, 0, 0))
    nbytes = (2 * (2 * tm * d * 4 + tm * SB_WIDTH * 4 + tm * (CONV_CH + SSM_CH) * 2 + 3 * d * d * 2 + 2 * n * d * 2)
              + 5 * tm * d * 4)
    return pl.pallas_call(
        _xattn_kernel,
        grid=(b, l // tm),
        in_specs=[rows(d), rows(SB_WIDTH), rows(CONV_CH), rows(SSM_CH)]
        + [_layer_block(a, layer) for a in (bgn, wm_bf, g, wq_bf, qg, wo_bf)] + [kv, kv],
        out_specs=rows(d),
        out_shape=jax.ShapeDtypeStruct((b, l, d), F32),
        scratch_shapes=[pltpu.VMEM((tm, d), BF16)],
        compiler_params=_params(("parallel", "parallel"), nbytes),
        name="mix_xattn",
    )(x, sb, cv, sm, bgn, wm_bf, g, wq_bf, qg, wo_bf, k_bf, v_bf)


def _ffn_kernel(x_ref, g_ref, wi_hbm, wo_hbm, o_ref, wi_scr, wo_scr, sem, *, th, layer):
    hidden = wo_scr.shape[0]
    slices = [(c0, min(c0 + th, hidden)) for c0 in range(0, hidden, th)]

    def copies(n):
        c0, c1 = slices[n]
        return (pltpu.make_async_copy(wi_hbm.at[layer, :, c0:c1], wi_scr.at[:, c0:c1], sem.at[0, n]),
                pltpu.make_async_copy(wi_hbm.at[layer, :, hidden + c0:hidden + c1],
                                      wi_scr.at[:, hidden + c0:hidden + c1], sem.at[1, n]),
                pltpu.make_async_copy(wo_hbm.at[layer, c0:c1, :], wo_scr.at[c0:c1, :], sem.at[2, n]))

    def body(first_step):
        x = x_ref[...]
        h = _rms_rows(x, g_ref[...]).astype(BF16)
        o_ref[...] = x
        for n, (c0, c1) in enumerate(slices):
            if first_step:
                for cp in copies(n):
                    cp.wait()
            gate = _wdot(h, wi_scr[:, c0:c1])
            up = _wdot(h, wi_scr[:, hidden + c0:hidden + c1])
            act = (gate * _sigmoid(gate) * up).astype(BF16)
            o_ref[...] += _wdot(act, wo_scr[c0:c1, :])

    @pl.when(pl.program_id(0) == 0)
    def _():
        for n in range(len(slices)):
            for cp in copies(n):
                cp.start(priority=n % 2)
        body(True)

    @pl.when(pl.program_id(0) > 0)
    def _():
        body(False)


def _ffn(x2, layer, g, w_in, w_out, tm, th):
    t, d = x2.shape
    hidden = w_out.shape[1]
    n_slices = -(-hidden // th)
    nbytes = 3 * d * hidden * 4 + 2 * 2 * tm * d * 4 + tm * d * 2 + 6 * tm * th * 4
    return pl.pallas_call(
        functools.partial(_ffn_kernel, th=th, layer=layer),
        grid=(t // tm,),
        in_specs=[pl.BlockSpec((tm, d), lambda i: (i, 0)), _layer_block(g, layer),
                  pl.BlockSpec(memory_space=pl.ANY), pl.BlockSpec(memory_space=pl.ANY)],
        out_specs=pl.BlockSpec((tm, d), lambda i: (i, 0)),
        out_shape=jax.ShapeDtypeStruct((t, d), F32),
        scratch_shapes=[pltpu.VMEM((d, 2 * hidden), F32), pltpu.VMEM((hidden, d), F32),
                        pltpu.SemaphoreType.DMA((3, n_slices))],
        compiler_params=_params(("arbitrary",), nbytes),
        name="ffn",
    )(x2, g, w_in, w_out)


def _tile(n, want):
    want = min(want, n)
    for cand in range(want, 0, -1):
        if n % cand == 0 and (cand % V7X_SUBLANES == 0 or cand == n):
            return cand
    return n


def kernel(x, mem, norm_mix_g, w_in, sb_q_norm_g, sb_k_norm_g, conv_dw_w, conv_dw_b, conv_ln_g, conv_ln_b, conv_pw2_w, ssm_lam_re, ssm_lam_im, ssm_log_dt, ssm_b_re, ssm_b_im, ssm_c_re, ssm_c_im, ssm_d, ssm_glu_w, branch_norm_g, w_out, norm_xa_g, norm_mem_g, xa_wq, xa_wk, xa_wv, xa_q_norm_g, xa_k_norm_g, xa_wo, norm_ffn_g, ffn_w_in, ffn_w_out):
    bsz, seq, d = x.shape
    depth = w_in.shape[0]
    t = bsz * seq
    tm = _tile(seq, 512)
    tx = _tile(seq, 1024)
    tq = _tile(seq, 256)
    ts = _tile(seq, 512)
    th = 256
    assert ts % (SSM_CHUNK * V7X_SUBLANES) == 0 and tm >= CONV_HALO
    vec = lambda a: a.astype(F32).reshape(depth, 1, -1)
    qg = vec(jnp.tile(sb_q_norm_g, (1, SB_HEADS))) * (SB_HEAD_DIM ** -0.5)
    kg = vec(jnp.tile(sb_k_norm_g, (1, SB_HEADS)))
    conv = (conv_dw_w.astype(F32), vec(conv_dw_b), vec(conv_ln_g), vec(conv_ln_b), conv_pw2_w.astype(F32))
    ops = jax.vmap(_ssm_operators)(ssm_lam_re, ssm_lam_im, ssm_log_dt, ssm_b_re, ssm_b_im, ssm_c_re, ssm_c_im)
    bgn = vec(branch_norm_g)

    seg = (jnp.arange(SEG_WIDTH)[:, None] // SB_HEAD_DIM == jnp.arange(SEG_WIDTH)[None, :] // SB_HEAD_DIM).astype(BF16)
    ntri = -(jnp.arange(tq)[:, None] > jnp.arange(tq)[None, :]).astype(BF16)

    x2 = x.reshape(t, d)
    for l in range(depth):
        q, k, v, o_conv, u = _mix_in(x2, seq, l, vec(norm_mix_g), w_in.astype(F32), qg, kg, seg, conv, bgn, tx)
        o_conv = o_conv.reshape(bsz, seq, CONV_CH)
        o_sb = _sb_attn(q.reshape(bsz, seq, SB_WIDTH), k.reshape(bsz, seq, SB_WIDTH),
                        v.reshape(bsz, seq, SB_WIDTH), ntri, tq)
        o_ssm = _ssm_branch(u.reshape(bsz, seq, SSM_CH), l, ops, vec(ssm_d), ssm_glu_w.astype(F32), bgn, ts)
        k_m, v_m = _mem_kv(mem, l, vec(norm_mem_g), xa_wk.astype(F32), xa_wv.astype(F32), vec(xa_k_norm_g))
        x2 = _mix_xattn(x2.reshape(bsz, seq, d), o_sb, o_conv, o_ssm, l, bgn, w_out.astype(F32), vec(norm_xa_g),
                        xa_wq.astype(F32), vec(xa_q_norm_g), xa_wo.astype(F32), k_m, v_m, tx).reshape(t, d)
        x2 = _ffn(x2, l, vec(norm_ffn_g), ffn_w_in.astype(F32), ffn_w_out.astype(F32), tm, th)
    return x2.reshape(bsz, seq, d)
```
